```python
import math
import jax, jax.numpy as jnp
from jax import lax
import numpy as np

D_MODEL = 1024
BATCH = 8
SEQ = 4096
DEPTH = 1

D_DN = 512
DN_HEADS = 4
DN_HEAD_DIM = 128
CONV_WIDTH = 4
CHUNK = 64
D_ATT = 512
ATT_HEADS = 8
ATT_HEAD_DIM = 64
DILATED_PATTERNS = ((128, 1), (512, 4), (2048, 16))
N_BUCKETS = 32
MAX_DISTANCE = 2048
D_MIX = D_DN + D_ATT
D_IN = 4 * D_DN + 2 * DN_HEADS + 4 * D_ATT
EPS = 1e-6

kernel_name = "hybrid_deltanet_dilated_attention_layer"


def rms_norm(x, w):
    xf = x.astype(jnp.float32)
    return xf * lax.rsqrt(jnp.mean(xf * xf, axis=-1, keepdims=True) + EPS) * w.astype(jnp.float32)


def l2_norm(x):
    return x * lax.rsqrt(jnp.sum(x * x, axis=-1, keepdims=True) + EPS)


def split_heads(t, n_heads):
    b, s, _ = t.shape
    return t.reshape(b, s, n_heads, -1).transpose(0, 2, 1, 3)


def causal_depthwise_conv(u, w):
    k_width = w.shape[0]
    s = u.shape[1]
    up = jnp.pad(u, ((0, 0), (k_width - 1, 0), (0, 0)))
    y = up[:, 0:s] * w[0]
    for i in range(1, k_width):
        y = y + up[:, i:i + s] * w[i]
    return y


def gated_delta_rule(q, k, v, g, beta):
    b, h, s, dk = q.shape
    dv = v.shape[-1]
    n = s // CHUNK
    q = q * (dk ** -0.5)
    qc = q.reshape(b, h, n, CHUNK, dk)
    kc = k.reshape(b, h, n, CHUNK, dk)
    vc = v.reshape(b, h, n, CHUNK, dv)
    bc = beta.reshape(b, h, n, CHUNK)
    gc = jnp.cumsum(g.reshape(b, h, n, CHUNK), axis=-1)
    tril_incl = np.tril(np.ones((CHUNK, CHUNK), dtype=bool))
    tril_strict = np.tril(np.ones((CHUNK, CHUNK), dtype=bool), -1)
    diff = gc[..., :, None] - gc[..., None, :]
    decay = jnp.exp(jnp.where(tril_incl, diff, -jnp.inf))
    kb = kc * bc[..., None]
    a_mat = jnp.where(tril_strict, jnp.einsum('bhnid,bhnjd->bhnij', kb, kc) * decay, 0.0)
    l_mat = a_mat + jnp.eye(CHUNK, dtype=a_mat.dtype)
    u = lax.linalg.triangular_solve(l_mat, vc * bc[..., None], left_side=True, lower=True, unit_diagonal=True)
    w = lax.linalg.triangular_solve(l_mat, kb * jnp.exp(gc)[..., None], left_side=True, lower=True, unit_diagonal=True)
    attn_intra = jnp.einsum('bhnid,bhnjd->bhnij', qc, kc) * decay
    q_dec = qc * jnp.exp(gc)[..., None]
    k_tail = kc * jnp.exp(gc[..., -1:] - gc)[..., None]
    g_last = jnp.exp(gc[..., -1])
    xs = tuple(jnp.moveaxis(t, 2, 0) for t in (attn_intra, q_dec, k_tail, u, w, g_last))

    def step(state, inp):
        a_i, qd_i, kt_i, u_i, w_i, gl_i = inp
        v_new = u_i - jnp.einsum('bhck,bhkv->bhcv', w_i, state)
        o_i = jnp.einsum('bhck,bhkv->bhcv', qd_i, state) + jnp.einsum('bhij,bhjv->bhiv', a_i, v_new)
        state = state * gl_i[..., None, None] + jnp.einsum('bhck,bhcv->bhkv', kt_i, v_new)
        return state, o_i

    s0 = jnp.zeros((b, h, dk, dv), jnp.float32)
    _, o = lax.scan(step, s0, xs)
    return jnp.moveaxis(o, 0, 2).reshape(b, h, s, dv)


def t5_bucket(dist):
    max_exact = N_BUCKETS // 2
    d = np.maximum(dist, 1).astype(np.float64)
    large = max_exact + (np.log(d / max_exact) / math.log(MAX_DISTANCE / max_exact)
                         * (N_BUCKETS - max_exact)).astype(np.int32)
    large = np.minimum(large, N_BUCKETS - 1)
    return np.where(dist < max_exact, dist, large).astype(np.int32)


def dilated_pattern(q, k, v, rel_bias, window, dilation):
    b, h, s, hd = q.shape
    r = dilation
    l_sub = s // r
    w_steps = window // r
    blk = w_steps
    n_blk = -(-l_sub // blk)
    l_pad = n_blk * blk

    def to_blocks(t):
        t = t.reshape(b, h, l_sub, r, hd).transpose(0, 1, 3, 2, 4)
        t = jnp.pad(t, ((0, 0), (0, 0), (0, 0), (0, l_pad - l_sub), (0, 0)))
        return t.reshape(b, h, r, n_blk, blk, hd)

    def with_prev(t):
        prev = jnp.pad(t, ((0, 0), (0, 0), (0, 0), (1, 0), (0, 0), (0, 0)))[:, :, :, :-1]
        return jnp.concatenate([prev, t], axis=4)

    qb = to_blocks(q)
    kw = with_prev(to_blocks(k))
    vw = with_prev(to_blocks(v))
    qi = np.arange(blk)[:, None]
    kj = np.arange(2 * blk)[None, :]
    step = qi - kj + blk
    band = (step >= 0) & (step <= w_steps)
    key_idx = np.arange(n_blk)[:, None, None] * blk + kj[None] - blk
    mask = band[None] & (key_idx >= 0)
    buckets = t5_bucket(np.clip(step, 0, None) * r)
    bias = rel_bias.astype(jnp.float32)[:, buckets]
    scores = jnp.einsum('bhrnqd,bhrnkd->bhrnqk', qb, kw).astype(jnp.float32) + bias[:, None, None]
    scores = jnp.where(mask, scores, -jnp.inf)
    lse = jax.nn.logsumexp(scores, axis=-1)
    p = jnp.exp(scores - lse[..., None])
    o = jnp.einsum('bhrnqk,bhrnkd->bhrnqd', p, vw.astype(jnp.float32))
    o = o.reshape(b, h, r, l_pad, hd)[:, :, :, :l_sub].transpose(0, 1, 3, 2, 4).reshape(b, h, s, hd)
    lse = lse.reshape(b, h, r, l_pad)[:, :, :, :l_sub].transpose(0, 1, 3, 2).reshape(b, h, s)
    return o, lse


def deltanet_branch(qkv, z, b_proj, a_proj, conv_w, a_log, dt_bias, dn_norm_w):
    bsz, s, _ = qkv.shape
    qkv = jax.nn.silu(causal_depthwise_conv(qkv.astype(jnp.float32), conv_w.astype(jnp.float32)))
    q, k, v = jnp.split(qkv, 3, axis=-1)
    q = l2_norm(split_heads(q, DN_HEADS))
    k = l2_norm(split_heads(k, DN_HEADS))
    v = split_heads(v, DN_HEADS)
    beta = jax.nn.sigmoid(b_proj.astype(jnp.float32)).transpose(0, 2, 1)
    g = -jnp.exp(a_log.astype(jnp.float32)) * jax.nn.softplus(a_proj.astype(jnp.float32) + dt_bias.astype(jnp.float32))
    g = g.transpose(0, 2, 1)
    o = gated_delta_rule(q, k, v, g, beta)
    o = rms_norm(o, dn_norm_w).transpose(0, 2, 1, 3).reshape(bsz, s, D_DN)
    return o * jax.nn.silu(z.astype(jnp.float32))


def dilated_attention_branch(qkv, gate, q_norm_w, k_norm_w, rel_bias):
    bsz, s, _ = qkv.shape
    q, k, v = jnp.split(qkv, 3, axis=-1)
    q = rms_norm(split_heads(q, ATT_HEADS), q_norm_w) * (ATT_HEAD_DIM ** -0.5)
    k = rms_norm(split_heads(k, ATT_HEADS), k_norm_w)
    v = split_heads(v, ATT_HEADS).astype(jnp.float32)
    outs, lses = [], []
    for window, dilation in DILATED_PATTERNS:
        o_p, lse_p = dilated_pattern(q, k, v, rel_bias, window, dilation)
        outs.append(o_p)
        lses.append(lse_p)
    wts = jax.nn.softmax(jnp.stack(lses), axis=0)
    o = jnp.sum(wts[..., None] * jnp.stack(outs), axis=0)
    o = o.transpose(0, 2, 1, 3).reshape(bsz, s, D_ATT)
    return o * jax.nn.silu(gate.astype(jnp.float32))


def hybrid_layer(x, norm_w, w_in, conv_w, a_log, dt_bias, dn_norm_w, q_norm_w, k_norm_w, rel_bias, w_out):
    h = rms_norm(x, norm_w).astype(x.dtype)
    proj = h @ w_in
    cuts = [3 * D_DN, 4 * D_DN, 4 * D_DN + DN_HEADS, 4 * D_DN + 2 * DN_HEADS,
            4 * D_DN + 2 * DN_HEADS + 3 * D_ATT]
    qkv_dn, z_dn, b_dn, a_dn, qkv_att, gate_att = jnp.split(proj, cuts, axis=-1)
    y_dn = deltanet_branch(qkv_dn, z_dn, b_dn, a_dn, conv_w, a_log, dt_bias, dn_norm_w)
    y_att = dilated_attention_branch(qkv_att, gate_att, q_norm_w, k_norm_w, rel_bias)
    mixed = jnp.concatenate([y_dn, y_att], axis=-1).astype(x.dtype)
    return x + mixed @ w_out


def setup_inputs(seed: int = 0) -> dict:
    key = jax.random.key(seed)
    ks = jax.random.split(key, 11)
    x = jax.random.normal(ks[0], (BATCH, SEQ, D_MODEL), jnp.float32)
    norm_w = 1.0 + 0.1 * jax.random.normal(ks[1], (DEPTH, D_MODEL), jnp.float32)
    w_in = jax.random.normal(ks[2], (DEPTH, D_MODEL, D_IN), jnp.float32) * (D_MODEL ** -0.5)
    conv_w = jax.random.normal(ks[3], (DEPTH, CONV_WIDTH, 3 * D_DN), jnp.float32) * (CONV_WIDTH ** -0.5)
    a_log = jnp.log(jax.random.uniform(ks[4], (DEPTH, DN_HEADS), jnp.float32, minval=1.0, maxval=16.0))
    dt = jnp.exp(jax.random.uniform(ks[5], (DEPTH, DN_HEADS), jnp.float32,
                                    minval=math.log(1e-3), maxval=math.log(1e-1)))
    dt_bias = dt + jnp.log(-jnp.expm1(-dt))
    dn_norm_w = 1.0 + 0.1 * jax.random.normal(ks[6], (DEPTH, DN_HEAD_DIM), jnp.float32)
    q_norm_w = 1.0 + 0.1 * jax.random.normal(ks[7], (DEPTH, ATT_HEAD_DIM), jnp.float32)
    k_norm_w = 1.0 + 0.1 * jax.random.normal(ks[8], (DEPTH, ATT_HEAD_DIM), jnp.float32)
    rel_bias = 0.5 * jax.random.normal(ks[9], (ATT_HEADS, N_BUCKETS), jnp.float32)
    w_out = jax.random.normal(ks[10], (DEPTH, D_MIX, D_MODEL), jnp.float32) * (D_MIX ** -0.5)
    return {"x": x, "norm_w": norm_w, "w_in": w_in, "conv_w": conv_w, "a_log": a_log,
            "dt_bias": dt_bias, "dn_norm_w": dn_norm_w, "q_norm_w": q_norm_w,
            "k_norm_w": k_norm_w, "rel_bias": rel_bias, "w_out": w_out}


def reference(x, norm_w, w_in, conv_w, a_log, dt_bias, dn_norm_w, q_norm_w, k_norm_w, rel_bias, w_out):
    for layer in range(DEPTH):
        x = hybrid_layer(x, norm_w[layer], w_in[layer], conv_w[layer], a_log[layer], dt_bias[layer],
                         dn_norm_w[layer], q_norm_w[layer], k_norm_w[layer], rel_bias, w_out[layer])
    return x
```

```python
import functools
import math

import numpy as np
import jax
import jax.numpy as jnp
from jax import lax
from jax.experimental import pallas as pl
from jax.experimental.pallas import tpu as pltpu

D_MODEL = 1024
D_DN = 512
DN_HEADS = 4
DN_HEAD_DIM = 128
CONV_WIDTH = 4
CHUNK = 64
D_ATT = 512
ATT_HEADS = 8
ATT_HEAD_DIM = 64
DILATED_PATTERNS = ((128, 1), (512, 4), (2048, 16))
N_BUCKETS = 32
MAX_DISTANCE = 2048
D_MIX = D_DN + D_ATT
EPS = 1e-6

LANES = 128
SUBLANES = 8
BLK = 128
SUPER = 256
SUB = 8
NEG = -1e30
VMEM_LIMIT = 56 * 1024 * 1024

F32 = jnp.float32
BF16 = jnp.bfloat16


def _mm(a, b):
    return jnp.dot(a.astype(BF16), b.astype(BF16), preferred_element_type=F32)


def _mm_nt(a, b):
    return lax.dot_general(a.astype(BF16), b.astype(BF16), (((1,), (1,)), ((), ())),
                           preferred_element_type=F32)


def _sigmoid(x):
    return 1.0 / (1.0 + jnp.exp(-x))


def _in_proj_kernel(x_ref, nw_ref, w_ref, dnqkv_ref, dnz_ref, ba_ref, attqkv_ref, attg_ref):
    x = x_ref[...]
    ms = jnp.mean(x * x, axis=-1, keepdims=True)
    h = (x * lax.rsqrt(ms + EPS) * nw_ref[...]).astype(BF16)
    c0 = 3 * D_DN
    c1 = c0 + D_DN
    c2 = c1 + 3 * D_ATT
    c3 = c2 + D_ATT
    dnqkv_ref[...] = jnp.dot(h, w_ref[:, 0:c0], preferred_element_type=F32)
    dnz_ref[...] = jnp.dot(h, w_ref[:, c0:c1], preferred_element_type=F32).astype(BF16)
    attqkv_ref[...] = jnp.dot(h, w_ref[:, c1:c2], preferred_element_type=F32)
    attg_ref[...] = jnp.dot(h, w_ref[:, c2:c3], preferred_element_type=F32).astype(BF16)
    ba_ref[...] = jnp.dot(h, w_ref[:, c3:c3 + LANES], preferred_element_type=F32)


def _in_proj(x2d, norm_w, w_all, tm=512):
    n = x2d.shape[0]
    ncol = w_all.shape[1]
    row = lambda i: (i, 0)
    fixed = lambda i: (0, 0)
    return pl.pallas_call(
        _in_proj_kernel,
        grid=(n // tm,),
        in_specs=[pl.BlockSpec((tm, D_MODEL), row),
                  pl.BlockSpec((1, D_MODEL), fixed),
                  pl.BlockSpec((D_MODEL, ncol), fixed)],
        out_specs=[pl.BlockSpec((tm, 3 * D_DN), row),
                   pl.BlockSpec((tm, D_DN), row),
                   pl.BlockSpec((tm, LANES), row),
                   pl.BlockSpec((tm, 3 * D_ATT), row),
                   pl.BlockSpec((tm, D_ATT), row)],
        out_shape=[jax.ShapeDtypeStruct((n, 3 * D_DN), F32),
                   jax.ShapeDtypeStruct((n, D_DN), BF16),
                   jax.ShapeDtypeStruct((n, LANES), F32),
                   jax.ShapeDtypeStruct((n, 3 * D_ATT), F32),
                   jax.ShapeDtypeStruct((n, D_ATT), BF16)],
        compiler_params=pltpu.CompilerParams(
            dimension_semantics=("arbitrary",), vmem_limit_bytes=VMEM_LIMIT),
        name="in_proj",
    )(x2d, norm_w, w_all)


def _t5_bucket(dist):
    max_exact = N_BUCKETS // 2
    d = np.maximum(dist, 1).astype(np.float64)
    large = max_exact + (np.log(d / max_exact) / math.log(MAX_DISTANCE / max_exact)
                         * (N_BUCKETS - max_exact)).astype(np.int32)
    large = np.minimum(large, N_BUCKETS - 1)
    return np.where(dist < max_exact, dist, large).astype(np.int32)


def _bucket_maps():
    maps = []
    for window, r in DILATED_PATTERNS:
        w_steps = window // r
        assert w_steps == BLK
        qi = np.arange(BLK)[:, None]
        kj = np.arange(2 * BLK)[None, :]
        step = qi - kj + BLK
        band = (step >= 0) & (step <= w_steps)
        buckets = _t5_bucket(np.clip(step, 0, None) * r)
        maps.append(np.where(band, buckets, -1).astype(np.int32))
    return np.stack(maps)


def _bias_kernel(rel_ref, bkt_ref, o_ref):
    h = pl.program_id(1)
    bkt = bkt_ref[0]
    acc = jnp.full(bkt.shape, NEG, F32)
    for b in range(N_BUCKETS):
        acc = jnp.where(bkt == b, rel_ref[h, b], acc)
    o_ref[0, 0] = acc


def _bias_tables(rel_bias):
    bkt = jnp.asarray(_bucket_maps())
    n_pat = bkt.shape[0]
    return pl.pallas_call(
        _bias_kernel,
        grid=(n_pat, ATT_HEADS),
        in_specs=[pl.BlockSpec(memory_space=pltpu.SMEM),
                  pl.BlockSpec((1, BLK, 2 * BLK), lambda p, h: (p, 0, 0))],
        out_specs=pl.BlockSpec((1, 1, BLK, 2 * BLK), lambda p, h: (p, h, 0, 0)),
        out_shape=jax.ShapeDtypeStruct((n_pat, ATT_HEADS, BLK, 2 * BLK), F32),
        name="bias_tables",
    )(rel_bias, bkt)


def _dn_kernel(q_ref, k_ref, v_ref, cwq_ref, cwk_ref, cwv_ref, ba_ref, z_ref,
               alog_ref, dtb_ref, nw_ref, o_ref, *, seq):
    h = pl.program_id(1)
    n_super = seq // SUPER
    n_chunk = SUPER // CHUNK

    ri = lax.broadcasted_iota(jnp.int32, (SUPER, SUPER), 0)
    ci = lax.broadcasted_iota(jnp.int32, (SUPER, SUPER), 1)
    same_chunk = (ri // CHUNK) == (ci // CHUNK)
    m_incl = same_chunk & (ri >= ci)
    m_strict = same_chunk & (ri > ci)
    m_sub = (ri // SUB) == (ci // SUB)
    eye = (ri == ci).astype(F32)
    lane = lax.broadcasted_iota(jnp.int32, (SUPER, LANES), 1)
    lane_t = lax.broadcasted_iota(jnp.int32, (LANES, SUPER), 1)
    pos = lax.broadcasted_iota(jnp.int32, (SUPER, LANES), 0) % CHUNK

    lane1 = lax.broadcasted_iota(jnp.int32, (1, LANES), 1)
    a_coef = -jnp.exp(jnp.sum(jnp.where(lane1 == h, alog_ref[...], 0.0), axis=-1, keepdims=True))
    dt_b = jnp.sum(jnp.where(lane1 == h, dtb_ref[...], 0.0), axis=-1, keepdims=True)
    nw = nw_ref[...]

    def conv_silu(x_ref, cw_ref, r0, not_first):
        cur = x_ref[0, pl.ds(r0, SUPER), :]
        prev = x_ref[0, pl.ds(jnp.maximum(r0 - SUBLANES, 0), SUBLANES), :] * not_first
        a = jnp.concatenate([prev, cur], axis=0)
        cw = cw_ref[...]
        y = cur * cw[CONV_WIDTH - 1:CONV_WIDTH, :]
        for s in range(1, CONV_WIDTH):
            sh = pltpu.roll(a, s, axis=0)[SUBLANES:SUBLANES + SUPER]
            y = y + sh * cw[CONV_WIDTH - 1 - s:CONV_WIDTH - s, :]
        return y * _sigmoid(y)

    def l2n(x):
        return x * lax.rsqrt(jnp.sum(x * x, axis=-1, keepdims=True) + EPS)

    def body(sc, state):
        r0 = pl.multiple_of(sc * SUPER, SUPER)
        not_first = (sc > 0).astype(F32)
        q = l2n(conv_silu(q_ref, cwq_ref, r0, not_first)) * (DN_HEAD_DIM ** -0.5)
        k = l2n(conv_silu(k_ref, cwk_ref, r0, not_first))
        v = conv_silu(v_ref, cwv_ref, r0, not_first)

        ba = ba_ref[0, pl.ds(r0, SUPER), :]
        b_pre = jnp.sum(jnp.where(lane == h, ba, 0.0), axis=-1, keepdims=True)
        a_pre = jnp.sum(jnp.where(lane == h + DN_HEADS, ba, 0.0), axis=-1, keepdims=True)
        beta = jnp.broadcast_to(_sigmoid(b_pre), (SUPER, LANES))
        xg = a_pre + dt_b
        softplus = jnp.maximum(xg, 0.0) + jnp.log(1.0 + jnp.exp(-jnp.abs(xg)))
        gc = jnp.broadcast_to(a_coef * softplus, (SUPER, LANES))
        for s in (1, 2, 4, 8, 16, 32):
            gc = gc + jnp.where(pos >= s, pltpu.roll(gc, s, axis=0), 0.0)
        gc_last = jnp.concatenate(
            [jnp.broadcast_to(gc[c * CHUNK + CHUNK - 1:c * CHUNK + CHUNK, :], (CHUNK, LANES))
             for c in range(n_chunk)], axis=0)
        gc_row = gc.T
        gc_cols = jnp.concatenate([gc, gc], axis=1)
        gc_rows = jnp.concatenate([gc_row, gc_row], axis=0)
        decay = jnp.exp(jnp.where(m_incl, gc_cols - gc_rows, NEG))

        kb = k * beta
        e_gc = jnp.exp(gc)
        k_bf = k.astype(BF16)
        kk = _mm_nt(kb, k_bf)
        qk = _mm_nt(q, k_bf)
        a_mat = jnp.where(m_strict, kk * decay, 0.0)
        attn = qk * decay

        d_blk = jnp.where(m_sub, a_mat, 0.0)
        l_blk = a_mat - d_blk
        x1 = -d_blk
        x2 = _mm(x1, x1)
        x4 = _mm(x2, x2)
        p1 = eye + x1 + x2 + _mm(x1, x2)
        t_d = p1 + _mm(p1, x4)
        y1 = -_mm(t_d, l_blk)
        y2 = _mm(y1, y1)
        y4 = _mm(y2, y2)
        q1 = eye + y1 + y2 + _mm(y1, y2)
        q2 = q1 + _mm(q1, y4)
        t_inv = _mm(q2, t_d)

        rhs = jnp.concatenate([v * beta, kb * e_gc], axis=1).astype(BF16)
        uw = _mm(t_inv, rhs).astype(BF16)
        aw = _mm(attn, uw)
        o_loc = aw[:, :LANES]
        q_t = (q * e_gc - aw[:, LANES:]).astype(BF16)
        kt_t = (k * jnp.exp(gc_last - gc)).T

        outs = []
        for c in range(n_chunk):
            sl = slice(c * CHUNK, (c + 1) * CHUNK)
            in_c = (lane_t // CHUNK) == c
            kw = _mm(jnp.where(in_c, kt_t, 0.0), uw)
            g_last = jnp.exp(jnp.broadcast_to(gc[c * CHUNK + CHUNK - 1:c * CHUNK + CHUNK, :],
                                              (LANES, LANES)))
            s_bf = state.astype(BF16)
            outs.append(_mm(q_t[sl], s_bf) + o_loc[sl])
            state = state * g_last - _mm(kw[:, LANES:], s_bf) + kw[:, :LANES]
        o = jnp.concatenate(outs, axis=0)

        ms = jnp.mean(o * o, axis=-1, keepdims=True)
        z = z_ref[0, pl.ds(r0, SUPER), :].astype(F32)
        y = o * lax.rsqrt(ms + EPS) * nw * (z * _sigmoid(z))
        o_ref[0, pl.ds(r0, SUPER), :] = y.astype(o_ref.dtype)
        return state

    lax.fori_loop(0, n_super, body, jnp.zeros((DN_HEAD_DIM, DN_HEAD_DIM), F32))


def _deltanet(dn_qkv, dn_z, ba, conv_w, a_log, dt_bias, dn_norm_w):
    bsz, seq, _ = dn_qkv.shape
    nh = DN_HEADS

    def col(off):
        return lambda b, h: (b, 0, off + h)

    def cw(off):
        return lambda b, h: (0, off + h)

    fixed = lambda b, h: (0, 0)
    return pl.pallas_call(
        functools.partial(_dn_kernel, seq=seq),
        grid=(bsz, nh),
        in_specs=[pl.BlockSpec((1, seq, LANES), col(0)),
                  pl.BlockSpec((1, seq, LANES), col(nh)),
                  pl.BlockSpec((1, seq, LANES), col(2 * nh)),
                  pl.BlockSpec((CONV_WIDTH, LANES), cw(0)),
                  pl.BlockSpec((CONV_WIDTH, LANES), cw(nh)),
                  pl.BlockSpec((CONV_WIDTH, LANES), cw(2 * nh)),
                  pl.BlockSpec((1, seq, LANES), lambda b, h: (b, 0, 0)),
                  pl.BlockSpec((1, seq, LANES), lambda b, h: (b, 0, h)),
                  pl.BlockSpec((1, LANES), fixed),
                  pl.BlockSpec((1, LANES), fixed),
                  pl.BlockSpec((1, LANES), fixed)],
        out_specs=pl.BlockSpec((1, seq, LANES), lambda b, h: (b, 0, h)),
        out_shape=jax.ShapeDtypeStruct((bsz, seq, D_DN), BF16),
        compiler_params=pltpu.CompilerParams(
            dimension_semantics=("arbitrary", "arbitrary"), vmem_limit_bytes=VMEM_LIMIT),
        name="deltanet",
    )(dn_qkv, dn_qkv, dn_qkv, conv_w, conv_w, conv_w, ba, dn_z, a_log, dt_bias, dn_norm_w)


def _att_kernel(q_ref, k_ref, v_ref, g_ref, qw_ref, kw_ref, bias_ref, o_ref,
                qn_s, kn_s, qp, kp, vp, m_s, l_s, acc_s, *, seq):
    tile = 512
    lane = lax.broadcasted_iota(jnp.int32, (BLK, LANES), 1)
    head0 = lane < ATT_HEAD_DIM
    ri = lax.broadcasted_iota(jnp.int32, (LANES, LANES), 0)
    ci = lax.broadcasted_iota(jnp.int32, (LANES, LANES), 1)
    same_head = ((ri // ATT_HEAD_DIM) == (ci // ATT_HEAD_DIM)).astype(BF16)

    def head_norm(x, w):
        x2 = x * x
        hi = x2.astype(BF16)
        lo = (x2 - hi.astype(F32)).astype(BF16)
        ss = (jnp.dot(hi, same_head, preferred_element_type=F32)
              + jnp.dot(lo, same_head, preferred_element_type=F32))
        return x * lax.rsqrt(ss * (1.0 / ATT_HEAD_DIM) + EPS) * w

    def norm_body(i, carry):
        rows = pl.ds(pl.multiple_of(i * tile, tile), tile)
        qn_s[rows, :] = head_norm(q_ref[0, rows, :], qw_ref[...]) * (ATT_HEAD_DIM ** -0.5)
        kn_s[rows, :] = head_norm(k_ref[0, rows, :], kw_ref[...])
        return carry

    lax.fori_loop(0, seq // tile, norm_body, 0)

    def block(p, r, base, tok0, first, merge):
        q = qp[pl.ds(base, BLK), :]
        if first:
            k = kp[pl.ds(base, BLK), :]
            v = vp[pl.ds(base, BLK), :]
        else:
            k = kp[pl.ds(base - BLK, 2 * BLK), :]
            v = vp[pl.ds(base - BLK, 2 * BLK), :]
        ms, ls, pvs = [], [], []
        for hh in range(2):
            qh = jnp.where(head0 if hh == 0 else ~head0, q, jnp.zeros_like(q))
            bias = bias_ref[p, hh, :, BLK:2 * BLK] if first else bias_ref[p, hh]
            s = lax.dot_general(qh, k, (((1,), (1,)), ((), ())), preferred_element_type=F32) + bias
            m = jnp.max(s, axis=-1, keepdims=True)
            e = jnp.exp(s - m)
            ms.append(m)
            ls.append(jnp.sum(e, axis=-1, keepdims=True))
            pvs.append(jnp.dot(e.astype(BF16), v, preferred_element_type=F32))
        m_b = jnp.where(head0, ms[0], ms[1])
        l_b = jnp.where(head0, ls[0], ls[1])
        pv = jnp.where(head0, pvs[0], pvs[1])
        rows = pl.ds(tok0, BLK) if r == 1 else pl.ds(tok0, BLK, stride=r)
        if merge:
            m_old = m_s[rows, :]
            m_new = jnp.maximum(m_old, m_b)
            a_old = jnp.exp(m_old - m_new)
            a_new = jnp.exp(m_b - m_new)
            m_s[rows, :] = m_new
            l_s[rows, :] = a_old * l_s[rows, :] + a_new * l_b
            acc_s[rows, :] = a_old * acc_s[rows, :] + a_new * pv
        else:
            m_s[rows, :] = m_b
            l_s[rows, :] = l_b
            acc_s[rows, :] = pv

    for p, (window, r) in enumerate(DILATED_PATTERNS):
        l_sub = seq // r
        n_blk = l_sub // BLK
        merge = p > 0

        if r == 1:
            def cast_body(i, carry):
                rows = pl.ds(pl.multiple_of(i * tile, tile), tile)
                qp[rows, :] = qn_s[rows, :].astype(BF16)
                kp[rows, :] = kn_s[rows, :].astype(BF16)
                vp[rows, :] = v_ref[0, rows, :].astype(BF16)
                return carry
            lax.fori_loop(0, seq // tile, cast_body, 0)
        else:
            n_piece = l_sub // BLK

            def perm_body(i, carry, r=r, l_sub=l_sub, n_piece=n_piece):
                res = i // n_piece
                j = i % n_piece
                src = pl.ds(res + j * (BLK * r), BLK, stride=r)
                dst = pl.ds(pl.multiple_of(res * l_sub + j * BLK, BLK), BLK)
                qp[dst, :] = qn_s[src, :].astype(BF16)
                kp[dst, :] = kn_s[src, :].astype(BF16)
                vp[dst, :] = v_ref[0, src, :].astype(BF16)
                return carry
            lax.fori_loop(0, r * n_piece, perm_body, 0)

        def res_body(res, carry, p=p, r=r, l_sub=l_sub, n_blk=n_blk, merge=merge):
            base0 = pl.multiple_of(res * l_sub, BLK)
            block(p, r, base0, res, True, merge)

            def blk_body(n, c2):
                base = pl.multiple_of(res * l_sub + n * BLK, BLK)
                block(p, r, base, res + n * (BLK * r), False, merge)
                return c2
            lax.fori_loop(1, n_blk, blk_body, 0)
            return carry
        lax.fori_loop(0, r, res_body, 0)

    def out_body(i, carry):
        rows = pl.ds(pl.multiple_of(i * tile, tile), tile)
        g = g_ref[0, rows, :].astype(F32)
        o_ref[0, rows, :] = (acc_s[rows, :] / l_s[rows, :] * (g * _sigmoid(g))).astype(o_ref.dtype)
        return carry
    lax.fori_loop(0, seq // tile, out_body, 0)


def _dilated_attn(att_qkv, att_g, q_norm_w, k_norm_w, bias):
    bsz, seq, _ = att_qkv.shape
    n_pair = ATT_HEADS // 2
    n_pat = bias.shape[0]

    def col(off):
        return lambda b, j: (b, 0, off + j)

    fixed = lambda b, j: (0, 0)
    return pl.pallas_call(
        functools.partial(_att_kernel, seq=seq),
        grid=(bsz, n_pair),
        in_specs=[pl.BlockSpec((1, seq, LANES), col(0)),
                  pl.BlockSpec((1, seq, LANES), col(n_pair)),
                  pl.BlockSpec((1, seq, LANES), col(2 * n_pair)),
                  pl.BlockSpec((1, seq, LANES), lambda b, j: (b, 0, j)),
                  pl.BlockSpec((1, LANES), fixed),
                  pl.BlockSpec((1, LANES), fixed),
                  pl.BlockSpec((n_pat, 2, BLK, 2 * BLK), lambda b, j: (0, j, 0, 0))],
        out_specs=pl.BlockSpec((1, seq, LANES), lambda b, j: (b, 0, j)),
        out_shape=jax.ShapeDtypeStruct((bsz, seq, D_ATT), BF16),
        scratch_shapes=[pltpu.VMEM((seq, LANES), F32),
                        pltpu.VMEM((seq, LANES), F32),
                        pltpu.VMEM((seq, LANES), BF16),
                        pltpu.VMEM((seq, LANES), BF16),
                        pltpu.VMEM((seq, LANES), BF16),
                        pltpu.VMEM((seq, LANES), F32),
                        pltpu.VMEM((seq, LANES), F32),
                        pltpu.VMEM((seq, LANES), F32)],
        compiler_params=pltpu.CompilerParams(
            dimension_semantics=("arbitrary", "arbitrary"), vmem_limit_bytes=VMEM_LIMIT),
        name="dilated_attn",
    )(att_qkv, att_qkv, att_qkv, att_g, q_norm_w, k_norm_w, bias)


def _out_proj_kernel(x_ref, ydn_ref, yatt_ref, w_ref, o_ref):
    o_ref[...] = (x_ref[...]
                  + jnp.dot(ydn_ref[...], w_ref[0:D_DN, :], preferred_element_type=F32)
                  + jnp.dot(yatt_ref[...], w_ref[D_DN:D_MIX, :], preferred_element_type=F32))


def _out_proj(x2d, y_dn, y_att, w_out, tm=512):
    n = x2d.shape[0]
    row = lambda i: (i, 0)
    return pl.pallas_call(
        _out_proj_kernel,
        grid=(n // tm,),
        in_specs=[pl.BlockSpec((tm, D_MODEL), row),
                  pl.BlockSpec((tm, D_DN), row),
                  pl.BlockSpec((tm, D_ATT), row),
                  pl.BlockSpec((D_MIX, D_MODEL), lambda i: (0, 0))],
        out_specs=pl.BlockSpec((tm, D_MODEL), row),
        out_shape=jax.ShapeDtypeStruct((n, D_MODEL), F32),
        compiler_params=pltpu.CompilerParams(
            dimension_semantics=("arbitrary",), vmem_limit_bytes=VMEM_LIMIT),
        name="out_proj",
    )(x2d, y_dn, y_att, w_out)


def _pad_lanes(v):
    v = v.reshape(1, -1).astype(F32)
    return jnp.pad(v, ((0, 0), (0, LANES - v.shape[1])))


def _layer(x, norm_w, w_in, conv_w, a_log, dt_bias, dn_norm_w, q_norm_w, k_norm_w, bias, w_out):
    bsz, seq, _ = x.shape
    x2d = x.reshape(bsz * seq, D_MODEL)
    c_dn = 4 * D_DN
    c_ba = c_dn + 2 * DN_HEADS
    w_ba = jnp.pad(w_in[:, c_dn:c_ba], ((0, 0), (0, LANES - 2 * DN_HEADS)))
    w_all = jnp.concatenate([w_in[:, :c_dn], w_in[:, c_ba:], w_ba], axis=1).astype(BF16)

    dn_qkv, dn_z, ba, att_qkv, att_g = _in_proj(x2d, norm_w.reshape(1, D_MODEL), w_all)
    dn_qkv = dn_qkv.reshape(bsz, seq, 3 * D_DN)
    dn_z = dn_z.reshape(bsz, seq, D_DN)
    ba = ba.reshape(bsz, seq, LANES)
    att_qkv = att_qkv.reshape(bsz, seq, 3 * D_ATT)
    att_g = att_g.reshape(bsz, seq, D_ATT)

    y_dn = _deltanet(dn_qkv, dn_z, ba, conv_w, _pad_lanes(a_log), _pad_lanes(dt_bias),
                     dn_norm_w.reshape(1, DN_HEAD_DIM))
    y_att = _dilated_attn(att_qkv, att_g,
                          jnp.tile(q_norm_w.reshape(1, ATT_HEAD_DIM), (1, 2)),
                          jnp.tile(k_norm_w.reshape(1, ATT_HEAD_DIM), (1, 2)), bias)
    out = _out_proj(x2d, y_dn.reshape(bsz * seq, D_DN), y_att.reshape(bsz * seq, D_ATT),
                    w_out.astype(BF16))
    return out.reshape(bsz, seq, D_MODEL)


def kernel(x, norm_w, w_in, conv_w, a_log, dt_bias, dn_norm_w, q_norm_w, k_norm_w, rel_bias, w_out):
    bias = _bias_tables(rel_bias.astype(F32))
    for layer in range(norm_w.shape[0]):
        x = _layer(x, norm_w[layer], w_in[layer], conv_w[layer], a_log[layer], dt_bias[layer],
                   dn_norm_w[layer], q_norm_w[layer], k_norm_w[layer], bias, w_out[layer])
    return x
```

```python
import functools
import math

import numpy as np
import jax
import jax.numpy as jnp
from jax import lax
from jax.experimental import pallas as pl
from jax.experimental.pallas import tpu as pltpu

D_MODEL = 1024
D_DN = 512
DN_HEADS = 4
DN_HEAD_DIM = 128
CONV_WIDTH = 4
CHUNK = 64
D_ATT = 512
ATT_HEADS = 8
ATT_HEAD_DIM = 64
DILATED_PATTERNS = ((128, 1), (512, 4), (2048, 16))
N_BUCKETS = 32
MAX_DISTANCE = 2048
D_MIX = D_DN + D_ATT
EPS = 1e-6

LANES = 128
SUBLANES = 8
BLK = 128
ATT_GROUP = 4
SUPER = 256
SUB = 8
NEG = -1e30
VMEM_LIMIT = 56 * 1024 * 1024

F32 = jnp.float32
BF16 = jnp.bfloat16


def _mm(a, b):
    return jnp.dot(a.astype(BF16), b.astype(BF16), preferred_element_type=F32)


def _mm_nt(a, b):
    return lax.dot_general(a.astype(BF16), b.astype(BF16), (((1,), (1,)), ((), ())),
                           preferred_element_type=F32)


def _sigmoid(x):
    return 1.0 / (1.0 + jnp.exp(-x))


def _in_proj_kernel(x_ref, nw_ref, w_ref, dnqkv_ref, dnz_ref, ba_ref, attqkv_ref, attg_ref):
    x = x_ref[...]
    ms = jnp.mean(x * x, axis=-1, keepdims=True)
    h = (x * lax.rsqrt(ms + EPS) * nw_ref[...]).astype(BF16)
    c0 = 3 * D_DN
    c1 = c0 + D_DN
    c2 = c1 + 3 * D_ATT
    c3 = c2 + D_ATT
    dnqkv_ref[...] = jnp.dot(h, w_ref[:, 0:c0], preferred_element_type=F32)
    dnz_ref[...] = jnp.dot(h, w_ref[:, c0:c1], preferred_element_type=F32).astype(BF16)
    attqkv_ref[...] = jnp.dot(h, w_ref[:, c1:c2], preferred_element_type=F32)
    attg_ref[...] = jnp.dot(h, w_ref[:, c2:c3], preferred_element_type=F32).astype(BF16)
    ba_ref[...] = jnp.dot(h, w_ref[:, c3:c3 + LANES], preferred_element_type=F32)


def _in_proj(x2d, norm_w, w_all, tm=512):
    n = x2d.shape[0]
    ncol = w_all.shape[1]
    row = lambda i: (i, 0)
    fixed = lambda i: (0, 0)
    return pl.pallas_call(
        _in_proj_kernel,
        grid=(n // tm,),
        in_specs=[pl.BlockSpec((tm, D_MODEL), row),
                  pl.BlockSpec((1, D_MODEL), fixed),
                  pl.BlockSpec((D_MODEL, ncol), fixed)],
        out_specs=[pl.BlockSpec((tm, 3 * D_DN), row),
                   pl.BlockSpec((tm, D_DN), row),
                   pl.BlockSpec((tm, LANES), row),
                   pl.BlockSpec((tm, 3 * D_ATT), row),
                   pl.BlockSpec((tm, D_ATT), row)],
        out_shape=[jax.ShapeDtypeStruct((n, 3 * D_DN), F32),
                   jax.ShapeDtypeStruct((n, D_DN), BF16),
                   jax.ShapeDtypeStruct((n, LANES), F32),
                   jax.ShapeDtypeStruct((n, 3 * D_ATT), F32),
                   jax.ShapeDtypeStruct((n, D_ATT), BF16)],
        compiler_params=pltpu.CompilerParams(
            dimension_semantics=("arbitrary",), vmem_limit_bytes=VMEM_LIMIT),
        name="in_proj",
    )(x2d, norm_w, w_all)


def _t5_bucket(dist):
    max_exact = N_BUCKETS // 2
    d = np.maximum(dist, 1).astype(np.float64)
    large = max_exact + (np.log(d / max_exact) / math.log(MAX_DISTANCE / max_exact)
                         * (N_BUCKETS - max_exact)).astype(np.int32)
    large = np.minimum(large, N_BUCKETS - 1)
    return np.where(dist < max_exact, dist, large).astype(np.int32)


def _bucket_maps():
    maps = []
    for window, r in DILATED_PATTERNS:
        w_steps = window // r
        assert w_steps == BLK
        qi = np.arange(BLK)[:, None]
        kj = np.arange(2 * BLK)[None, :]
        step = qi - kj + BLK
        band = (step >= 0) & (step <= w_steps)
        buckets = _t5_bucket(np.clip(step, 0, None) * r)
        maps.append(np.where(band, buckets, -1).astype(np.int32))
    return np.stack(maps)


def _bias_kernel(rel_ref, bkt_ref, o_ref):
    h = pl.program_id(1)
    bkt = bkt_ref[0]
    acc = jnp.full(bkt.shape, NEG, F32)
    for b in range(N_BUCKETS):
        acc = jnp.where(bkt == b, rel_ref[h, b], acc)
    o_ref[0, 0, 0] = acc
    col = lax.broadcasted_iota(jnp.int32, bkt.shape, 1)
    o_ref[0, 0, 1] = jnp.where(col < BLK, NEG, acc)


def _bias_tables(rel_bias):
    bkt = jnp.asarray(_bucket_maps())
    n_pat = bkt.shape[0]
    return pl.pallas_call(
        _bias_kernel,
        grid=(n_pat, ATT_HEADS),
        in_specs=[pl.BlockSpec(memory_space=pltpu.SMEM),
                  pl.BlockSpec((1, BLK, 2 * BLK), lambda p, h: (p, 0, 0))],
        out_specs=pl.BlockSpec((1, 1, 2, BLK, 2 * BLK), lambda p, h: (p, h, 0, 0, 0)),
        out_shape=jax.ShapeDtypeStruct((n_pat, ATT_HEADS, 2, BLK, 2 * BLK), F32),
        name="bias_tables",
    )(rel_bias, bkt)


def _dn_kernel(q_ref, k_ref, v_ref, cwq_ref, cwk_ref, cwv_ref, ba_ref, z_ref,
               alog_ref, dtb_ref, nw_ref, o_ref, *, seq):
    h = pl.program_id(1)
    n_super = seq // SUPER
    n_chunk = SUPER // CHUNK

    ri = lax.broadcasted_iota(jnp.int32, (SUPER, SUPER), 0)
    ci = lax.broadcasted_iota(jnp.int32, (SUPER, SUPER), 1)
    same_chunk = (ri // CHUNK) == (ci // CHUNK)
    m_incl = same_chunk & (ri >= ci)
    m_strict = same_chunk & (ri > ci)
    m_sub = (ri // SUB) == (ci // SUB)
    eye = (ri == ci).astype(F32)
    lane = lax.broadcasted_iota(jnp.int32, (SUPER, LANES), 1)
    lane_t = lax.broadcasted_iota(jnp.int32, (LANES, SUPER), 1)
    pos = lax.broadcasted_iota(jnp.int32, (SUPER, LANES), 0) % CHUNK

    lane1 = lax.broadcasted_iota(jnp.int32, (1, LANES), 1)
    a_coef = -jnp.exp(jnp.sum(jnp.where(lane1 == h, alog_ref[...], 0.0), axis=-1, keepdims=True))
    dt_b = jnp.sum(jnp.where(lane1 == h, dtb_ref[...], 0.0), axis=-1, keepdims=True)
    nw = nw_ref[...]

    def conv_silu(x_ref, cw_ref, r0, not_first):
        cur = x_ref[0, pl.ds(r0, SUPER), :]
        prev = x_ref[0, pl.ds(jnp.maximum(r0 - SUBLANES, 0), SUBLANES), :] * not_first
        a = jnp.concatenate([prev, cur], axis=0)
        cw = cw_ref[...]
        y = cur * cw[CONV_WIDTH - 1:CONV_WIDTH, :]
        for s in range(1, CONV_WIDTH):
            sh = pltpu.roll(a, s, axis=0)[SUBLANES:SUBLANES + SUPER]
            y = y + sh * cw[CONV_WIDTH - 1 - s:CONV_WIDTH - s, :]
        return y * _sigmoid(y)

    def l2n(x):
        return x * lax.rsqrt(jnp.sum(x * x, axis=-1, keepdims=True) + EPS)

    def body(sc, state):
        r0 = pl.multiple_of(sc * SUPER, SUPER)
        not_first = (sc > 0).astype(F32)
        q = l2n(conv_silu(q_ref, cwq_ref, r0, not_first)) * (DN_HEAD_DIM ** -0.5)
        k = l2n(conv_silu(k_ref, cwk_ref, r0, not_first))
        v = conv_silu(v_ref, cwv_ref, r0, not_first)

        ba = ba_ref[0, pl.ds(r0, SUPER), :]
        b_pre = jnp.sum(jnp.where(lane == h, ba, 0.0), axis=-1, keepdims=True)
        a_pre = jnp.sum(jnp.where(lane == h + DN_HEADS, ba, 0.0), axis=-1, keepdims=True)
        beta = jnp.broadcast_to(_sigmoid(b_pre), (SUPER, LANES))
        xg = a_pre + dt_b
        softplus = jnp.maximum(xg, 0.0) + jnp.log(1.0 + jnp.exp(-jnp.abs(xg)))
        gc = jnp.broadcast_to(a_coef * softplus, (SUPER, LANES))
        for s in (1, 2, 4, 8, 16, 32):
            gc = gc + jnp.where(pos >= s, pltpu.roll(gc, s, axis=0), 0.0)
        gc_last = jnp.concatenate(
            [jnp.broadcast_to(gc[c * CHUNK + CHUNK - 1:c * CHUNK + CHUNK, :], (CHUNK, LANES))
             for c in range(n_chunk)], axis=0)
        gc_row = gc.T
        gc_cols = jnp.concatenate([gc, gc], axis=1)
        gc_rows = jnp.concatenate([gc_row, gc_row], axis=0)
        decay = jnp.exp(jnp.where(m_incl, gc_cols - gc_rows, NEG))

        kb = k * beta
        e_gc = jnp.exp(gc)
        k_bf = k.astype(BF16)
        kk = _mm_nt(kb, k_bf)
        qk = _mm_nt(q, k_bf)
        a_mat = jnp.where(m_strict, kk * decay, 0.0)
        attn = qk * decay

        d_blk = jnp.where(m_sub, a_mat, 0.0)
        l_blk = a_mat - d_blk
        x1 = -d_blk
        x2 = _mm(x1, x1)
        x4 = _mm(x2, x2)
        p1 = eye + x1 + x2 + _mm(x1, x2)
        t_d = p1 + _mm(p1, x4)
        y1 = -_mm(t_d, l_blk)
        y2 = _mm(y1, y1)
        y4 = _mm(y2, y2)
        q1 = eye + y1 + y2 + _mm(y1, y2)
        q2 = q1 + _mm(q1, y4)
        t_inv = _mm(q2, t_d)

        rhs = jnp.concatenate([v * beta, kb * e_gc], axis=1).astype(BF16)
        uw = _mm(t_inv, rhs).astype(BF16)
        aw = _mm(attn, uw)
        o_loc = aw[:, :LANES]
        q_t = (q * e_gc - aw[:, LANES:]).astype(BF16)
        kt_t = (k * jnp.exp(gc_last - gc)).T

        outs = []
        for c in range(n_chunk):
            sl = slice(c * CHUNK, (c + 1) * CHUNK)
            in_c = (lane_t // CHUNK) == c
            kw = _mm(jnp.where(in_c, kt_t, 0.0), uw)
            g_last = jnp.exp(jnp.broadcast_to(gc[c * CHUNK + CHUNK - 1:c * CHUNK + CHUNK, :],
                                              (LANES, LANES)))
            s_bf = state.astype(BF16)
            outs.append(_mm(q_t[sl], s_bf) + o_loc[sl])
            state = state * g_last - _mm(kw[:, LANES:], s_bf) + kw[:, :LANES]
        o = jnp.concatenate(outs, axis=0)

        ms = jnp.mean(o * o, axis=-1, keepdims=True)
        z = z_ref[0, pl.ds(r0, SUPER), :].astype(F32)
        y = o * lax.rsqrt(ms + EPS) * nw * (z * _sigmoid(z))
        o_ref[0, pl.ds(r0, SUPER), :] = y.astype(o_ref.dtype)
        return state

    lax.fori_loop(0, n_super, body, jnp.zeros((DN_HEAD_DIM, DN_HEAD_DIM), F32))


def _deltanet(dn_qkv, dn_z, ba, conv_w, a_log, dt_bias, dn_norm_w):
    bsz, seq, _ = dn_qkv.shape
    nh = DN_HEADS

    def col(off):
        return lambda b, h: (b, 0, off + h)

    def cw(off):
        return lambda b, h: (0, off + h)

    fixed = lambda b, h: (0, 0)
    return pl.pallas_call(
        functools.partial(_dn_kernel, seq=seq),
        grid=(bsz, nh),
        in_specs=[pl.BlockSpec((1, seq, LANES), col(0)),
                  pl.BlockSpec((1, seq, LANES), col(nh)),
                  pl.BlockSpec((1, seq, LANES), col(2 * nh)),
                  pl.BlockSpec((CONV_WIDTH, LANES), cw(0)),
                  pl.BlockSpec((CONV_WIDTH, LANES), cw(nh)),
                  pl.BlockSpec((CONV_WIDTH, LANES), cw(2 * nh)),
                  pl.BlockSpec((1, seq, LANES), lambda b, h: (b, 0, 0)),
                  pl.BlockSpec((1, seq, LANES), lambda b, h: (b, 0, h)),
                  pl.BlockSpec((1, LANES), fixed),
                  pl.BlockSpec((1, LANES), fixed),
                  pl.BlockSpec((1, LANES), fixed)],
        out_specs=pl.BlockSpec((1, seq, LANES), lambda b, h: (b, 0, h)),
        out_shape=jax.ShapeDtypeStruct((bsz, seq, D_DN), BF16),
        compiler_params=pltpu.CompilerParams(
            dimension_semantics=("arbitrary", "arbitrary"), vmem_limit_bytes=VMEM_LIMIT),
        name="deltanet",
    )(dn_qkv, dn_qkv, dn_qkv, conv_w, conv_w, conv_w, ba, dn_z, a_log, dt_bias, dn_norm_w)


def _att_kernel(q_ref, k_ref, v_ref, g_ref, qw_ref, kw_ref, bias_ref, o_ref,
                qn_s, kn_s, qp, kp, vp, m_s, l_s, acc_s, *, seq):
    tile = 512
    lane = lax.broadcasted_iota(jnp.int32, (BLK, LANES), 1)
    head0 = lane < ATT_HEAD_DIM
    ri = lax.broadcasted_iota(jnp.int32, (LANES, LANES), 0)
    ci = lax.broadcasted_iota(jnp.int32, (LANES, LANES), 1)
    same_head = ((ri // ATT_HEAD_DIM) == (ci // ATT_HEAD_DIM)).astype(BF16)

    def head_norm(x, w):
        x2 = x * x
        hi = x2.astype(BF16)
        lo = (x2 - hi.astype(F32)).astype(BF16)
        ss = (jnp.dot(hi, same_head, preferred_element_type=F32)
              + jnp.dot(lo, same_head, preferred_element_type=F32))
        return x * lax.rsqrt(ss * (1.0 / ATT_HEAD_DIM) + EPS) * w

    def norm_body(i, carry):
        rows = pl.ds(pl.multiple_of(i * tile, tile), tile)
        qn_s[rows, :] = head_norm(q_ref[0, rows, :], qw_ref[...]) * (ATT_HEAD_DIM ** -0.5)
        kn_s[rows, :] = head_norm(k_ref[0, rows, :], kw_ref[...])
        return carry

    lax.fori_loop(0, seq // tile, norm_body, 0)

    kp[pl.ds(0, BLK), :] = jnp.zeros((BLK, LANES), BF16)
    vp[pl.ds(0, BLK), :] = jnp.zeros((BLK, LANES), BF16)
    ones_v = jnp.ones((2 * BLK, LANES), BF16)
    zero_q = jnp.zeros((BLK, LANES), BF16)

    def group(p, r, n_blk, bi0):
        idx = []
        for g in range(ATT_GROUP):
            bi = bi0 + g
            idx.append((pl.multiple_of(bi * BLK, BLK), bi // n_blk, bi % n_blk))
        scores = []
        for base, res, n in idx:
            q = qp[pl.ds(base, BLK), :]
            q2 = jnp.concatenate([jnp.where(head0, q, zero_q), jnp.where(head0, zero_q, q)], axis=0)
            k = kp[pl.ds(base, 2 * BLK), :]
            first = (n == 0).astype(jnp.int32)
            bias = jnp.concatenate([bias_ref[p, 0, first], bias_ref[p, 1, first]], axis=0)
            scores.append(lax.dot_general(q2, k, (((1,), (1,)), ((), ())),
                                          preferred_element_type=F32) + bias)
        maxes, probs = [], []
        for s in scores:
            m = jnp.max(s, axis=-1, keepdims=True)
            maxes.append(m)
            probs.append(jnp.exp(s - m).astype(BF16))
        for (base, res, n), m, e in zip(idx, maxes, probs):
            v2 = jnp.concatenate([vp[pl.ds(base, 2 * BLK), :], ones_v], axis=1)
            pv = jnp.dot(e, v2, preferred_element_type=F32)
            tok0 = res + n * (BLK * r)
            rows = pl.ds(tok0, BLK) if r == 1 else pl.ds(tok0, BLK, stride=r)
            m_s[p, rows, :] = jnp.where(head0, m[:BLK], m[BLK:])
            l_s[p, rows, :] = jnp.where(head0, pv[:BLK, LANES:], pv[BLK:, LANES:])
            acc_s[p, rows, :] = jnp.where(head0, pv[:BLK, :LANES], pv[BLK:, :LANES])

    for p, (window, r) in enumerate(DILATED_PATTERNS):
        l_sub = seq // r
        n_blk = l_sub // BLK

        def perm_body(i, carry, r=r, n_blk=n_blk):
            dst = pl.multiple_of(i * BLK, BLK)
            if r == 1:
                src = pl.ds(dst, BLK)
            else:
                src = pl.ds(i // n_blk + (i % n_blk) * (BLK * r), BLK, stride=r)
            qp[pl.ds(dst, BLK), :] = qn_s[src, :].astype(BF16)
            kp[pl.ds(dst + BLK, BLK), :] = kn_s[src, :].astype(BF16)
            vp[pl.ds(dst + BLK, BLK), :] = v_ref[0, src, :].astype(BF16)
            return carry
        lax.fori_loop(0, seq // BLK, perm_body, 0)

        def group_body(i, carry, p=p, r=r, n_blk=n_blk):
            group(p, r, n_blk, i * ATT_GROUP)
            return carry
        lax.fori_loop(0, seq // (BLK * ATT_GROUP), group_body, 0)

    def out_body(i, carry):
        rows = pl.ds(pl.multiple_of(i * tile, tile), tile)
        m1, m2, m3 = m_s[0, rows, :], m_s[1, rows, :], m_s[2, rows, :]
        m = jnp.maximum(jnp.maximum(m1, m2), m3)
        w1, w2, w3 = jnp.exp(m1 - m), jnp.exp(m2 - m), jnp.exp(m3 - m)
        num = w1 * acc_s[0, rows, :] + w2 * acc_s[1, rows, :] + w3 * acc_s[2, rows, :]
        den = w1 * l_s[0, rows, :] + w2 * l_s[1, rows, :] + w3 * l_s[2, rows, :]
        g = g_ref[0, rows, :].astype(F32)
        o_ref[0, rows, :] = (num / den * (g * _sigmoid(g))).astype(o_ref.dtype)
        return carry
    lax.fori_loop(0, seq // tile, out_body, 0)


def _dilated_attn(att_qkv, att_g, q_norm_w, k_norm_w, bias):
    bsz, seq, _ = att_qkv.shape
    n_pair = ATT_HEADS // 2
    n_pat = bias.shape[0]

    def col(off):
        return lambda b, j: (b, 0, off + j)

    fixed = lambda b, j: (0, 0)
    return pl.pallas_call(
        functools.partial(_att_kernel, seq=seq),
        grid=(bsz, n_pair),
        in_specs=[pl.BlockSpec((1, seq, LANES), col(0)),
                  pl.BlockSpec((1, seq, LANES), col(n_pair)),
                  pl.BlockSpec((1, seq, LANES), col(2 * n_pair)),
                  pl.BlockSpec((1, seq, LANES), lambda b, j: (b, 0, j)),
                  pl.BlockSpec((1, LANES), fixed),
                  pl.BlockSpec((1, LANES), fixed),
                  pl.BlockSpec((n_pat, 2, 2, BLK, 2 * BLK), lambda b, j: (0, j, 0, 0, 0))],
        out_specs=pl.BlockSpec((1, seq, LANES), lambda b, j: (b, 0, j)),
        out_shape=jax.ShapeDtypeStruct((bsz, seq, D_ATT), BF16),
        scratch_shapes=[pltpu.VMEM((seq, LANES), F32),
                        pltpu.VMEM((seq, LANES), F32),
                        pltpu.VMEM((seq, LANES), BF16),
                        pltpu.VMEM((seq + BLK, LANES), BF16),
                        pltpu.VMEM((seq + BLK, LANES), BF16),
                        pltpu.VMEM((n_pat, seq, LANES), F32),
                        pltpu.VMEM((n_pat, seq, LANES), F32),
                        pltpu.VMEM((n_pat, seq, LANES), F32)],
        compiler_params=pltpu.CompilerParams(
            dimension_semantics=("arbitrary", "arbitrary"), vmem_limit_bytes=VMEM_LIMIT),
        name="dilated_attn",
    )(att_qkv, att_qkv, att_qkv, att_g, q_norm_w, k_norm_w, bias)


def _out_proj_kernel(x_ref, ydn_ref, yatt_ref, w_ref, o_ref):
    o_ref[...] = (x_ref[...]
                  + jnp.dot(ydn_ref[...], w_ref[0:D_DN, :], preferred_element_type=F32)
                  + jnp.dot(yatt_ref[...], w_ref[D_DN:D_MIX, :], preferred_element_type=F32))


def _out_proj(x2d, y_dn, y_att, w_out, tm=512):
    n = x2d.shape[0]
    row = lambda i: (i, 0)
    return pl.pallas_call(
        _out_proj_kernel,
        grid=(n // tm,),
        in_specs=[pl.BlockSpec((tm, D_MODEL), row),
                  pl.BlockSpec((tm, D_DN), row),
                  pl.BlockSpec((tm, D_ATT), row),
                  pl.BlockSpec((D_MIX, D_MODEL), lambda i: (0, 0))],
        out_specs=pl.BlockSpec((tm, D_MODEL), row),
        out_shape=jax.ShapeDtypeStruct((n, D_MODEL), F32),
        compiler_params=pltpu.CompilerParams(
            dimension_semantics=("arbitrary",), vmem_limit_bytes=VMEM_LIMIT),
        name="out_proj",
    )(x2d, y_dn, y_att, w_out)


def _pad_lanes(v):
    v = v.reshape(1, -1).astype(F32)
    return jnp.pad(v, ((0, 0), (0, LANES - v.shape[1])))


def _layer(x, norm_w, w_in, conv_w, a_log, dt_bias, dn_norm_w, q_norm_w, k_norm_w, bias, w_out):
    bsz, seq, _ = x.shape
    x2d = x.reshape(bsz * seq, D_MODEL)
    c_dn = 4 * D_DN
    c_ba = c_dn + 2 * DN_HEADS
    w_ba = jnp.pad(w_in[:, c_dn:c_ba], ((0, 0), (0, LANES - 2 * DN_HEADS)))
    w_all = jnp.concatenate([w_in[:, :c_dn], w_in[:, c_ba:], w_ba], axis=1).astype(BF16)

    dn_qkv, dn_z, ba, att_qkv, att_g = _in_proj(x2d, norm_w.reshape(1, D_MODEL), w_all)
    dn_qkv = dn_qkv.reshape(bsz, seq, 3 * D_DN)
    dn_z = dn_z.reshape(bsz, seq, D_DN)
    ba = ba.reshape(bsz, seq, LANES)
    att_qkv = att_qkv.reshape(bsz, seq, 3 * D_ATT)
    att_g = att_g.reshape(bsz, seq, D_ATT)

    y_dn = _deltanet(dn_qkv, dn_z, ba, conv_w, _pad_lanes(a_log), _pad_lanes(dt_bias),
                     dn_norm_w.reshape(1, DN_HEAD_DIM))
    y_att = _dilated_attn(att_qkv, att_g,
                          jnp.tile(q_norm_w.reshape(1, ATT_HEAD_DIM), (1, 2)),
                          jnp.tile(k_norm_w.reshape(1, ATT_HEAD_DIM), (1, 2)), bias)
    out = _out_proj(x2d, y_dn.reshape(bsz * seq, D_DN), y_att.reshape(bsz * seq, D_ATT),
                    w_out.astype(BF16))
    return out.reshape(bsz, seq, D_MODEL)


def kernel(x, norm_w, w_in, conv_w, a_log, dt_bias, dn_norm_w, q_norm_w, k_norm_w, rel_bias, w_out):
    bias = _bias_tables(rel_bias.astype(F32))
    for layer in range(norm_w.shape[0]):
        x = _layer(x, norm_w[layer], w_in[layer], conv_w[layer], a_log[layer], dt_bias[layer],
                   dn_norm_w[layer], q_norm_w[layer], k_norm_w[layer], bias, w_out[layer])
    return x
```

```python
import functools
import math

import numpy as np
import jax
import jax.numpy as jnp
from jax import lax
from jax.experimental import pallas as pl
from jax.experimental.pallas import tpu as pltpu

D_MODEL = 1024
D_DN = 512
DN_HEADS = 4
DN_HEAD_DIM = 128
CONV_WIDTH = 4
CHUNK = 64
D_ATT = 512
ATT_HEADS = 8
ATT_HEAD_DIM = 64
DILATED_PATTERNS = ((128, 1), (512, 4), (2048, 16))
N_BUCKETS = 32
MAX_DISTANCE = 2048
D_MIX = D_DN + D_ATT
EPS = 1e-6

LANES = 128
SUBLANES = 8
BLK = 128
ATT_GROUP = 4
SUPER = 128
SUB = 8
NEG = -1e30
VMEM_LIMIT = 56 * 1024 * 1024

F32 = jnp.float32
BF16 = jnp.bfloat16


def _mm(a, b):
    return jnp.dot(a.astype(BF16), b.astype(BF16), preferred_element_type=F32)


def _mm_nt(a, b):
    return lax.dot_general(a.astype(BF16), b.astype(BF16), (((1,), (1,)), ((), ())),
                           preferred_element_type=F32)


def _sigmoid(x):
    return 1.0 / (1.0 + jnp.exp(-x))


def _in_proj_kernel(x_ref, nw_ref, w_ref, dnqkv_ref, dnz_ref, ba_ref, attqkv_ref, attg_ref):
    x = x_ref[...]
    ms = jnp.mean(x * x, axis=-1, keepdims=True)
    h = (x * lax.rsqrt(ms + EPS) * nw_ref[...]).astype(BF16)
    c0 = 3 * D_DN
    c1 = c0 + D_DN
    c2 = c1 + 3 * D_ATT
    c3 = c2 + D_ATT
    dnqkv_ref[...] = jnp.dot(h, w_ref[:, 0:c0], preferred_element_type=F32)
    dnz_ref[...] = jnp.dot(h, w_ref[:, c0:c1], preferred_element_type=F32).astype(BF16)
    attqkv_ref[...] = jnp.dot(h, w_ref[:, c1:c2], preferred_element_type=F32)
    attg_ref[...] = jnp.dot(h, w_ref[:, c2:c3], preferred_element_type=F32).astype(BF16)
    ba_ref[...] = jnp.dot(h, w_ref[:, c3:c3 + LANES], preferred_element_type=F32)


def _in_proj(x2d, norm_w, w_all, tm=512):
    n = x2d.shape[0]
    ncol = w_all.shape[1]
    row = lambda i: (i, 0)
    fixed = lambda i: (0, 0)
    return pl.pallas_call(
        _in_proj_kernel,
        grid=(n // tm,),
        in_specs=[pl.BlockSpec((tm, D_MODEL), row),
                  pl.BlockSpec((1, D_MODEL), fixed),
                  pl.BlockSpec((D_MODEL, ncol), fixed)],
        out_specs=[pl.BlockSpec((tm, 3 * D_DN), row),
                   pl.BlockSpec((tm, D_DN), row),
                   pl.BlockSpec((tm, LANES), row),
                   pl.BlockSpec((tm, 3 * D_ATT), row),
                   pl.BlockSpec((tm, D_ATT), row)],
        out_shape=[jax.ShapeDtypeStruct((n, 3 * D_DN), F32),
                   jax.ShapeDtypeStruct((n, D_DN), BF16),
                   jax.ShapeDtypeStruct((n, LANES), F32),
                   jax.ShapeDtypeStruct((n, 3 * D_ATT), F32),
                   jax.ShapeDtypeStruct((n, D_ATT), BF16)],
        compiler_params=pltpu.CompilerParams(
            dimension_semantics=("arbitrary",), vmem_limit_bytes=VMEM_LIMIT),
        name="in_proj",
    )(x2d, norm_w, w_all)


def _t5_bucket(dist):
    max_exact = N_BUCKETS // 2
    d = np.maximum(dist, 1).astype(np.float64)
    large = max_exact + (np.log(d / max_exact) / math.log(MAX_DISTANCE / max_exact)
                         * (N_BUCKETS - max_exact)).astype(np.int32)
    large = np.minimum(large, N_BUCKETS - 1)
    return np.where(dist < max_exact, dist, large).astype(np.int32)


def _bucket_maps():
    maps = []
    for window, r in DILATED_PATTERNS:
        w_steps = window // r
        assert w_steps == BLK
        qi = np.arange(BLK)[:, None]
        kj = np.arange(2 * BLK)[None, :]
        step = qi - kj + BLK
        band = (step >= 0) & (step <= w_steps)
        buckets = _t5_bucket(np.clip(step, 0, None) * r)
        maps.append(np.where(band, buckets, -1).astype(np.int32))
    return np.stack(maps)


def _bias_kernel(rel_ref, bkt_ref, o_ref):
    h = pl.program_id(1)
    bkt = bkt_ref[0]
    acc = jnp.full(bkt.shape, NEG, F32)
    for b in range(N_BUCKETS):
        acc = jnp.where(bkt == b, rel_ref[h, b], acc)
    o_ref[0, 0, 0] = acc
    col = lax.broadcasted_iota(jnp.int32, bkt.shape, 1)
    o_ref[0, 0, 1] = jnp.where(col < BLK, NEG, acc)


def _bias_tables(rel_bias):
    bkt = jnp.asarray(_bucket_maps())
    n_pat = bkt.shape[0]
    return pl.pallas_call(
        _bias_kernel,
        grid=(n_pat, ATT_HEADS),
        in_specs=[pl.BlockSpec(memory_space=pltpu.SMEM),
                  pl.BlockSpec((1, BLK, 2 * BLK), lambda p, h: (p, 0, 0))],
        out_specs=pl.BlockSpec((1, 1, 2, BLK, 2 * BLK), lambda p, h: (p, h, 0, 0, 0)),
        out_shape=jax.ShapeDtypeStruct((n_pat, ATT_HEADS, 2, BLK, 2 * BLK), F32),
        name="bias_tables",
    )(rel_bias, bkt)


def _dn_kernel(x_ref, cw_ref, ba_ref, z_ref, alog_ref, dtb_ref, nw_ref, o_ref,
               state_s, tail_s, *, tile):
    nh = DN_HEADS
    n_chunk = SUPER // CHUNK
    heads = range(nh)

    @pl.when(pl.program_id(1) == 0)
    def _():
        state_s[...] = jnp.zeros(state_s.shape, F32)
        tail_s[...] = jnp.zeros(tail_s.shape, F32)

    ri = lax.broadcasted_iota(jnp.int32, (SUPER, SUPER), 0)
    ci = lax.broadcasted_iota(jnp.int32, (SUPER, SUPER), 1)
    same_chunk = (ri // CHUNK) == (ci // CHUNK)
    m_incl = same_chunk & (ri >= ci)
    m_strict = same_chunk & (ri > ci)
    m_sub = (ri // SUB) == (ci // SUB)
    eye = (ri == ci).astype(F32)
    pos = ri % CHUNK
    chunk_cols = [(ci // CHUNK) == c for c in range(n_chunk)]

    a_coef = -jnp.exp(alog_ref[...])
    dt_b = dtb_ref[...]
    nw = nw_ref[...]
    cw = cw_ref[...]

    def conv_silu(rows, col):
        cols = slice(col, col + LANES)
        cur = x_ref[0, rows, cols]
        a = jnp.concatenate([tail_s[:, cols], cur], axis=0)
        y = cur * cw[CONV_WIDTH - 1:CONV_WIDTH, cols]
        for s in range(1, CONV_WIDTH):
            sh = pltpu.roll(a, s, axis=0)[SUBLANES:SUBLANES + SUPER]
            y = y + sh * cw[CONV_WIDTH - 1 - s:CONV_WIDTH - s, cols]
        return y * _sigmoid(y)

    def l2n(x):
        return x * lax.rsqrt(jnp.sum(x * x, axis=-1, keepdims=True) + EPS)

    def lane_bcast(x, j):
        return jnp.broadcast_to(x[:, j:j + 1], x.shape)

    def row_bcast(x, i, n_rows):
        return jnp.broadcast_to(x[i:i + 1, :], (n_rows, x.shape[1]))

    def body(sc, carry):
        r0 = pl.multiple_of(sc * SUPER, SUPER)
        rows = pl.ds(r0, SUPER)

        ba = ba_ref[0, rows, :]
        beta_all = _sigmoid(ba)
        xg = ba + dt_b
        gc_all = a_coef * (jnp.maximum(xg, 0.0) + jnp.log(1.0 + jnp.exp(-jnp.abs(xg))))
        for s in (1, 2, 4, 8, 16, 32):
            gc_all = gc_all + jnp.where(pos >= s, pltpu.roll(gc_all, s, axis=0), 0.0)
        gc_t = gc_all.T

        q = [l2n(conv_silu(rows, h * LANES)) * (DN_HEAD_DIM ** -0.5) for h in heads]
        k = [l2n(conv_silu(rows, D_DN + h * LANES)) for h in heads]
        v = [conv_silu(rows, 2 * D_DN + h * LANES) for h in heads]
        tail_s[...] = x_ref[0, pl.ds(r0 + SUPER - SUBLANES, SUBLANES), :]

        beta = [lane_bcast(beta_all, h) for h in heads]
        gc = [lane_bcast(gc_all, nh + h) for h in heads]
        decay = [jnp.exp(jnp.where(m_incl, gc[h] - row_bcast(gc_t, nh + h, SUPER), NEG))
                 for h in heads]
        gc_last = [jnp.concatenate([row_bcast(gc[h], c * CHUNK + CHUNK - 1, CHUNK)
                                    for c in range(n_chunk)], axis=0) for h in heads]
        e_gc = [jnp.exp(gc[h]) for h in heads]
        kb = [k[h] * beta[h] for h in heads]

        kq = [_mm_nt(jnp.concatenate([kb[h], q[h]], axis=0), k[h]) for h in heads]
        a_mat = [jnp.where(m_strict, kq[h][:SUPER] * decay[h], 0.0) for h in heads]
        attn = [(kq[h][SUPER:] * decay[h]).astype(BF16) for h in heads]

        d_blk = [jnp.where(m_sub, a_mat[h], 0.0) for h in heads]
        l_b = [(a_mat[h] - d_blk[h]).astype(BF16) for h in heads]
        x1 = [-d_blk[h] for h in heads]
        x1b = [x1[h].astype(BF16) for h in heads]
        x2 = [_mm(x1b[h], x1b[h]) for h in heads]
        x2b = [x2[h].astype(BF16) for h in heads]
        x4 = [_mm(x2b[h], x2b[h]) for h in heads]
        p1 = [eye + x1[h] + x2[h] + _mm(x1b[h], x2b[h]) for h in heads]
        t_d = [p1[h] + _mm(p1[h], x4[h]) for h in heads]
        t_db = [t_d[h].astype(BF16) for h in heads]
        y1 = [-_mm(t_db[h], l_b[h]) for h in heads]
        y1b = [y1[h].astype(BF16) for h in heads]
        y2 = [_mm(y1b[h], y1b[h]) for h in heads]
        y2b = [y2[h].astype(BF16) for h in heads]
        y4 = [_mm(y2b[h], y2b[h]) for h in heads]
        q1 = [eye + y1[h] + y2[h] + _mm(y1b[h], y2b[h]) for h in heads]
        q2 = [q1[h] + _mm(q1[h], y4[h]) for h in heads]
        t_inv = [_mm(q2[h], t_db[h]) for h in heads]

        rhs = [jnp.concatenate([v[h] * beta[h], kb[h] * e_gc[h]], axis=1) for h in heads]
        uw = [_mm(t_inv[h], rhs[h]).astype(BF16) for h in heads]
        aw = [_mm(attn[h], uw[h]) for h in heads]
        q_t = [(q[h] * e_gc[h] - aw[h][:, LANES:]).astype(BF16) for h in heads]
        kt_t = [(k[h] * jnp.exp(gc_last[h] - gc[h])).T.astype(BF16) for h in heads]
        kw = [[_mm(jnp.where(chunk_cols[c], kt_t[h], jnp.zeros_like(kt_t[h])), uw[h])
               for h in heads] for c in range(n_chunk)]

        state = [state_s[h] for h in heads]
        outs = [[] for _ in heads]
        for c in range(n_chunk):
            sl = slice(c * CHUNK, (c + 1) * CHUNK)
            s_bf = [state[h].astype(BF16) for h in heads]
            for h in heads:
                outs[h].append(_mm(q_t[h][sl], s_bf[h]) + aw[h][sl, :LANES])
            for h in heads:
                g_last = jnp.exp(row_bcast(gc[h], c * CHUNK + CHUNK - 1, LANES))
                state[h] = (state[h] * g_last - _mm(kw[c][h][:, LANES:], s_bf[h])
                            + kw[c][h][:, :LANES])
        for h in heads:
            state_s[h] = state[h]
            o = jnp.concatenate(outs[h], axis=0)
            ms = jnp.mean(o * o, axis=-1, keepdims=True)
            z = z_ref[0, rows, h * LANES:(h + 1) * LANES].astype(F32)
            y = o * lax.rsqrt(ms + EPS) * nw * (z * _sigmoid(z))
            o_ref[0, rows, h * LANES:(h + 1) * LANES] = y.astype(o_ref.dtype)
        return carry

    lax.fori_loop(0, tile // SUPER, body, 0)


def _deltanet(dn_qkv, dn_z, ba, conv_w, a_log, dt_bias, dn_norm_w, tile=1024):
    bsz, seq, _ = dn_qkv.shape
    seq_tile = lambda b, t: (b, t, 0)
    fixed = lambda b, t: (0, 0)
    return pl.pallas_call(
        functools.partial(_dn_kernel, tile=tile),
        grid=(bsz, seq // tile),
        in_specs=[pl.BlockSpec((1, tile, 3 * D_DN), seq_tile),
                  pl.BlockSpec((CONV_WIDTH, 3 * D_DN), fixed),
                  pl.BlockSpec((1, tile, LANES), seq_tile),
                  pl.BlockSpec((1, tile, D_DN), seq_tile),
                  pl.BlockSpec((1, LANES), fixed),
                  pl.BlockSpec((1, LANES), fixed),
                  pl.BlockSpec((1, LANES), fixed)],
        out_specs=pl.BlockSpec((1, tile, D_DN), seq_tile),
        out_shape=jax.ShapeDtypeStruct((bsz, seq, D_DN), BF16),
        scratch_shapes=[pltpu.VMEM((DN_HEADS, DN_HEAD_DIM, DN_HEAD_DIM), F32),
                        pltpu.VMEM((SUBLANES, 3 * D_DN), F32)],
        compiler_params=pltpu.CompilerParams(
            dimension_semantics=("arbitrary", "arbitrary"), vmem_limit_bytes=VMEM_LIMIT),
        name="deltanet",
    )(dn_qkv, conv_w, ba, dn_z, a_log, dt_bias, dn_norm_w)


def _att_kernel(q_ref, k_ref, v_ref, g_ref, qw_ref, kw_ref, bias_ref, o_ref,
                qn_s, kn_s, qp, kp, vp, m_s, l_s, acc_s, *, seq):
    tile = 512
    lane = lax.broadcasted_iota(jnp.int32, (BLK, LANES), 1)
    head0 = lane < ATT_HEAD_DIM
    ri = lax.broadcasted_iota(jnp.int32, (LANES, LANES), 0)
    ci = lax.broadcasted_iota(jnp.int32, (LANES, LANES), 1)
    same_head = ((ri // ATT_HEAD_DIM) == (ci // ATT_HEAD_DIM)).astype(BF16)

    def head_norm(x, w):
        x2 = x * x
        hi = x2.astype(BF16)
        lo = (x2 - hi.astype(F32)).astype(BF16)
        ss = (jnp.dot(hi, same_head, preferred_element_type=F32)
              + jnp.dot(lo, same_head, preferred_element_type=F32))
        return x * lax.rsqrt(ss * (1.0 / ATT_HEAD_DIM) + EPS) * w

    def norm_body(i, carry):
        rows = pl.ds(pl.multiple_of(i * tile, tile), tile)
        qn_s[rows, :] = head_norm(q_ref[0, rows, :], qw_ref[...]) * (ATT_HEAD_DIM ** -0.5)
        kn_s[rows, :] = head_norm(k_ref[0, rows, :], kw_ref[...])
        return carry

    lax.fori_loop(0, seq // tile, norm_body, 0)

    kp[pl.ds(0, BLK), :] = jnp.zeros((BLK, LANES), BF16)
    vp[pl.ds(0, BLK), :] = jnp.zeros((BLK, LANES), BF16)
    ones_v = jnp.ones((2 * BLK, LANES), BF16)
    zero_q = jnp.zeros((BLK, LANES), BF16)

    def group(p, r, n_blk, bi0):
        idx = []
        for g in range(ATT_GROUP):
            bi = bi0 + g
            idx.append((pl.multiple_of(bi * BLK, BLK), bi // n_blk, bi % n_blk))
        scores = []
        for base, res, n in idx:
            q = qp[pl.ds(base, BLK), :]
            q2 = jnp.concatenate([jnp.where(head0, q, zero_q), jnp.where(head0, zero_q, q)], axis=0)
            k = kp[pl.ds(base, 2 * BLK), :]
            first = jnp.asarray(n == 0, jnp.int32)
            bias = jnp.concatenate([bias_ref[p, 0, first], bias_ref[p, 1, first]], axis=0)
            scores.append(lax.dot_general(q2, k, (((1,), (1,)), ((), ())),
                                          preferred_element_type=F32) + bias)
        maxes, probs = [], []
        for s in scores:
            m = jnp.max(s, axis=-1, keepdims=True)
            maxes.append(m)
            probs.append(jnp.exp(s - m).astype(BF16))
        for (base, res, n), m, e in zip(idx, maxes, probs):
            v2 = jnp.concatenate([vp[pl.ds(base, 2 * BLK), :], ones_v], axis=1)
            pv = jnp.dot(e, v2, preferred_element_type=F32)
            tok0 = res + n * (BLK * r)
            rows = pl.ds(tok0, BLK) if r == 1 else pl.ds(tok0, BLK, stride=r)
            m_s[p, rows, :] = jnp.where(head0, m[:BLK], m[BLK:])
            l_s[p, rows, :] = jnp.where(head0, pv[:BLK, LANES:], pv[BLK:, LANES:])
            acc_s[p, rows, :] = jnp.where(head0, pv[:BLK, :LANES], pv[BLK:, :LANES])

    for p, (window, r) in enumerate(DILATED_PATTERNS):
        l_sub = seq // r
        n_blk = l_sub // BLK

        def perm_body(i, carry, r=r, n_blk=n_blk):
            dst = pl.multiple_of(i * BLK, BLK)
            if r == 1:
                src = pl.ds(dst, BLK)
            else:
                src = pl.ds(i // n_blk + (i % n_blk) * (BLK * r), BLK, stride=r)
            qp[pl.ds(dst, BLK), :] = qn_s[src, :].astype(BF16)
            kp[pl.ds(dst + BLK, BLK), :] = kn_s[src, :].astype(BF16)
            vp[pl.ds(dst + BLK, BLK), :] = v_ref[0, src, :].astype(BF16)
            return carry
        lax.fori_loop(0, seq // BLK, perm_body, 0)

        def group_body(i, carry, p=p, r=r, n_blk=n_blk):
            group(p, r, n_blk, i * ATT_GROUP)
            return carry
        lax.fori_loop(0, seq // (BLK * ATT_GROUP), group_body, 0)

    def out_body(i, carry):
        rows = pl.ds(pl.multiple_of(i * tile, tile), tile)
        m1, m2, m3 = m_s[0, rows, :], m_s[1, rows, :], m_s[2, rows, :]
        m = jnp.maximum(jnp.maximum(m1, m2), m3)
        w1, w2, w3 = jnp.exp(m1 - m), jnp.exp(m2 - m), jnp.exp(m3 - m)
        num = w1 * acc_s[0, rows, :] + w2 * acc_s[1, rows, :] + w3 * acc_s[2, rows, :]
        den = w1 * l_s[0, rows, :] + w2 * l_s[1, rows, :] + w3 * l_s[2, rows, :]
        g = g_ref[0, rows, :].astype(F32)
        o_ref[0, rows, :] = (num / den * (g * _sigmoid(g))).astype(o_ref.dtype)
        return carry
    lax.fori_loop(0, seq // tile, out_body, 0)


def _dilated_attn(att_qkv, att_g, q_norm_w, k_norm_w, bias):
    bsz, seq, _ = att_qkv.shape
    n_pair = ATT_HEADS // 2
    n_pat = bias.shape[0]

    def col(off):
        return lambda b, j: (b, 0, off + j)

    fixed = lambda b, j: (0, 0)
    return pl.pallas_call(
        functools.partial(_att_kernel, seq=seq),
        grid=(bsz, n_pair),
        in_specs=[pl.BlockSpec((1, seq, LANES), col(0)),
                  pl.BlockSpec((1, seq, LANES), col(n_pair)),
                  pl.BlockSpec((1, seq, LANES), col(2 * n_pair)),
                  pl.BlockSpec((1, seq, LANES), lambda b, j: (b, 0, j)),
                  pl.BlockSpec((1, LANES), fixed),
                  pl.BlockSpec((1, LANES), fixed),
                  pl.BlockSpec((n_pat, 2, 2, BLK, 2 * BLK), lambda b, j: (0, j, 0, 0, 0))],
        out_specs=pl.BlockSpec((1, seq, LANES), lambda b, j: (b, 0, j)),
        out_shape=jax.ShapeDtypeStruct((bsz, seq, D_ATT), BF16),
        scratch_shapes=[pltpu.VMEM((seq, LANES), F32),
                        pltpu.VMEM((seq, LANES), F32),
                        pltpu.VMEM((seq, LANES), BF16),
                        pltpu.VMEM((seq + BLK, LANES), BF16),
                        pltpu.VMEM((seq + BLK, LANES), BF16),
                        pltpu.VMEM((n_pat, seq, LANES), F32),
                        pltpu.VMEM((n_pat, seq, LANES), F32),
                        pltpu.VMEM((n_pat, seq, LANES), F32)],
        compiler_params=pltpu.CompilerParams(
            dimension_semantics=("arbitrary", "arbitrary"), vmem_limit_bytes=VMEM_LIMIT),
        name="dilated_attn",
    )(att_qkv, att_qkv, att_qkv, att_g, q_norm_w, k_norm_w, bias)


def _out_proj_kernel(x_ref, ydn_ref, yatt_ref, w_ref, o_ref):
    o_ref[...] = (x_ref[...]
                  + jnp.dot(ydn_ref[...], w_ref[0:D_DN, :], preferred_element_type=F32)
                  + jnp.dot(yatt_ref[...], w_ref[D_DN:D_MIX, :], preferred_element_type=F32))


def _out_proj(x2d, y_dn, y_att, w_out, tm=512):
    n = x2d.shape[0]
    row = lambda i: (i, 0)
    return pl.pallas_call(
        _out_proj_kernel,
        grid=(n // tm,),
        in_specs=[pl.BlockSpec((tm, D_MODEL), row),
                  pl.BlockSpec((tm, D_DN), row),
                  pl.BlockSpec((tm, D_ATT), row),
                  pl.BlockSpec((D_MIX, D_MODEL), lambda i: (0, 0))],
        out_specs=pl.BlockSpec((tm, D_MODEL), row),
        out_shape=jax.ShapeDtypeStruct((n, D_MODEL), F32),
        compiler_params=pltpu.CompilerParams(
            dimension_semantics=("arbitrary",), vmem_limit_bytes=VMEM_LIMIT),
        name="out_proj",
    )(x2d, y_dn, y_att, w_out)


def _pad_lanes(v, offset):
    v = v.reshape(1, -1).astype(F32)
    return jnp.pad(v, ((0, 0), (offset, LANES - offset - v.shape[1])))


def _layer(x, norm_w, w_in, conv_w, a_log, dt_bias, dn_norm_w, q_norm_w, k_norm_w, bias, w_out):
    bsz, seq, _ = x.shape
    x2d = x.reshape(bsz * seq, D_MODEL)
    c_dn = 4 * D_DN
    c_ba = c_dn + 2 * DN_HEADS
    w_ba = jnp.pad(w_in[:, c_dn:c_ba], ((0, 0), (0, LANES - 2 * DN_HEADS)))
    w_all = jnp.concatenate([w_in[:, :c_dn], w_in[:, c_ba:], w_ba], axis=1).astype(BF16)

    dn_qkv, dn_z, ba, att_qkv, att_g = _in_proj(x2d, norm_w.reshape(1, D_MODEL), w_all)
    dn_qkv = dn_qkv.reshape(bsz, seq, 3 * D_DN)
    dn_z = dn_z.reshape(bsz, seq, D_DN)
    ba = ba.reshape(bsz, seq, LANES)
    att_qkv = att_qkv.reshape(bsz, seq, 3 * D_ATT)
    att_g = att_g.reshape(bsz, seq, D_ATT)

    y_dn = _deltanet(dn_qkv, dn_z, ba, conv_w, _pad_lanes(a_log, DN_HEADS), _pad_lanes(dt_bias, DN_HEADS),
                     dn_norm_w.reshape(1, DN_HEAD_DIM))
    y_att = _dilated_attn(att_qkv, att_g,
                          jnp.tile(q_norm_w.reshape(1, ATT_HEAD_DIM), (1, 2)),
                          jnp.tile(k_norm_w.reshape(1, ATT_HEAD_DIM), (1, 2)), bias)
    out = _out_proj(x2d, y_dn.reshape(bsz * seq, D_DN), y_att.reshape(bsz * seq, D_ATT),
                    w_out.astype(BF16))
    return out.reshape(bsz, seq, D_MODEL)


def kernel(x, norm_w, w_in, conv_w, a_log, dt_bias, dn_norm_w, q_norm_w, k_norm_w, rel_bias, w_out):
    bias = _bias_tables(rel_bias.astype(F32))
    for layer in range(norm_w.shape[0]):
        x = _layer(x, norm_w[layer], w_in[layer], conv_w[layer], a_log[layer], dt_bias[layer],
                   dn_norm_w[layer], q_norm_w[layer], k_norm_w[layer], bias, w_out[layer])
    return x
```

```python
import functools
import math

import numpy as np
import jax
import jax.numpy as jnp
from jax import lax
from jax.experimental import pallas as pl
from jax.experimental.pallas import tpu as pltpu

D_MODEL = 1024
D_DN = 512
DN_HEADS = 4
DN_HEAD_DIM = 128
CONV_WIDTH = 4
CHUNK = 64
D_ATT = 512
ATT_HEADS = 8
ATT_HEAD_DIM = 64
DILATED_PATTERNS = ((128, 1), (512, 4), (2048, 16))
N_BUCKETS = 32
MAX_DISTANCE = 2048
D_MIX = D_DN + D_ATT
EPS = 1e-6

LANES = 128
SUBLANES = 8
BLK = 128
ATT_GROUP = 4
SUPER = 128
DN_UNROLL = 4
SUB = 8
NEG = -1e30
VMEM_LIMIT = 56 * 1024 * 1024

F32 = jnp.float32
BF16 = jnp.bfloat16


def _mm(a, b):
    return jnp.dot(a.astype(BF16), b.astype(BF16), preferred_element_type=F32)


def _mm_nt(a, b):
    return lax.dot_general(a.astype(BF16), b.astype(BF16), (((1,), (1,)), ((), ())),
                           preferred_element_type=F32)


def _sigmoid(x):
    return 0.5 * jnp.tanh(0.5 * x) + 0.5


def _in_proj_kernel(x_ref, nw_ref, w_ref, dnqkv_ref, dnz_ref, ba_ref, attqkv_ref, attg_ref):
    x = x_ref[...]
    ms = jnp.mean(x * x, axis=-1, keepdims=True)
    h = (x * lax.rsqrt(ms + EPS) * nw_ref[...]).astype(BF16)
    c0 = 3 * D_DN
    c1 = c0 + D_DN
    c2 = c1 + 3 * D_ATT
    c3 = c2 + D_ATT
    dnqkv_ref[...] = jnp.dot(h, w_ref[:, 0:c0], preferred_element_type=F32)
    dnz_ref[...] = jnp.dot(h, w_ref[:, c0:c1], preferred_element_type=F32).astype(BF16)
    attqkv_ref[...] = jnp.dot(h, w_ref[:, c1:c2], preferred_element_type=F32)
    attg_ref[...] = jnp.dot(h, w_ref[:, c2:c3], preferred_element_type=F32).astype(BF16)
    ba_ref[...] = jnp.dot(h, w_ref[:, c3:c3 + LANES], preferred_element_type=F32)


def _in_proj(x2d, norm_w, w_all, tm=512):
    n = x2d.shape[0]
    ncol = w_all.shape[1]
    row = lambda i: (i, 0)
    fixed = lambda i: (0, 0)
    return pl.pallas_call(
        _in_proj_kernel,
        grid=(n // tm,),
        in_specs=[pl.BlockSpec((tm, D_MODEL), row),
                  pl.BlockSpec((1, D_MODEL), fixed),
                  pl.BlockSpec((D_MODEL, ncol), fixed)],
        out_specs=[pl.BlockSpec((tm, 3 * D_DN), row),
                   pl.BlockSpec((tm, D_DN), row),
                   pl.BlockSpec((tm, LANES), row),
                   pl.BlockSpec((tm, 3 * D_ATT), row),
                   pl.BlockSpec((tm, D_ATT), row)],
        out_shape=[jax.ShapeDtypeStruct((n, 3 * D_DN), F32),
                   jax.ShapeDtypeStruct((n, D_DN), BF16),
                   jax.ShapeDtypeStruct((n, LANES), F32),
                   jax.ShapeDtypeStruct((n, 3 * D_ATT), F32),
                   jax.ShapeDtypeStruct((n, D_ATT), BF16)],
        compiler_params=pltpu.CompilerParams(
            dimension_semantics=("arbitrary",), vmem_limit_bytes=VMEM_LIMIT),
        name="in_proj",
    )(x2d, norm_w, w_all)


def _t5_bucket(dist):
    max_exact = N_BUCKETS // 2
    d = np.maximum(dist, 1).astype(np.float64)
    large = max_exact + (np.log(d / max_exact) / math.log(MAX_DISTANCE / max_exact)
                         * (N_BUCKETS - max_exact)).astype(np.int32)
    large = np.minimum(large, N_BUCKETS - 1)
    return np.where(dist < max_exact, dist, large).astype(np.int32)


def _bucket_maps():
    maps = []
    for window, r in DILATED_PATTERNS:
        w_steps = window // r
        assert w_steps == BLK
        qi = np.arange(BLK)[:, None]
        kj = np.arange(2 * BLK)[None, :]
        step = qi - kj + BLK
        band = (step >= 0) & (step <= w_steps)
        buckets = _t5_bucket(np.clip(step, 0, None) * r)
        maps.append(np.where(band, buckets, -1).astype(np.int32))
    return np.stack(maps)


def _bias_kernel(rel_ref, bkt_ref, o_ref):
    h = pl.program_id(1)
    bkt = bkt_ref[0]
    acc = jnp.full(bkt.shape, NEG, F32)
    for b in range(N_BUCKETS):
        acc = jnp.where(bkt == b, rel_ref[h, b], acc)
    o_ref[0, 0, 0] = acc
    col = lax.broadcasted_iota(jnp.int32, bkt.shape, 1)
    o_ref[0, 0, 1] = jnp.where(col < BLK, NEG, acc)


def _bias_tables(rel_bias):
    bkt = jnp.asarray(_bucket_maps())
    n_pat = bkt.shape[0]
    return pl.pallas_call(
        _bias_kernel,
        grid=(n_pat, ATT_HEADS),
        in_specs=[pl.BlockSpec(memory_space=pltpu.SMEM),
                  pl.BlockSpec((1, BLK, 2 * BLK), lambda p, h: (p, 0, 0))],
        out_specs=pl.BlockSpec((1, 1, 2, BLK, 2 * BLK), lambda p, h: (p, h, 0, 0, 0)),
        out_shape=jax.ShapeDtypeStruct((n_pat, ATT_HEADS, 2, BLK, 2 * BLK), F32),
        name="bias_tables",
    )(rel_bias, bkt)


def _dn_kernel(*refs, tile):
    nh = DN_HEADS
    n_stream = 3 * nh
    x_refs = refs[:n_stream]
    cw_ref, ba_ref, z_ref, alog_ref, dtb_ref, nw_ref, o_ref, state_s, xs = refs[n_stream:]
    n_chunk = SUPER // CHUNK
    heads = range(nh)
    lead = SUBLANES

    @pl.when(pl.program_id(1) == 0)
    def _():
        state_s[...] = jnp.zeros(state_s.shape, F32)
        xs[:, 0:lead, :] = jnp.zeros((n_stream, lead, LANES), F32)

    def copy_body(i, carry):
        r0 = pl.multiple_of(i * SUPER, SUPER)
        for j in range(n_stream):
            xs[j, pl.ds(r0 + lead, SUPER), :] = x_refs[j][0, pl.ds(r0, SUPER), :]
        return carry
    lax.fori_loop(0, tile // SUPER, copy_body, 0)

    ri = lax.broadcasted_iota(jnp.int32, (SUPER, SUPER), 0)
    ci = lax.broadcasted_iota(jnp.int32, (SUPER, SUPER), 1)
    same_chunk = (ri // CHUNK) == (ci // CHUNK)
    m_incl = same_chunk & (ri >= ci)
    m_strict = same_chunk & (ri > ci)
    m_sub = (ri // SUB) == (ci // SUB)
    eye = (ri == ci).astype(F32)
    chunk_cols = [(ci // CHUNK) == c for c in range(n_chunk)]
    cum_mat = (same_chunk & (ri <= ci)).astype(BF16)
    is_beta_row = lax.broadcasted_iota(jnp.int32, (SUBLANES, SUPER), 0) < nh

    a_coef = -jnp.exp(alog_ref[...])
    dt_b = dtb_ref[...]
    nw = nw_ref[...]
    cw = cw_ref[...]

    def conv_silu(j, r0):
        cwj = cw[:, j * LANES:(j + 1) * LANES]
        y = xs[j, pl.ds(r0 + lead, SUPER), :] * cwj[CONV_WIDTH - 1:CONV_WIDTH]
        for s in range(1, CONV_WIDTH):
            y = y + xs[j, pl.ds(r0 + lead - s, SUPER), :] * cwj[CONV_WIDTH - 1 - s:CONV_WIDTH - s]
        return y * _sigmoid(y)

    def l2n(x):
        return x * lax.rsqrt(jnp.sum(x * x, axis=-1, keepdims=True) + EPS)

    def lane_bcast(x, j):
        return jnp.broadcast_to(x[:, j:j + 1], x.shape)

    def row_bcast(x, i, n_rows):
        return jnp.broadcast_to(x[i:i + 1, :], (n_rows, x.shape[1]))

    def gate_rows(r0):
        pre = ba_ref[0, pl.ds(r0, SUPER), :].T[0:SUBLANES, :]
        xg = pre + dt_b
        g = a_coef * (jnp.maximum(xg, 0.0) + jnp.log(1.0 + jnp.exp(-jnp.abs(xg))))
        g_hi = g.astype(BF16)
        g_r = g - g_hi.astype(F32)
        g_mid = g_r.astype(BF16)
        g_lo = (g_r - g_mid.astype(F32)).astype(BF16)
        cs = jnp.dot(jnp.concatenate([g_hi, g_mid, g_lo], axis=0), cum_mat,
                     preferred_element_type=F32)
        gc_r = cs[0:SUBLANES] + cs[SUBLANES:2 * SUBLANES] + cs[2 * SUBLANES:3 * SUBLANES]
        gates = jnp.where(is_beta_row, _sigmoid(pre), gc_r)
        gates_c = jnp.concatenate([gates, jnp.zeros((SUPER - SUBLANES, SUPER), F32)], axis=0).T
        return gc_r, gates_c

    def body(it, carry):
        r0s = [pl.multiple_of((it * DN_UNROLL + u) * SUPER, SUPER) for u in range(DN_UNROLL)]
        items = [(u, h) for u in range(DN_UNROLL) for h in heads]
        idx = range(len(items))
        gate = [gate_rows(r0) for r0 in r0s]

        q = [l2n(conv_silu(h, r0s[u])) * (DN_HEAD_DIM ** -0.5) for u, h in items]
        k = [l2n(conv_silu(nh + h, r0s[u])) for u, h in items]
        v = [conv_silu(2 * nh + h, r0s[u]) for u, h in items]

        beta = [lane_bcast(gate[u][1], h) for u, h in items]
        gc = [lane_bcast(gate[u][1], nh + h) for u, h in items]
        decay = [jnp.exp(jnp.where(m_incl, gc[i] - row_bcast(gate[u][0], nh + h, SUPER), NEG))
                 for i, (u, h) in enumerate(items)]
        gc_last = [jnp.concatenate([row_bcast(gc[i], c * CHUNK + CHUNK - 1, CHUNK)
                                    for c in range(n_chunk)], axis=0) for i in idx]
        e_gc = [jnp.exp(gc[i]) for i in idx]
        kb = [k[i] * beta[i] for i in idx]

        kq = [_mm_nt(jnp.concatenate([kb[i], q[i]], axis=0), k[i]) for i in idx]
        a_mat = [jnp.where(m_strict, kq[i][:SUPER] * decay[i], 0.0) for i in idx]
        attn = [(kq[i][SUPER:] * decay[i]).astype(BF16) for i in idx]
        rhs = [jnp.concatenate([v[i] * beta[i], kb[i] * e_gc[i]], axis=1).astype(BF16) for i in idx]

        d_blk = [jnp.where(m_sub, a_mat[i], 0.0) for i in idx]
        l_b = [(a_mat[i] - d_blk[i]).astype(BF16) for i in idx]
        x1 = [-d_blk[i] for i in idx]
        x1b = [x1[i].astype(BF16) for i in idx]
        x2 = [_mm(x1b[i], x1b[i]) for i in idx]
        x2b = [x2[i].astype(BF16) for i in idx]
        x4 = [_mm(x2b[i], x2b[i]) for i in idx]
        p1 = [eye + x1[i] + x2[i] + _mm(x1b[i], x2b[i]) for i in idx]
        t_d = [p1[i] + _mm(p1[i], x4[i]) for i in idx]
        t_db = [t_d[i].astype(BF16) for i in idx]
        y1 = [-_mm(t_db[i], l_b[i]) for i in idx]
        td_rhs = [_mm(t_db[i], rhs[i]).astype(BF16) for i in idx]
        y1b = [y1[i].astype(BF16) for i in idx]
        y2 = [_mm(y1b[i], y1b[i]) for i in idx]
        y2b = [y2[i].astype(BF16) for i in idx]
        y4 = [_mm(y2b[i], y2b[i]) for i in idx]
        q1 = [eye + y1[i] + y2[i] + _mm(y1b[i], y2b[i]) for i in idx]
        q2 = [q1[i] + _mm(q1[i], y4[i]) for i in idx]

        uw = [_mm(q2[i], td_rhs[i]).astype(BF16) for i in idx]
        aw = [_mm(attn[i], uw[i]) for i in idx]
        q_t = [(q[i] * e_gc[i] - aw[i][:, LANES:]).astype(BF16) for i in idx]
        kt_t = [(k[i] * jnp.exp(gc_last[i] - gc[i])).T.astype(BF16) for i in idx]
        kw = [[_mm(jnp.where(chunk_cols[c], kt_t[i], jnp.zeros_like(kt_t[i])), uw[i])
               for c in range(n_chunk)] for i in idx]

        state = [state_s[h] for h in heads]
        for u in range(DN_UNROLL):
            outs = [[] for _ in heads]
            for c in range(n_chunk):
                sl = slice(c * CHUNK, (c + 1) * CHUNK)
                s_bf = [state[h].astype(BF16) for h in heads]
                for h in heads:
                    i = u * nh + h
                    outs[h].append(_mm(q_t[i][sl], s_bf[h]) + aw[i][sl, :LANES])
                for h in heads:
                    i = u * nh + h
                    g_last = jnp.exp(row_bcast(gc[i], c * CHUNK + CHUNK - 1, LANES))
                    state[h] = (state[h] * g_last - _mm(kw[i][c][:, LANES:], s_bf[h])
                                + kw[i][c][:, :LANES])
            rows = pl.ds(r0s[u], SUPER)
            for h in heads:
                o = jnp.concatenate(outs[h], axis=0)
                ms = jnp.mean(o * o, axis=-1, keepdims=True)
                z = z_ref[0, rows, h * LANES:(h + 1) * LANES].astype(F32)
                y = o * lax.rsqrt(ms + EPS) * nw * (z * _sigmoid(z))
                o_ref[0, rows, h * LANES:(h + 1) * LANES] = y.astype(o_ref.dtype)
        for h in heads:
            state_s[h] = state[h]
        return carry

    lax.fori_loop(0, tile // (SUPER * DN_UNROLL), body, 0)
    xs[:, 0:lead, :] = xs[:, tile:tile + lead, :]


def _deltanet(dn_qkv, dn_z, ba, conv_w, a_log, dt_bias, dn_norm_w, tile=1024):
    bsz, seq, _ = dn_qkv.shape
    n_stream = 3 * DN_HEADS
    seq_tile = lambda b, t: (b, t, 0)
    fixed = lambda b, t: (0, 0)

    def slab(j):
        return pl.BlockSpec((1, tile, LANES), lambda b, t: (b, t, j))

    return pl.pallas_call(
        functools.partial(_dn_kernel, tile=tile),
        grid=(bsz, seq // tile),
        in_specs=[slab(j) for j in range(n_stream)]
        + [pl.BlockSpec((CONV_WIDTH, 3 * D_DN), fixed),
           pl.BlockSpec((1, tile, LANES), seq_tile),
           pl.BlockSpec((1, tile, D_DN), seq_tile),
           pl.BlockSpec((SUBLANES, LANES), fixed),
           pl.BlockSpec((SUBLANES, LANES), fixed),
           pl.BlockSpec((1, LANES), fixed)],
        out_specs=pl.BlockSpec((1, tile, D_DN), seq_tile),
        out_shape=jax.ShapeDtypeStruct((bsz, seq, D_DN), BF16),
        scratch_shapes=[pltpu.VMEM((DN_HEADS, DN_HEAD_DIM, DN_HEAD_DIM), F32),
                        pltpu.VMEM((n_stream, tile + SUBLANES, LANES), F32)],
        compiler_params=pltpu.CompilerParams(
            dimension_semantics=("arbitrary", "arbitrary"), vmem_limit_bytes=VMEM_LIMIT),
        name="deltanet",
    )(*([dn_qkv] * n_stream), conv_w, ba, dn_z, a_log, dt_bias, dn_norm_w)


def _att_kernel(q_ref, k_ref, v_ref, g_ref, qw_ref, kw_ref, bias_ref, o_ref,
                qn_s, kn_s, qp, kp, vp, m_s, l_s, acc_s, *, seq):
    tile = 512
    lane = lax.broadcasted_iota(jnp.int32, (BLK, LANES), 1)
    head0 = lane < ATT_HEAD_DIM
    ri = lax.broadcasted_iota(jnp.int32, (LANES, LANES), 0)
    ci = lax.broadcasted_iota(jnp.int32, (LANES, LANES), 1)
    same_head = ((ri // ATT_HEAD_DIM) == (ci // ATT_HEAD_DIM)).astype(BF16)

    def head_norm(x, w):
        x2 = x * x
        hi = x2.astype(BF16)
        lo = (x2 - hi.astype(F32)).astype(BF16)
        ss = (jnp.dot(hi, same_head, preferred_element_type=F32)
              + jnp.dot(lo, same_head, preferred_element_type=F32))
        return x * lax.rsqrt(ss * (1.0 / ATT_HEAD_DIM) + EPS) * w

    def norm_body(i, carry):
        rows = pl.ds(pl.multiple_of(i * tile, tile), tile)
        qn_s[rows, :] = head_norm(q_ref[0, rows, :], qw_ref[...]) * (ATT_HEAD_DIM ** -0.5)
        kn_s[rows, :] = head_norm(k_ref[0, rows, :], kw_ref[...])
        return carry

    lax.fori_loop(0, seq // tile, norm_body, 0)

    kp[pl.ds(0, BLK), :] = jnp.zeros((BLK, LANES), BF16)
    vp[pl.ds(0, BLK), :] = jnp.zeros((BLK, LANES), BF16)
    ones_v = jnp.ones((2 * BLK, LANES), BF16)
    zero_q = jnp.zeros((BLK, LANES), BF16)

    def group(p, r, n_blk, bi0):
        idx = []
        for g in range(ATT_GROUP):
            bi = bi0 + g
            idx.append((pl.multiple_of(bi * BLK, BLK), bi // n_blk, bi % n_blk))
        scores = []
        for base, res, n in idx:
            q = qp[pl.ds(base, BLK), :]
            q2 = jnp.concatenate([jnp.where(head0, q, zero_q), jnp.where(head0, zero_q, q)], axis=0)
            k = kp[pl.ds(base, 2 * BLK), :]
            first = jnp.asarray(n == 0, jnp.int32)
            bias = jnp.concatenate([bias_ref[p, 0, first], bias_ref[p, 1, first]], axis=0)
            scores.append(lax.dot_general(q2, k, (((1,), (1,)), ((), ())),
                                          preferred_element_type=F32) + bias)
        maxes, probs = [], []
        for s in scores:
            m = jnp.max(s, axis=-1, keepdims=True)
            maxes.append(m)
            probs.append(jnp.exp(s - m).astype(BF16))
        for (base, res, n), m, e in zip(idx, maxes, probs):
            v2 = jnp.concatenate([vp[pl.ds(base, 2 * BLK), :], ones_v], axis=1)
            pv = jnp.dot(e, v2, preferred_element_type=F32)
            tok0 = res + n * (BLK * r)
            rows = pl.ds(tok0, BLK) if r == 1 else pl.ds(tok0, BLK, stride=r)
            m_s[p, rows, :] = jnp.where(head0, m[:BLK], m[BLK:])
            l_s[p, rows, :] = jnp.where(head0, pv[:BLK, LANES:], pv[BLK:, LANES:])
            acc_s[p, rows, :] = jnp.where(head0, pv[:BLK, :LANES], pv[BLK:, :LANES])

    for p, (window, r) in enumerate(DILATED_PATTERNS):
        l_sub = seq // r
        n_blk = l_sub // BLK

        def perm_body(i, carry, r=r, n_blk=n_blk):
            dst = pl.multiple_of(i * BLK, BLK)
            if r == 1:
                src = pl.ds(dst, BLK)
            else:
                src = pl.ds(i // n_blk + (i % n_blk) * (BLK * r), BLK, stride=r)
            qp[pl.ds(dst, BLK), :] = qn_s[src, :].astype(BF16)
            kp[pl.ds(dst + BLK, BLK), :] = kn_s[src, :].astype(BF16)
            vp[pl.ds(dst + BLK, BLK), :] = v_ref[0, src, :].astype(BF16)
            return carry
        lax.fori_loop(0, seq // BLK, perm_body, 0)

        def group_body(i, carry, p=p, r=r, n_blk=n_blk):
            group(p, r, n_blk, i * ATT_GROUP)
            return carry
        lax.fori_loop(0, seq // (BLK * ATT_GROUP), group_body, 0)

    def out_body(i, carry):
        rows = pl.ds(pl.multiple_of(i * tile, tile), tile)
        m1, m2, m3 = m_s[0, rows, :], m_s[1, rows, :], m_s[2, rows, :]
        m = jnp.maximum(jnp.maximum(m1, m2), m3)
        w1, w2, w3 = jnp.exp(m1 - m), jnp.exp(m2 - m), jnp.exp(m3 - m)
        num = w1 * acc_s[0, rows, :] + w2 * acc_s[1, rows, :] + w3 * acc_s[2, rows, :]
        den = w1 * l_s[0, rows, :] + w2 * l_s[1, rows, :] + w3 * l_s[2, rows, :]
        g = g_ref[0, rows, :].astype(F32)
        o_ref[0, rows, :] = (num / den * (g * _sigmoid(g))).astype(o_ref.dtype)
        return carry
    lax.fori_loop(0, seq // tile, out_body, 0)


def _dilated_attn(att_qkv, att_g, q_norm_w, k_norm_w, bias):
    bsz, seq, _ = att_qkv.shape
    n_pair = ATT_HEADS // 2
    n_pat = bias.shape[0]

    def col(off):
        return lambda b, j: (b, 0, off + j)

    fixed = lambda b, j: (0, 0)
    return pl.pallas_call(
        functools.partial(_att_kernel, seq=seq),
        grid=(bsz, n_pair),
        in_specs=[pl.BlockSpec((1, seq, LANES), col(0)),
                  pl.BlockSpec((1, seq, LANES), col(n_pair)),
                  pl.BlockSpec((1, seq, LANES), col(2 * n_pair)),
                  pl.BlockSpec((1, seq, LANES), lambda b, j: (b, 0, j)),
                  pl.BlockSpec((1, LANES), fixed),
                  pl.BlockSpec((1, LANES), fixed),
                  pl.BlockSpec((n_pat, 2, 2, BLK, 2 * BLK), lambda b, j: (0, j, 0, 0, 0))],
        out_specs=pl.BlockSpec((1, seq, LANES), lambda b, j: (b, 0, j)),
        out_shape=jax.ShapeDtypeStruct((bsz, seq, D_ATT), BF16),
        scratch_shapes=[pltpu.VMEM((seq, LANES), F32),
                        pltpu.VMEM((seq, LANES), F32),
                        pltpu.VMEM((seq, LANES), BF16),
                        pltpu.VMEM((seq + BLK, LANES), BF16),
                        pltpu.VMEM((seq + BLK, LANES), BF16),
                        pltpu.VMEM((n_pat, seq, LANES), F32),
                        pltpu.VMEM((n_pat, seq, LANES), F32),
                        pltpu.VMEM((n_pat, seq, LANES), F32)],
        compiler_params=pltpu.CompilerParams(
            dimension_semantics=("arbitrary", "arbitrary"), vmem_limit_bytes=VMEM_LIMIT),
        name="dilated_attn",
    )(att_qkv, att_qkv, att_qkv, att_g, q_norm_w, k_norm_w, bias)


def _out_proj_kernel(x_ref, ydn_ref, yatt_ref, w_ref, o_ref):
    o_ref[...] = (x_ref[...]
                  + jnp.dot(ydn_ref[...], w_ref[0:D_DN, :], preferred_element_type=F32)
                  + jnp.dot(yatt_ref[...], w_ref[D_DN:D_MIX, :], preferred_element_type=F32))


def _out_proj(x2d, y_dn, y_att, w_out, tm=512):
    n = x2d.shape[0]
    row = lambda i: (i, 0)
    return pl.pallas_call(
        _out_proj_kernel,
        grid=(n // tm,),
        in_specs=[pl.BlockSpec((tm, D_MODEL), row),
                  pl.BlockSpec((tm, D_DN), row),
                  pl.BlockSpec((tm, D_ATT), row),
                  pl.BlockSpec((D_MIX, D_MODEL), lambda i: (0, 0))],
        out_specs=pl.BlockSpec((tm, D_MODEL), row),
        out_shape=jax.ShapeDtypeStruct((n, D_MODEL), F32),
        compiler_params=pltpu.CompilerParams(
            dimension_semantics=("arbitrary",), vmem_limit_bytes=VMEM_LIMIT),
        name="out_proj",
    )(x2d, y_dn, y_att, w_out)


def _gate_rows(v):
    col = jnp.pad(v.reshape(-1, 1).astype(F32), ((DN_HEADS, SUBLANES - 2 * DN_HEADS), (0, 0)))
    return jnp.broadcast_to(col, (SUBLANES, LANES))


def _layer(x, norm_w, w_in, conv_w, a_log, dt_bias, dn_norm_w, q_norm_w, k_norm_w, bias, w_out):
    bsz, seq, _ = x.shape
    x2d = x.reshape(bsz * seq, D_MODEL)
    c_dn = 4 * D_DN
    c_ba = c_dn + 2 * DN_HEADS
    w_ba = jnp.pad(w_in[:, c_dn:c_ba], ((0, 0), (0, LANES - 2 * DN_HEADS)))
    w_all = jnp.concatenate([w_in[:, :c_dn], w_in[:, c_ba:], w_ba], axis=1).astype(BF16)

    dn_qkv, dn_z, ba, att_qkv, att_g = _in_proj(x2d, norm_w.reshape(1, D_MODEL), w_all)
    dn_qkv = dn_qkv.reshape(bsz, seq, 3 * D_DN)
    dn_z = dn_z.reshape(bsz, seq, D_DN)
    ba = ba.reshape(bsz, seq, LANES)
    att_qkv = att_qkv.reshape(bsz, seq, 3 * D_ATT)
    att_g = att_g.reshape(bsz, seq, D_ATT)

    y_dn = _deltanet(dn_qkv, dn_z, ba, conv_w, _gate_rows(a_log), _gate_rows(dt_bias),
                     dn_norm_w.reshape(1, DN_HEAD_DIM))
    y_att = _dilated_attn(att_qkv, att_g,
                          jnp.tile(q_norm_w.reshape(1, ATT_HEAD_DIM), (1, 2)),
                          jnp.tile(k_norm_w.reshape(1, ATT_HEAD_DIM), (1, 2)), bias)
    out = _out_proj(x2d, y_dn.reshape(bsz * seq, D_DN), y_att.reshape(bsz * seq, D_ATT),
                    w_out.astype(BF16))
    return out.reshape(bsz, seq, D_MODEL)


def kernel(x, norm_w, w_in, conv_w, a_log, dt_bias, dn_norm_w, q_norm_w, k_norm_w, rel_bias, w_out):
    bias = _bias_tables(rel_bias.astype(F32))
    for layer in range(norm_w.shape[0]):
        x = _layer(x, norm_w[layer], w_in[layer], conv_w[layer], a_log[layer], dt_bias[layer],
                   dn_norm_w[layer], q_norm_w[layer], k_norm_w[layer], bias, w_out[layer])
    return x
```

```python
import functools
import math

import numpy as np
import jax
import jax.numpy as jnp
from jax import lax
from jax.experimental import pallas as pl
from jax.experimental.pallas import tpu as pltpu

D_MODEL = 1024
D_DN = 512
DN_HEADS = 4
DN_HEAD_DIM = 128
CONV_WIDTH = 4
CHUNK = 64
D_ATT = 512
ATT_HEADS = 8
ATT_HEAD_DIM = 64
DILATED_PATTERNS = ((128, 1), (512, 4), (2048, 16))
N_BUCKETS = 32
MAX_DISTANCE = 2048
D_MIX = D_DN + D_ATT
EPS = 1e-6

LANES = 128
SUBLANES = 8
BLK = 128
ATT_GROUP = 4
SUPER = 128
DN_UNROLL = 4
SUB = 8
NEG = -1e30
LOG2E = math.log2(math.e)
N_BIAS_VARIANTS = 3
VMEM_LIMIT = 56 * 1024 * 1024

F32 = jnp.float32
BF16 = jnp.bfloat16


def _mm(a, b):
    return jnp.dot(a.astype(BF16), b.astype(BF16), preferred_element_type=F32)


def _mm_nt(a, b):
    return lax.dot_general(a.astype(BF16), b.astype(BF16), (((1,), (1,)), ((), ())),
                           preferred_element_type=F32)


def _sigmoid(x):
    return 0.5 * jnp.tanh(0.5 * x) + 0.5


def _in_proj_kernel(x_ref, nw_ref, w_ref, qw_ref, kw_ref,
                    dnqkv_ref, dnz_ref, ba_ref, attg_ref, *att_refs_and_scratch):
    att_refs = att_refs_and_scratch[:len(DILATED_PATTERNS)]
    att_s = att_refs_and_scratch[len(DILATED_PATTERNS)]
    tm = x_ref.shape[1]
    x = x_ref[0]
    ms = jnp.mean(x * x, axis=-1, keepdims=True)
    h = (x * lax.rsqrt(ms + EPS) * nw_ref[...]).astype(BF16)
    c0 = 3 * D_DN
    c1 = c0 + D_DN
    c2 = c1 + 3 * D_ATT
    c3 = c2 + D_ATT

    att = jnp.dot(h, w_ref[:, c1:c2], preferred_element_type=F32)
    head0 = lax.broadcasted_iota(jnp.int32, (tm, LANES), 1) < ATT_HEAD_DIM
    n_slab = 3 * D_ATT // LANES
    for j in range(n_slab):
        xs = att[:, j * LANES:(j + 1) * LANES]
        if j < 2 * D_ATT // LANES:
            is_q = j < D_ATT // LANES
            x2 = xs * xs
            s0 = jnp.sum(jnp.where(head0, x2, 0.0), axis=-1, keepdims=True)
            s1 = jnp.sum(jnp.where(head0, 0.0, x2), axis=-1, keepdims=True)
            ss = jnp.where(head0, s0, s1)
            wn = qw_ref[...] if is_q else kw_ref[...]
            xs = xs * lax.rsqrt(ss * (1.0 / ATT_HEAD_DIM) + EPS) * wn
        att_s[j] = xs
    for j in range(n_slab):
        cols = slice(j * LANES, (j + 1) * LANES)
        for o_ref, (_, r) in zip(att_refs, DILATED_PATTERNS):
            rows_per = tm // r
            for res in range(r):
                src = pl.ds(0, tm) if r == 1 else pl.ds(res, rows_per, stride=r)
                o_ref[0, res, :, cols] = att_s[j, src, :].astype(BF16)

    dnqkv_ref[0] = jnp.dot(h, w_ref[:, 0:c0], preferred_element_type=F32)
    dnz_ref[0] = jnp.dot(h, w_ref[:, c0:c1], preferred_element_type=F32).astype(BF16)
    attg_ref[0] = jnp.dot(h, w_ref[:, c2:c3], preferred_element_type=F32).astype(BF16)
    ba_ref[0] = jnp.dot(h, w_ref[:, c3:c3 + LANES], preferred_element_type=F32)


def _in_proj(x, norm_w, w_all, qw, kw, tm=512):
    bsz, seq, _ = x.shape
    ncol = w_all.shape[1]
    row = lambda b, i: (b, i, 0)
    fixed = lambda b, i: (0, 0)
    att_specs = [pl.BlockSpec((1, r, tm // r, 3 * D_ATT), lambda b, i: (b, 0, i, 0))
                 for _, r in DILATED_PATTERNS]
    att_shapes = [jax.ShapeDtypeStruct((bsz, r, seq // r, 3 * D_ATT), BF16)
                  for _, r in DILATED_PATTERNS]
    return pl.pallas_call(
        _in_proj_kernel,
        grid=(bsz, seq // tm),
        in_specs=[pl.BlockSpec((1, tm, D_MODEL), row),
                  pl.BlockSpec((1, D_MODEL), fixed),
                  pl.BlockSpec((D_MODEL, ncol), fixed),
                  pl.BlockSpec((1, LANES), fixed),
                  pl.BlockSpec((1, LANES), fixed)],
        out_specs=[pl.BlockSpec((1, tm, 3 * D_DN), row),
                   pl.BlockSpec((1, tm, D_DN), row),
                   pl.BlockSpec((1, tm, LANES), row),
                   pl.BlockSpec((1, tm, D_ATT), row)] + att_specs,
        out_shape=[jax.ShapeDtypeStruct((bsz, seq, 3 * D_DN), F32),
                   jax.ShapeDtypeStruct((bsz, seq, D_DN), BF16),
                   jax.ShapeDtypeStruct((bsz, seq, LANES), F32),
                   jax.ShapeDtypeStruct((bsz, seq, D_ATT), BF16)] + att_shapes,
        scratch_shapes=[pltpu.VMEM((3 * D_ATT // LANES, tm, LANES), F32)],
        compiler_params=pltpu.CompilerParams(
            dimension_semantics=("arbitrary", "arbitrary"), vmem_limit_bytes=VMEM_LIMIT),
        name="in_proj",
    )(x, norm_w, w_all, qw, kw)


def _t5_bucket(dist):
    max_exact = N_BUCKETS // 2
    d = np.maximum(dist, 1).astype(np.float64)
    large = max_exact + (np.log(d / max_exact) / math.log(MAX_DISTANCE / max_exact)
                         * (N_BUCKETS - max_exact)).astype(np.int32)
    large = np.minimum(large, N_BUCKETS - 1)
    return np.where(dist < max_exact, dist, large).astype(np.int32)


def _bucket_maps():
    maps = []
    for window, r in DILATED_PATTERNS:
        w_steps = window // r
        assert w_steps == BLK
        qi = np.arange(BLK)[:, None]
        kj = np.arange(2 * BLK)[None, :]
        step = qi - kj + BLK
        band = (step >= 0) & (step <= w_steps)
        buckets = _t5_bucket(np.clip(step, 0, None) * r)
        maps.append(np.where(band, buckets, -1).astype(np.int32))
    return np.stack(maps)


def _bias_kernel(rel_ref, bkt_ref, o_ref):
    h = pl.program_id(1)
    bkt = bkt_ref[0]
    acc = jnp.full(bkt.shape, NEG, F32)
    for b in range(N_BUCKETS):
        acc = jnp.where(bkt == b, rel_ref[h, b] * LOG2E, acc)
    o_ref[0, 0, 0] = acc
    col = lax.broadcasted_iota(jnp.int32, bkt.shape, 1)
    o_ref[0, 0, 1] = jnp.where(col < BLK, NEG, acc)
    o_ref[0, 0, 2] = jnp.concatenate([acc[:, BLK:], jnp.full((BLK, BLK), NEG, F32)], axis=1)


def _bias_tables(rel_bias):
    bkt = jnp.asarray(_bucket_maps())
    n_pat = bkt.shape[0]
    return pl.pallas_call(
        _bias_kernel,
        grid=(n_pat, ATT_HEADS),
        in_specs=[pl.BlockSpec(memory_space=pltpu.SMEM),
                  pl.BlockSpec((1, BLK, 2 * BLK), lambda p, h: (p, 0, 0))],
        out_specs=pl.BlockSpec((1, 1, N_BIAS_VARIANTS, BLK, 2 * BLK), lambda p, h: (p, h, 0, 0, 0)),
        out_shape=jax.ShapeDtypeStruct((n_pat, ATT_HEADS, N_BIAS_VARIANTS, BLK, 2 * BLK), F32),
        name="bias_tables",
    )(rel_bias, bkt)


def _dn_kernel(*refs, tile):
    nh = DN_HEADS
    n_stream = 3 * nh
    x_refs = refs[:n_stream]
    cw_ref, ba_ref, z_ref, alog_ref, dtb_ref, nw_ref, o_ref, state_s, xs = refs[n_stream:]
    n_chunk = SUPER // CHUNK
    heads = range(nh)
    lead = SUBLANES

    @pl.when(pl.program_id(1) == 0)
    def _():
        state_s[...] = jnp.zeros(state_s.shape, F32)
        xs[:, 0:lead, :] = jnp.zeros((n_stream, lead, LANES), F32)

    def copy_body(i, carry):
        r0 = pl.multiple_of(i * SUPER, SUPER)
        for j in range(n_stream):
            xs[j, pl.ds(r0 + lead, SUPER), :] = x_refs[j][0, pl.ds(r0, SUPER), :]
        return carry
    lax.fori_loop(0, tile // SUPER, copy_body, 0)

    ri = lax.broadcasted_iota(jnp.int32, (SUPER, SUPER), 0)
    ci = lax.broadcasted_iota(jnp.int32, (SUPER, SUPER), 1)
    same_chunk = (ri // CHUNK) == (ci // CHUNK)
    m_incl = same_chunk & (ri >= ci)
    m_strict = same_chunk & (ri > ci)
    m_sub = (ri // SUB) == (ci // SUB)
    eye = (ri == ci).astype(F32)
    chunk_cols = [(ci // CHUNK) == c for c in range(n_chunk)]
    cum_mat = (same_chunk & (ri <= ci)).astype(BF16)
    is_beta_row = lax.broadcasted_iota(jnp.int32, (SUBLANES, SUPER), 0) < nh

    a_coef = -jnp.exp(alog_ref[...])
    dt_b = dtb_ref[...]
    nw = nw_ref[...]
    cw = cw_ref[...]

    def conv_silu(j, r0):
        cwj = cw[:, j * LANES:(j + 1) * LANES]
        y = xs[j, pl.ds(r0 + lead, SUPER), :] * cwj[CONV_WIDTH - 1:CONV_WIDTH]
        for s in range(1, CONV_WIDTH):
            y = y + xs[j, pl.ds(r0 + lead - s, SUPER), :] * cwj[CONV_WIDTH - 1 - s:CONV_WIDTH - s]
        return y * _sigmoid(y)

    def l2n(x):
        return x * lax.rsqrt(jnp.sum(x * x, axis=-1, keepdims=True) + EPS)

    def lane_bcast(x, j):
        return jnp.broadcast_to(x[:, j:j + 1], x.shape)

    def row_bcast(x, i, n_rows):
        return jnp.broadcast_to(x[i:i + 1, :], (n_rows, x.shape[1]))

    def gate_rows(r0):
        pre = ba_ref[0, pl.ds(r0, SUPER), :].T[0:SUBLANES, :]
        xg = pre + dt_b
        g = a_coef * (jnp.maximum(xg, 0.0) + jnp.log(1.0 + jnp.exp(-jnp.abs(xg))))
        g_hi = g.astype(BF16)
        g_r = g - g_hi.astype(F32)
        g_mid = g_r.astype(BF16)
        g_lo = (g_r - g_mid.astype(F32)).astype(BF16)
        cs = jnp.dot(jnp.concatenate([g_hi, g_mid, g_lo], axis=0), cum_mat,
                     preferred_element_type=F32)
        gc_r = cs[0:SUBLANES] + cs[SUBLANES:2 * SUBLANES] + cs[2 * SUBLANES:3 * SUBLANES]
        gates = jnp.where(is_beta_row, _sigmoid(pre), gc_r)
        gates_c = jnp.concatenate([gates, jnp.zeros((SUPER - SUBLANES, SUPER), F32)], axis=0).T
        return gc_r, gates_c

    def body(it, carry):
        r0s = [pl.multiple_of((it * DN_UNROLL + u) * SUPER, SUPER) for u in range(DN_UNROLL)]
        items = [(u, h) for u in range(DN_UNROLL) for h in heads]
        idx = range(len(items))
        gate = [gate_rows(r0) for r0 in r0s]

        q = [l2n(conv_silu(h, r0s[u])) * (DN_HEAD_DIM ** -0.5) for u, h in items]
        k = [l2n(conv_silu(nh + h, r0s[u])) for u, h in items]
        v = [conv_silu(2 * nh + h, r0s[u]) for u, h in items]

        beta = [lane_bcast(gate[u][1], h) for u, h in items]
        gc = [lane_bcast(gate[u][1], nh + h) for u, h in items]
        decay = [jnp.exp(jnp.where(m_incl, gc[i] - row_bcast(gate[u][0], nh + h, SUPER), NEG))
                 for i, (u, h) in enumerate(items)]
        gc_last = [jnp.concatenate([row_bcast(gc[i], c * CHUNK + CHUNK - 1, CHUNK)
                                    for c in range(n_chunk)], axis=0) for i in idx]
        e_gc = [jnp.exp(gc[i]) for i in idx]
        kb = [k[i] * beta[i] for i in idx]

        kq = [_mm_nt(jnp.concatenate([kb[i], q[i]], axis=0), k[i]) for i in idx]
        a_mat = [jnp.where(m_strict, kq[i][:SUPER] * decay[i], 0.0) for i in idx]
        attn = [(kq[i][SUPER:] * decay[i]).astype(BF16) for i in idx]
        rhs = [jnp.concatenate([v[i] * beta[i], kb[i] * e_gc[i]], axis=1).astype(BF16) for i in idx]

        d_blk = [jnp.where(m_sub, a_mat[i], 0.0) for i in idx]
        l_b = [(a_mat[i] - d_blk[i]).astype(BF16) for i in idx]
        x1 = [-d_blk[i] for i in idx]
        x1b = [x1[i].astype(BF16) for i in idx]
        x2 = [_mm(x1b[i], x1b[i]) for i in idx]
        x2b = [x2[i].astype(BF16) for i in idx]
        x4 = [_mm(x2b[i], x2b[i]) for i in idx]
        p1 = [eye + x1[i] + x2[i] + _mm(x1b[i], x2b[i]) for i in idx]
        t_d = [p1[i] + _mm(p1[i], x4[i]) for i in idx]
        t_db = [t_d[i].astype(BF16) for i in idx]
        y1 = [-_mm(t_db[i], l_b[i]) for i in idx]
        td_rhs = [_mm(t_db[i], rhs[i]).astype(BF16) for i in idx]
        y1b = [y1[i].astype(BF16) for i in idx]
        y2 = [_mm(y1b[i], y1b[i]) for i in idx]
        y2b = [y2[i].astype(BF16) for i in idx]
        y4 = [_mm(y2b[i], y2b[i]) for i in idx]
        q1 = [eye + y1[i] + y2[i] + _mm(y1b[i], y2b[i]) for i in idx]
        q2 = [q1[i] + _mm(q1[i], y4[i]) for i in idx]

        uw = [_mm(q2[i], td_rhs[i]).astype(BF16) for i in idx]
        aw = [_mm(attn[i], uw[i]) for i in idx]
        q_t = [(q[i] * e_gc[i] - aw[i][:, LANES:]).astype(BF16) for i in idx]
        kt_t = [(k[i] * jnp.exp(gc_last[i] - gc[i])).T.astype(BF16) for i in idx]
        kw = [[_mm(jnp.where(chunk_cols[c], kt_t[i], jnp.zeros_like(kt_t[i])), uw[i])
               for c in range(n_chunk)] for i in idx]

        state = [state_s[h] for h in heads]
        for u in range(DN_UNROLL):
            outs = [[] for _ in heads]
            for c in range(n_chunk):
                sl = slice(c * CHUNK, (c + 1) * CHUNK)
                s_bf = [state[h].astype(BF16) for h in heads]
                for h in heads:
                    i = u * nh + h
                    outs[h].append(_mm(q_t[i][sl], s_bf[h]) + aw[i][sl, :LANES])
                for h in heads:
                    i = u * nh + h
                    g_last = jnp.exp(row_bcast(gc[i], c * CHUNK + CHUNK - 1, LANES))
                    state[h] = (state[h] * g_last - _mm(kw[i][c][:, LANES:], s_bf[h])
                                + kw[i][c][:, :LANES])
            rows = pl.ds(r0s[u], SUPER)
            for h in heads:
                o = jnp.concatenate(outs[h], axis=0)
                ms = jnp.mean(o * o, axis=-1, keepdims=True)
                z = z_ref[0, rows, h * LANES:(h + 1) * LANES].astype(F32)
                y = o * lax.rsqrt(ms + EPS) * nw * (z * _sigmoid(z))
                o_ref[0, rows, h * LANES:(h + 1) * LANES] = y.astype(o_ref.dtype)
        for h in heads:
            state_s[h] = state[h]
        return carry

    lax.fori_loop(0, tile // (SUPER * DN_UNROLL), body, 0)
    xs[:, 0:lead, :] = xs[:, tile:tile + lead, :]


def _deltanet(dn_qkv, dn_z, ba, conv_w, a_log, dt_bias, dn_norm_w, tile=1024):
    bsz, seq, _ = dn_qkv.shape
    n_stream = 3 * DN_HEADS
    seq_tile = lambda b, t: (b, t, 0)
    fixed = lambda b, t: (0, 0)

    def slab(j):
        return pl.BlockSpec((1, tile, LANES), lambda b, t: (b, t, j))

    return pl.pallas_call(
        functools.partial(_dn_kernel, tile=tile),
        grid=(bsz, seq // tile),
        in_specs=[slab(j) for j in range(n_stream)]
        + [pl.BlockSpec((CONV_WIDTH, 3 * D_DN), fixed),
           pl.BlockSpec((1, tile, LANES), seq_tile),
           pl.BlockSpec((1, tile, D_DN), seq_tile),
           pl.BlockSpec((SUBLANES, LANES), fixed),
           pl.BlockSpec((SUBLANES, LANES), fixed),
           pl.BlockSpec((1, LANES), fixed)],
        out_specs=pl.BlockSpec((1, tile, D_DN), seq_tile),
        out_shape=jax.ShapeDtypeStruct((bsz, seq, D_DN), BF16),
        scratch_shapes=[pltpu.VMEM((DN_HEADS, DN_HEAD_DIM, DN_HEAD_DIM), F32),
                        pltpu.VMEM((n_stream, tile + SUBLANES, LANES), F32)],
        compiler_params=pltpu.CompilerParams(
            dimension_semantics=("arbitrary", "arbitrary"), vmem_limit_bytes=VMEM_LIMIT),
        name="deltanet",
    )(*([dn_qkv] * n_stream), conv_w, ba, dn_z, a_log, dt_bias, dn_norm_w)


def _att_kernel(*refs, seq):
    n_pat = len(DILATED_PATTERNS)
    qkv_refs = [refs[3 * p:3 * p + 3] for p in range(n_pat)]
    g_ref, bias_ref, o_ref, m_s, l_s, acc_s = refs[3 * n_pat:]
    tile = 512
    lane = lax.broadcasted_iota(jnp.int32, (BLK, LANES), 1)
    head0 = lane < ATT_HEAD_DIM
    ones_v = jnp.ones((2 * BLK, LANES), BF16)
    zero_q = jnp.zeros((BLK, LANES), BF16)

    def group(p, r, n_blk, bi0):
        q_ref, k_ref, v_ref = qkv_refs[p]
        idx = []
        for g in range(ATT_GROUP):
            bi = bi0 + g
            base = pl.multiple_of(bi * BLK, BLK)
            kbase = pl.multiple_of(jnp.maximum(base - BLK, 0), BLK)
            n = bi % n_blk
            variant = jnp.where(bi == 0, 2, jnp.where(n == 0, 1, 0))
            idx.append((base, kbase, variant, bi // n_blk + n * (BLK * r)))
        scores = []
        for base, kbase, variant, tok0 in idx:
            q = q_ref[0, pl.ds(base, BLK), :]
            q2 = jnp.concatenate([jnp.where(head0, q, zero_q), jnp.where(head0, zero_q, q)], axis=0)
            k = k_ref[0, pl.ds(kbase, 2 * BLK), :]
            bias = jnp.concatenate([bias_ref[p, 0, variant], bias_ref[p, 1, variant]], axis=0)
            scores.append(lax.dot_general(q2, k, (((1,), (1,)), ((), ())),
                                          preferred_element_type=F32) + bias)
        maxes, probs = [], []
        for s in scores:
            m = jnp.max(s, axis=-1, keepdims=True)
            maxes.append(m)
            probs.append(jnp.exp2(s - m).astype(BF16))
        for (base, kbase, variant, tok0), m, e in zip(idx, maxes, probs):
            v2 = jnp.concatenate([v_ref[0, pl.ds(kbase, 2 * BLK), :], ones_v], axis=1)
            pv = jnp.dot(e, v2, preferred_element_type=F32)
            rows = pl.ds(tok0, BLK) if r == 1 else pl.ds(tok0, BLK, stride=r)
            m_s[p, rows, :] = jnp.where(head0, m[:BLK], m[BLK:])
            l_s[p, rows, :] = jnp.where(head0, pv[:BLK, LANES:], pv[BLK:, LANES:])
            acc_s[p, rows, :] = jnp.where(head0, pv[:BLK, :LANES], pv[BLK:, :LANES])

    for p, (window, r) in enumerate(DILATED_PATTERNS):
        n_blk = seq // (r * BLK)

        def group_body(i, carry, p=p, r=r, n_blk=n_blk):
            group(p, r, n_blk, i * ATT_GROUP)
            return carry
        lax.fori_loop(0, seq // (BLK * ATT_GROUP), group_body, 0)

    def out_body(i, carry):
        rows = pl.ds(pl.multiple_of(i * tile, tile), tile)
        m1, m2, m3 = m_s[0, rows, :], m_s[1, rows, :], m_s[2, rows, :]
        m = jnp.maximum(jnp.maximum(m1, m2), m3)
        w1, w2, w3 = jnp.exp2(m1 - m), jnp.exp2(m2 - m), jnp.exp2(m3 - m)
        num = w1 * acc_s[0, rows, :] + w2 * acc_s[1, rows, :] + w3 * acc_s[2, rows, :]
        den = w1 * l_s[0, rows, :] + w2 * l_s[1, rows, :] + w3 * l_s[2, rows, :]
        g = g_ref[0, rows, :].astype(F32)
        o_ref[0, rows, :] = (num / den * (g * _sigmoid(g))).astype(o_ref.dtype)
        return carry
    lax.fori_loop(0, seq // tile, out_body, 0)


def _dilated_attn(att_perm, att_g, bias):
    bsz, seq, _ = att_g.shape
    n_pair = ATT_HEADS // 2
    n_pat = bias.shape[0]

    def col(off):
        return pl.BlockSpec((1, seq, LANES), lambda b, j: (b, 0, off + j))

    qkv_specs, qkv_args = [], []
    for a in att_perm:
        qkv_specs += [col(0), col(n_pair), col(2 * n_pair)]
        qkv_args += [a, a, a]
    return pl.pallas_call(
        functools.partial(_att_kernel, seq=seq),
        grid=(bsz, n_pair),
        in_specs=qkv_specs
        + [col(0),
           pl.BlockSpec((n_pat, 2, N_BIAS_VARIANTS, BLK, 2 * BLK), lambda b, j: (0, j, 0, 0, 0))],
        out_specs=col(0),
        out_shape=jax.ShapeDtypeStruct((bsz, seq, D_ATT), BF16),
        scratch_shapes=[pltpu.VMEM((n_pat, seq, LANES), F32),
                        pltpu.VMEM((n_pat, seq, LANES), F32),
                        pltpu.VMEM((n_pat, seq, LANES), F32)],
        compiler_params=pltpu.CompilerParams(
            dimension_semantics=("arbitrary", "arbitrary"), vmem_limit_bytes=VMEM_LIMIT),
        name="dilated_attn",
    )(*qkv_args, att_g, bias)


def _out_proj_kernel(x_ref, ydn_ref, yatt_ref, w_ref, o_ref):
    o_ref[...] = (x_ref[...]
                  + jnp.dot(ydn_ref[...], w_ref[0:D_DN, :], preferred_element_type=F32)
                  + jnp.dot(yatt_ref[...], w_ref[D_DN:D_MIX, :], preferred_element_type=F32))


def _out_proj(x2d, y_dn, y_att, w_out, tm=512):
    n = x2d.shape[0]
    row = lambda i: (i, 0)
    return pl.pallas_call(
        _out_proj_kernel,
        grid=(n // tm,),
        in_specs=[pl.BlockSpec((tm, D_MODEL), row),
                  pl.BlockSpec((tm, D_DN), row),
                  pl.BlockSpec((tm, D_ATT), row),
                  pl.BlockSpec((D_MIX, D_MODEL), lambda i: (0, 0))],
        out_specs=pl.BlockSpec((tm, D_MODEL), row),
        out_shape=jax.ShapeDtypeStruct((n, D_MODEL), F32),
        compiler_params=pltpu.CompilerParams(
            dimension_semantics=("arbitrary",), vmem_limit_bytes=VMEM_LIMIT),
        name="out_proj",
    )(x2d, y_dn, y_att, w_out)


def _gate_rows(v):
    col = jnp.pad(v.reshape(-1, 1).astype(F32), ((DN_HEADS, SUBLANES - 2 * DN_HEADS), (0, 0)))
    return jnp.broadcast_to(col, (SUBLANES, LANES))


def _layer(x, norm_w, w_in, conv_w, a_log, dt_bias, dn_norm_w, q_norm_w, k_norm_w, bias, w_out):
    bsz, seq, _ = x.shape
    x2d = x.reshape(bsz * seq, D_MODEL)
    c_dn = 4 * D_DN
    c_ba = c_dn + 2 * DN_HEADS
    w_ba = jnp.pad(w_in[:, c_dn:c_ba], ((0, 0), (0, LANES - 2 * DN_HEADS)))
    w_all = jnp.concatenate([w_in[:, :c_dn], w_in[:, c_ba:], w_ba], axis=1).astype(BF16)

    qw = jnp.tile(q_norm_w.reshape(1, ATT_HEAD_DIM).astype(F32), (1, 2)) * (ATT_HEAD_DIM ** -0.5 * LOG2E)
    kw = jnp.tile(k_norm_w.reshape(1, ATT_HEAD_DIM).astype(F32), (1, 2))
    dn_qkv, dn_z, ba, att_g, *att_perm = _in_proj(x, norm_w.reshape(1, D_MODEL), w_all, qw, kw)
    att_perm = [a.reshape(bsz, seq, 3 * D_ATT) for a in att_perm]

    y_dn = _deltanet(dn_qkv, dn_z, ba, conv_w, _gate_rows(a_log), _gate_rows(dt_bias),
                     dn_norm_w.reshape(1, DN_HEAD_DIM))
    y_att = _dilated_attn(att_perm, att_g, bias)
    out = _out_proj(x2d, y_dn.reshape(bsz * seq, D_DN), y_att.reshape(bsz * seq, D_ATT),
                    w_out.astype(BF16))
    return out.reshape(bsz, seq, D_MODEL)


def kernel(x, norm_w, w_in, conv_w, a_log, dt_bias, dn_norm_w, q_norm_w, k_norm_w, rel_bias, w_out):
    bias = _bias_tables(rel_bias.astype(F32))
    for layer in range(norm_w.shape[0]):
        x = _layer(x, norm_w[layer], w_in[layer], conv_w[layer], a_log[layer], dt_bias[layer],
                   dn_norm_w[layer], q_norm_w[layer], k_norm_w[layer], bias, w_out[layer])
    return x
```

```python
import functools
import math

import numpy as np
import jax
import jax.numpy as jnp
from jax import lax
from jax.experimental import pallas as pl
from jax.experimental.pallas import tpu as pltpu

D_MODEL = 1024
D_DN = 512
DN_HEADS = 4
DN_HEAD_DIM = 128
CONV_WIDTH = 4
CHUNK = 64
D_ATT = 512
ATT_HEADS = 8
ATT_HEAD_DIM = 64
DILATED_PATTERNS = ((128, 1), (512, 4), (2048, 16))
N_BUCKETS = 32
MAX_DISTANCE = 2048
D_MIX = D_DN + D_ATT
EPS = 1e-6

LANES = 128
SUBLANES = 8
BLK = 128
ATT_GROUP = 8
SUPER = 128
DN_UNROLL = 4
SUB = 8
NEG = -1e30
LOG2E = math.log2(math.e)
N_BIAS_VARIANTS = 3
VMEM_LIMIT = 56 * 1024 * 1024

F32 = jnp.float32
BF16 = jnp.bfloat16


def _mm(a, b):
    return jnp.dot(a.astype(BF16), b.astype(BF16), preferred_element_type=F32)


def _mm_nt(a, b):
    return lax.dot_general(a.astype(BF16), b.astype(BF16), (((1,), (1,)), ((), ())),
                           preferred_element_type=F32)


def _sigmoid(x):
    return 0.5 * jnp.tanh(0.5 * x) + 0.5


def _in_proj_kernel(x_ref, nw_ref, w_ref, qw_ref, kw_ref,
                    dnqkv_ref, dnz_ref, ba_ref, attg_ref, *att_refs_and_scratch):
    att_refs = att_refs_and_scratch[:len(DILATED_PATTERNS)]
    att_s, perm_s = att_refs_and_scratch[len(DILATED_PATTERNS):]
    tm = x_ref.shape[1]
    x = x_ref[0]
    ms = jnp.mean(x * x, axis=-1, keepdims=True)
    h = (x * lax.rsqrt(ms + EPS) * nw_ref[...]).astype(BF16)
    c0 = 3 * D_DN
    c1 = c0 + D_DN
    c2 = c1 + 3 * D_ATT
    c3 = c2 + D_ATT

    att = jnp.dot(h, w_ref[:, c1:c2], preferred_element_type=F32)
    head0 = lax.broadcasted_iota(jnp.int32, (tm, LANES), 1) < ATT_HEAD_DIM
    n_slab = 3 * D_ATT // LANES
    for j in range(n_slab):
        xs = att[:, j * LANES:(j + 1) * LANES]
        if j < 2 * D_ATT // LANES:
            is_q = j < D_ATT // LANES
            x2 = xs * xs
            s0 = jnp.sum(jnp.where(head0, x2, 0.0), axis=-1, keepdims=True)
            s1 = jnp.sum(jnp.where(head0, 0.0, x2), axis=-1, keepdims=True)
            ss = jnp.where(head0, s0, s1)
            wn = qw_ref[...] if is_q else kw_ref[...]
            xs = xs * lax.rsqrt(ss * (1.0 / ATT_HEAD_DIM) + EPS) * wn
        att_s[j] = xs
    for j in range(n_slab):
        cols = slice(j * LANES, (j + 1) * LANES)
        src_ref, r_prev = att_s.at[j], 1
        for level, (o_ref, (_, r)) in enumerate(zip(att_refs, DILATED_PATTERNS)):
            step = r // r_prev
            rows_per = tm // r
            keep = level + 1 < len(DILATED_PATTERNS) and r > 1
            for res in range(r):
                if r == 1:
                    piece = src_ref[...]
                else:
                    start = (res % r_prev) * (tm // r_prev) + res // r_prev
                    piece = src_ref[pl.ds(start, rows_per, stride=step), :]
                o_ref[0, res, :, cols] = piece.astype(BF16)
                if keep:
                    perm_s[j, pl.ds(res * rows_per, rows_per), :] = piece
            if keep:
                src_ref, r_prev = perm_s.at[j], r

    dnqkv_ref[0] = jnp.dot(h, w_ref[:, 0:c0], preferred_element_type=F32)
    dnz_ref[0] = jnp.dot(h, w_ref[:, c0:c1], preferred_element_type=F32).astype(BF16)
    attg_ref[0] = jnp.dot(h, w_ref[:, c2:c3], preferred_element_type=F32).astype(BF16)
    ba_ref[0] = jnp.dot(h, w_ref[:, c3:c3 + LANES], preferred_element_type=F32)


def _in_proj(x, norm_w, w_all, qw, kw, tm=512):
    bsz, seq, _ = x.shape
    ncol = w_all.shape[1]
    row = lambda b, i: (b, i, 0)
    fixed = lambda b, i: (0, 0)
    att_specs = [pl.BlockSpec((1, r, tm // r, 3 * D_ATT), lambda b, i: (b, 0, i, 0))
                 for _, r in DILATED_PATTERNS]
    att_shapes = [jax.ShapeDtypeStruct((bsz, r, seq // r, 3 * D_ATT), BF16)
                  for _, r in DILATED_PATTERNS]
    return pl.pallas_call(
        _in_proj_kernel,
        grid=(bsz, seq // tm),
        in_specs=[pl.BlockSpec((1, tm, D_MODEL), row),
                  pl.BlockSpec((1, D_MODEL), fixed),
                  pl.BlockSpec((D_MODEL, ncol), fixed),
                  pl.BlockSpec((1, LANES), fixed),
                  pl.BlockSpec((1, LANES), fixed)],
        out_specs=[pl.BlockSpec((1, tm, 3 * D_DN), row),
                   pl.BlockSpec((1, tm, D_DN), row),
                   pl.BlockSpec((1, tm, LANES), row),
                   pl.BlockSpec((1, tm, D_ATT), row)] + att_specs,
        out_shape=[jax.ShapeDtypeStruct((bsz, seq, 3 * D_DN), F32),
                   jax.ShapeDtypeStruct((bsz, seq, D_DN), BF16),
                   jax.ShapeDtypeStruct((bsz, seq, LANES), F32),
                   jax.ShapeDtypeStruct((bsz, seq, D_ATT), BF16)] + att_shapes,
        scratch_shapes=[pltpu.VMEM((3 * D_ATT // LANES, tm, LANES), F32),
                        pltpu.VMEM((3 * D_ATT // LANES, tm, LANES), F32)],
        compiler_params=pltpu.CompilerParams(
            dimension_semantics=("arbitrary", "arbitrary"), vmem_limit_bytes=VMEM_LIMIT),
        name="in_proj",
    )(x, norm_w, w_all, qw, kw)


def _t5_bucket(dist):
    max_exact = N_BUCKETS // 2
    d = np.maximum(dist, 1).astype(np.float64)
    large = max_exact + (np.log(d / max_exact) / math.log(MAX_DISTANCE / max_exact)
                         * (N_BUCKETS - max_exact)).astype(np.int32)
    large = np.minimum(large, N_BUCKETS - 1)
    return np.where(dist < max_exact, dist, large).astype(np.int32)


def _bucket_maps():
    maps = []
    for window, r in DILATED_PATTERNS:
        w_steps = window // r
        assert w_steps == BLK
        qi = np.arange(BLK)[:, None]
        kj = np.arange(2 * BLK)[None, :]
        step = qi - kj + BLK
        band = (step >= 0) & (step <= w_steps)
        buckets = _t5_bucket(np.clip(step, 0, None) * r)
        maps.append(np.where(band, buckets, -1).astype(np.int32))
    return np.stack(maps)


def _bias_kernel(rel_ref, bkt_ref, o_ref):
    h = pl.program_id(1)
    bkt = bkt_ref[0]
    acc = jnp.full(bkt.shape, NEG, F32)
    for b in range(N_BUCKETS):
        acc = jnp.where(bkt == b, rel_ref[h, b] * LOG2E, acc)
    o_ref[0, 0, 0] = acc
    col = lax.broadcasted_iota(jnp.int32, bkt.shape, 1)
    o_ref[0, 0, 1] = jnp.where(col < BLK, NEG, acc)
    o_ref[0, 0, 2] = jnp.concatenate([acc[:, BLK:], jnp.full((BLK, BLK), NEG, F32)], axis=1)


def _bias_tables(rel_bias):
    bkt = jnp.asarray(_bucket_maps())
    n_pat = bkt.shape[0]
    return pl.pallas_call(
        _bias_kernel,
        grid=(n_pat, ATT_HEADS),
        in_specs=[pl.BlockSpec(memory_space=pltpu.SMEM),
                  pl.BlockSpec((1, BLK, 2 * BLK), lambda p, h: (p, 0, 0))],
        out_specs=pl.BlockSpec((1, 1, N_BIAS_VARIANTS, BLK, 2 * BLK), lambda p, h: (p, h, 0, 0, 0)),
        out_shape=jax.ShapeDtypeStruct((n_pat, ATT_HEADS, N_BIAS_VARIANTS, BLK, 2 * BLK), F32),
        name="bias_tables",
    )(rel_bias, bkt)


def _dn_kernel(*refs, tile):
    nh = DN_HEADS
    n_stream = 3 * nh
    x_refs = refs[:n_stream]
    cw_ref, ba_ref, z_ref, alog_ref, dtb_ref, nw_ref, o_ref, state_s, xs = refs[n_stream:]
    n_chunk = SUPER // CHUNK
    heads = range(nh)
    lead = SUBLANES

    @pl.when(pl.program_id(1) == 0)
    def _():
        state_s[...] = jnp.zeros(state_s.shape, F32)
        xs[:, 0:lead, :] = jnp.zeros((n_stream, lead, LANES), F32)

    def copy_body(i, carry):
        r0 = pl.multiple_of(i * SUPER, SUPER)
        for j in range(n_stream):
            xs[j, pl.ds(r0 + lead, SUPER), :] = x_refs[j][0, pl.ds(r0, SUPER), :]
        return carry
    lax.fori_loop(0, tile // SUPER, copy_body, 0)

    ri = lax.broadcasted_iota(jnp.int32, (SUPER, SUPER), 0)
    ci = lax.broadcasted_iota(jnp.int32, (SUPER, SUPER), 1)
    same_chunk = (ri // CHUNK) == (ci // CHUNK)
    m_incl = same_chunk & (ri >= ci)
    m_strict = same_chunk & (ri > ci)
    m_sub = (ri // SUB) == (ci // SUB)
    eye = (ri == ci).astype(F32)
    chunk_cols = [(ci // CHUNK) == c for c in range(n_chunk)]
    cum_mat = (same_chunk & (ri <= ci)).astype(BF16)
    is_beta_row = lax.broadcasted_iota(jnp.int32, (SUBLANES, SUPER), 0) < nh

    a_coef = -jnp.exp(alog_ref[...])
    dt_b = dtb_ref[...]
    nw = nw_ref[...]
    cw = cw_ref[...]

    def conv_silu(j, r0):
        cwj = cw[:, j * LANES:(j + 1) * LANES]
        y = xs[j, pl.ds(r0 + lead, SUPER), :] * cwj[CONV_WIDTH - 1:CONV_WIDTH]
        for s in range(1, CONV_WIDTH):
            y = y + xs[j, pl.ds(r0 + lead - s, SUPER), :] * cwj[CONV_WIDTH - 1 - s:CONV_WIDTH - s]
        return y * _sigmoid(y)

    def l2n(x):
        return x * lax.rsqrt(jnp.sum(x * x, axis=-1, keepdims=True) + EPS)

    def lane_bcast(x, j):
        return jnp.broadcast_to(x[:, j:j + 1], x.shape)

    def row_bcast(x, i, n_rows):
        return jnp.broadcast_to(x[i:i + 1, :], (n_rows, x.shape[1]))

    def gate_rows(r0):
        pre = ba_ref[0, pl.ds(r0, SUPER), :].T[0:SUBLANES, :]
        xg = pre + dt_b
        g = a_coef * (jnp.maximum(xg, 0.0) + jnp.log(1.0 + jnp.exp(-jnp.abs(xg))))
        g_hi = g.astype(BF16)
        g_r = g - g_hi.astype(F32)
        g_mid = g_r.astype(BF16)
        g_lo = (g_r - g_mid.astype(F32)).astype(BF16)
        cs = jnp.dot(jnp.concatenate([g_hi, g_mid, g_lo], axis=0), cum_mat,
                     preferred_element_type=F32)
        gc_r = cs[0:SUBLANES] + cs[SUBLANES:2 * SUBLANES] + cs[2 * SUBLANES:3 * SUBLANES]
        gates = jnp.where(is_beta_row, _sigmoid(pre), gc_r)
        gates_c = jnp.concatenate([gates, jnp.zeros((SUPER - SUBLANES, SUPER), F32)], axis=0).T
        return gc_r, gates_c

    def body(it, carry):
        r0s = [pl.multiple_of((it * DN_UNROLL + u) * SUPER, SUPER) for u in range(DN_UNROLL)]
        items = [(u, h) for u in range(DN_UNROLL) for h in heads]
        idx = range(len(items))
        gate = [gate_rows(r0) for r0 in r0s]

        q = [l2n(conv_silu(h, r0s[u])) * (DN_HEAD_DIM ** -0.5) for u, h in items]
        k = [l2n(conv_silu(nh + h, r0s[u])) for u, h in items]
        v = [conv_silu(2 * nh + h, r0s[u]) for u, h in items]

        beta = [lane_bcast(gate[u][1], h) for u, h in items]
        gc = [lane_bcast(gate[u][1], nh + h) for u, h in items]
        decay = [jnp.exp(jnp.where(m_incl, gc[i] - row_bcast(gate[u][0], nh + h, SUPER), NEG))
                 for i, (u, h) in enumerate(items)]
        gc_last = [jnp.concatenate([row_bcast(gc[i], c * CHUNK + CHUNK - 1, CHUNK)
                                    for c in range(n_chunk)], axis=0) for i in idx]
        e_gc = [jnp.exp(gc[i]) for i in idx]
        kb = [k[i] * beta[i] for i in idx]

        kq = [_mm_nt(jnp.concatenate([kb[i], q[i]], axis=0), k[i]) for i in idx]
        a_mat = [jnp.where(m_strict, kq[i][:SUPER] * decay[i], 0.0) for i in idx]
        attn = [(kq[i][SUPER:] * decay[i]).astype(BF16) for i in idx]
        rhs = [jnp.concatenate([v[i] * beta[i], kb[i] * e_gc[i]], axis=1).astype(BF16) for i in idx]

        d_blk = [jnp.where(m_sub, a_mat[i], 0.0) for i in idx]
        l_b = [(a_mat[i] - d_blk[i]).astype(BF16) for i in idx]
        x1 = [-d_blk[i] for i in idx]
        x1b = [x1[i].astype(BF16) for i in idx]
        x2 = [_mm(x1b[i], x1b[i]) for i in idx]
        x2b = [x2[i].astype(BF16) for i in idx]
        x4 = [_mm(x2b[i], x2b[i]) for i in idx]
        p1 = [eye + x1[i] + x2[i] + _mm(x1b[i], x2b[i]) for i in idx]
        t_d = [p1[i] + _mm(p1[i], x4[i]) for i in idx]
        t_db = [t_d[i].astype(BF16) for i in idx]
        y1 = [-_mm(t_db[i], l_b[i]) for i in idx]
        td_rhs = [_mm(t_db[i], rhs[i]).astype(BF16) for i in idx]
        y1b = [y1[i].astype(BF16) for i in idx]
        y2 = [_mm(y1b[i], y1b[i]) for i in idx]
        y2b = [y2[i].astype(BF16) for i in idx]
        y4 = [_mm(y2b[i], y2b[i]) for i in idx]
        q1 = [eye + y1[i] + y2[i] + _mm(y1b[i], y2b[i]) for i in idx]
        q2 = [q1[i] + _mm(q1[i], y4[i]) for i in idx]

        uw = [_mm(q2[i], td_rhs[i]).astype(BF16) for i in idx]
        aw = [_mm(attn[i], uw[i]) for i in idx]
        q_t = [(q[i] * e_gc[i] - aw[i][:, LANES:]).astype(BF16) for i in idx]
        kt_t = [(k[i] * jnp.exp(gc_last[i] - gc[i])).T.astype(BF16) for i in idx]
        kw = [[_mm(jnp.where(chunk_cols[c], kt_t[i], jnp.zeros_like(kt_t[i])), uw[i])
               for c in range(n_chunk)] for i in idx]

        state = [state_s[h] for h in heads]
        for u in range(DN_UNROLL):
            outs = [[] for _ in heads]
            for c in range(n_chunk):
                sl = slice(c * CHUNK, (c + 1) * CHUNK)
                s_bf = [state[h].astype(BF16) for h in heads]
                for h in heads:
                    i = u * nh + h
                    outs[h].append(_mm(q_t[i][sl], s_bf[h]) + aw[i][sl, :LANES])
                for h in heads:
                    i = u * nh + h
                    g_last = jnp.exp(row_bcast(gc[i], c * CHUNK + CHUNK - 1, LANES))
                    state[h] = (state[h] * g_last - _mm(kw[i][c][:, LANES:], s_bf[h])
                                + kw[i][c][:, :LANES])
            rows = pl.ds(r0s[u], SUPER)
            for h in heads:
                o = jnp.concatenate(outs[h], axis=0)
                ms = jnp.mean(o * o, axis=-1, keepdims=True)
                z = z_ref[0, rows, h * LANES:(h + 1) * LANES].astype(F32)
                y = o * lax.rsqrt(ms + EPS) * nw * (z * _sigmoid(z))
                o_ref[0, rows, h * LANES:(h + 1) * LANES] = y.astype(o_ref.dtype)
        for h in heads:
            state_s[h] = state[h]
        return carry

    lax.fori_loop(0, tile // (SUPER * DN_UNROLL), body, 0)
    xs[:, 0:lead, :] = xs[:, tile:tile + lead, :]


def _deltanet(dn_qkv, dn_z, ba, conv_w, a_log, dt_bias, dn_norm_w, tile=1024):
    bsz, seq, _ = dn_qkv.shape
    n_stream = 3 * DN_HEADS
    seq_tile = lambda b, t: (b, t, 0)
    fixed = lambda b, t: (0, 0)

    def slab(j):
        return pl.BlockSpec((1, tile, LANES), lambda b, t: (b, t, j))

    return pl.pallas_call(
        functools.partial(_dn_kernel, tile=tile),
        grid=(bsz, seq // tile),
        in_specs=[slab(j) for j in range(n_stream)]
        + [pl.BlockSpec((CONV_WIDTH, 3 * D_DN), fixed),
           pl.BlockSpec((1, tile, LANES), seq_tile),
           pl.BlockSpec((1, tile, D_DN), seq_tile),
           pl.BlockSpec((SUBLANES, LANES), fixed),
           pl.BlockSpec((SUBLANES, LANES), fixed),
           pl.BlockSpec((1, LANES), fixed)],
        out_specs=pl.BlockSpec((1, tile, D_DN), seq_tile),
        out_shape=jax.ShapeDtypeStruct((bsz, seq, D_DN), BF16),
        scratch_shapes=[pltpu.VMEM((DN_HEADS, DN_HEAD_DIM, DN_HEAD_DIM), F32),
                        pltpu.VMEM((n_stream, tile + SUBLANES, LANES), F32)],
        compiler_params=pltpu.CompilerParams(
            dimension_semantics=("arbitrary", "arbitrary"), vmem_limit_bytes=VMEM_LIMIT),
        name="deltanet",
    )(*([dn_qkv] * n_stream), conv_w, ba, dn_z, a_log, dt_bias, dn_norm_w)


def _part_pitch(seq, r):
    return seq // r + 1 if r >= 2 * SUBLANES else None


def _part_rows(seq, r):
    pitch = _part_pitch(seq, r)
    return seq if pitch is None else -(-(r * pitch) // SUBLANES) * SUBLANES


def _att_kernel(*refs, seq):
    n_pat = len(DILATED_PATTERNS)
    qkv_refs = [refs[3 * p:3 * p + 3] for p in range(n_pat)]
    g_ref, bias_ref, o_ref = refs[3 * n_pat:3 * n_pat + 3]
    part_s = refs[3 * n_pat + 3:]
    tile = 512
    lane = lax.broadcasted_iota(jnp.int32, (BLK, LANES), 1)
    head0 = lane < ATT_HEAD_DIM
    ones_v = jnp.ones((2 * BLK, LANES), BF16)
    zero_q = jnp.zeros((BLK, LANES), BF16)

    def store_part(p, r, res, n, vals):
        pitch = _part_pitch(seq, r)
        if pitch is None:
            tok0 = res + n * (BLK * r)
            rows = pl.ds(tok0, BLK) if r == 1 else pl.ds(tok0, BLK, stride=r)
        else:
            rows = pl.ds(res * pitch + n * BLK, BLK)
        for a, val in enumerate(vals):
            part_s[p][a, rows, :] = val

    def load_part(p, r, a, t0):
        pitch = _part_pitch(seq, r)
        if pitch is None:
            return part_s[p][a, pl.ds(t0, tile), :]
        m0 = t0 // r
        return jnp.concatenate([part_s[p][a, pl.ds(m0 + jj, r, stride=pitch), :]
                                for jj in range(tile // r)], axis=0)

    def group(p, r, n_blk, bi0):
        q_ref, k_ref, v_ref = qkv_refs[p]
        idx = []
        for g in range(ATT_GROUP):
            bi = bi0 + g
            base = pl.multiple_of(bi * BLK, BLK)
            kbase = pl.multiple_of(jnp.maximum(base - BLK, 0), BLK)
            n = bi % n_blk
            variant = jnp.where(bi == 0, 2, jnp.where(n == 0, 1, 0))
            idx.append((base, kbase, variant, (bi // n_blk, n)))
        scores = []
        for base, kbase, variant, _ in idx:
            q = q_ref[0, pl.ds(base, BLK), :]
            q2 = jnp.concatenate([jnp.where(head0, q, zero_q), jnp.where(head0, zero_q, q)], axis=0)
            k = k_ref[0, pl.ds(kbase, 2 * BLK), :]
            bias = jnp.concatenate([bias_ref[p, 0, variant], bias_ref[p, 1, variant]], axis=0)
            scores.append(lax.dot_general(q2, k, (((1,), (1,)), ((), ())),
                                          preferred_element_type=F32) + bias)
        maxes, probs = [], []
        for s in scores:
            m = jnp.max(s, axis=-1, keepdims=True)
            maxes.append(m)
            probs.append(jnp.exp2(s - m).astype(BF16))
        for (base, kbase, variant, (res, n)), m, e in zip(idx, maxes, probs):
            v2 = jnp.concatenate([v_ref[0, pl.ds(kbase, 2 * BLK), :], ones_v], axis=1)
            pv = jnp.dot(e, v2, preferred_element_type=F32)
            store_part(p, r, res, n,
                       (jnp.where(head0, m[:BLK], m[BLK:]),
                        jnp.where(head0, pv[:BLK, LANES:], pv[BLK:, LANES:]),
                        jnp.where(head0, pv[:BLK, :LANES], pv[BLK:, :LANES])))

    for p, (window, r) in enumerate(DILATED_PATTERNS):
        n_blk = seq // (r * BLK)

        def group_body(i, carry, p=p, r=r, n_blk=n_blk):
            group(p, r, n_blk, i * ATT_GROUP)
            return carry
        lax.fori_loop(0, seq // (BLK * ATT_GROUP), group_body, 0)

    def out_body(i, carry):
        t0 = pl.multiple_of(i * tile, tile)
        rows = pl.ds(t0, tile)
        part = [[load_part(p, r, a, t0) for a in range(3)]
                for p, (_, r) in enumerate(DILATED_PATTERNS)]
        m = functools.reduce(jnp.maximum, [pt[0] for pt in part])
        w = [jnp.exp2(pt[0] - m) for pt in part]
        den = sum(wp * pt[1] for wp, pt in zip(w, part))
        num = sum(wp * pt[2] for wp, pt in zip(w, part))
        g = g_ref[0, rows, :].astype(F32)
        o_ref[0, rows, :] = (num / den * (g * _sigmoid(g))).astype(o_ref.dtype)
        return carry
    lax.fori_loop(0, seq // tile, out_body, 0)


def _dilated_attn(att_perm, att_g, bias):
    bsz, seq, _ = att_g.shape
    n_pair = ATT_HEADS // 2
    n_pat = bias.shape[0]

    def col(off):
        return pl.BlockSpec((1, seq, LANES), lambda b, j: (b, 0, off + j))

    qkv_specs, qkv_args = [], []
    for a in att_perm:
        qkv_specs += [col(0), col(n_pair), col(2 * n_pair)]
        qkv_args += [a, a, a]
    return pl.pallas_call(
        functools.partial(_att_kernel, seq=seq),
        grid=(bsz, n_pair),
        in_specs=qkv_specs
        + [col(0),
           pl.BlockSpec((n_pat, 2, N_BIAS_VARIANTS, BLK, 2 * BLK), lambda b, j: (0, j, 0, 0, 0))],
        out_specs=col(0),
        out_shape=jax.ShapeDtypeStruct((bsz, seq, D_ATT), BF16),
        scratch_shapes=[pltpu.VMEM((3, _part_rows(seq, r), LANES), F32)
                        for _, r in DILATED_PATTERNS],
        compiler_params=pltpu.CompilerParams(
            dimension_semantics=("arbitrary", "arbitrary"), vmem_limit_bytes=VMEM_LIMIT),
        name="dilated_attn",
    )(*qkv_args, att_g, bias)


def _out_proj_kernel(x_ref, ydn_ref, yatt_ref, w_ref, o_ref):
    o_ref[...] = (x_ref[...]
                  + jnp.dot(ydn_ref[...], w_ref[0:D_DN, :], preferred_element_type=F32)
                  + jnp.dot(yatt_ref[...], w_ref[D_DN:D_MIX, :], preferred_element_type=F32))


def _out_proj(x2d, y_dn, y_att, w_out, tm=512):
    n = x2d.shape[0]
    row = lambda i: (i, 0)
    return pl.pallas_call(
        _out_proj_kernel,
        grid=(n // tm,),
        in_specs=[pl.BlockSpec((tm, D_MODEL), row),
                  pl.BlockSpec((tm, D_DN), row),
                  pl.BlockSpec((tm, D_ATT), row),
                  pl.BlockSpec((D_MIX, D_MODEL), lambda i: (0, 0))],
        out_specs=pl.BlockSpec((tm, D_MODEL), row),
        out_shape=jax.ShapeDtypeStruct((n, D_MODEL), F32),
        compiler_params=pltpu.CompilerParams(
            dimension_semantics=("arbitrary",), vmem_limit_bytes=VMEM_LIMIT),
        name="out_proj",
    )(x2d, y_dn, y_att, w_out)


def _gate_rows(v):
    col = jnp.pad(v.reshape(-1, 1).astype(F32), ((DN_HEADS, SUBLANES - 2 * DN_HEADS), (0, 0)))
    return jnp.broadcast_to(col, (SUBLANES, LANES))


def _layer(x, norm_w, w_in, conv_w, a_log, dt_bias, dn_norm_w, q_norm_w, k_norm_w, bias, w_out):
    bsz, seq, _ = x.shape
    x2d = x.reshape(bsz * seq, D_MODEL)
    c_dn = 4 * D_DN
    c_ba = c_dn + 2 * DN_HEADS
    w_ba = jnp.pad(w_in[:, c_dn:c_ba], ((0, 0), (0, LANES - 2 * DN_HEADS)))
    w_all = jnp.concatenate([w_in[:, :c_dn], w_in[:, c_ba:], w_ba], axis=1).astype(BF16)

    qw = jnp.tile(q_norm_w.reshape(1, ATT_HEAD_DIM).astype(F32), (1, 2)) * (ATT_HEAD_DIM ** -0.5 * LOG2E)
    kw = jnp.tile(k_norm_w.reshape(1, ATT_HEAD_DIM).astype(F32), (1, 2))
    dn_qkv, dn_z, ba, att_g, *att_perm = _in_proj(x, norm_w.reshape(1, D_MODEL), w_all, qw, kw)
    att_perm = [a.reshape(bsz, seq, 3 * D_ATT) for a in att_perm]

    y_dn = _deltanet(dn_qkv, dn_z, ba, conv_w, _gate_rows(a_log), _gate_rows(dt_bias),
                     dn_norm_w.reshape(1, DN_HEAD_DIM))
    y_att = _dilated_attn(att_perm, att_g, bias)
    out = _out_proj(x2d, y_dn.reshape(bsz * seq, D_DN), y_att.reshape(bsz * seq, D_ATT),
                    w_out.astype(BF16))
    return out.reshape(bsz, seq, D_MODEL)


def kernel(x, norm_w, w_in, conv_w, a_log, dt_bias, dn_norm_w, q_norm_w, k_norm_w, rel_bias, w_out):
    bias = _bias_tables(rel_bias.astype(F32))
    for layer in range(norm_w.shape[0]):
        x = _layer(x, norm_w[layer], w_in[layer], conv_w[layer], a_log[layer], dt_bias[layer],
                   dn_norm_w[layer], q_norm_w[layer], k_norm_w[layer], bias, w_out[layer])
    return x
```

```python
import functools
import math

import numpy as np
import jax
import jax.numpy as jnp
from jax import lax
from jax.experimental import pallas as pl
from jax.experimental.pallas import tpu as pltpu

D_MODEL = 1024
D_DN = 512
DN_HEADS = 4
DN_HEAD_DIM = 128
CONV_WIDTH = 4
CHUNK = 64
D_ATT = 512
ATT_HEADS = 8
ATT_HEAD_DIM = 64
DILATED_PATTERNS = ((128, 1), (512, 4), (2048, 16))
N_BUCKETS = 32
MAX_DISTANCE = 2048
D_MIX = D_DN + D_ATT
EPS = 1e-6

LANES = 128
SUBLANES = 8
BLK = 128
ATT_GROUP = 8
SUPER = 128
DN_UNROLL = 4
SUB = 8
NEG = -1e30
LOG2E = math.log2(math.e)
N_BIAS_VARIANTS = 3
VMEM_LIMIT = 56 * 1024 * 1024

F32 = jnp.float32
BF16 = jnp.bfloat16


def _mm(a, b):
    return jnp.dot(a.astype(BF16), b.astype(BF16), preferred_element_type=F32)


def _mm_nt(a, b):
    return lax.dot_general(a.astype(BF16), b.astype(BF16), (((1,), (1,)), ((), ())),
                           preferred_element_type=F32)


def _sigmoid(x):
    return 0.5 * jnp.tanh(0.5 * x) + 0.5


def _in_proj_kernel(x_ref, nw_ref, w_ref, qw_ref, kw_ref, dnw_ref,
                    dnqkv_ref, dnz_ref, ba_ref, attg_ref, *att_refs_and_scratch):
    att_refs = att_refs_and_scratch[:len(DILATED_PATTERNS)]
    att_s, perm_s = att_refs_and_scratch[len(DILATED_PATTERNS):]
    tm = x_ref.shape[1]
    x = x_ref[0]
    ms = jnp.mean(x * x, axis=-1, keepdims=True)
    h = (x * lax.rsqrt(ms + EPS) * nw_ref[...]).astype(BF16)
    c0 = 3 * D_DN
    c1 = c0 + D_DN
    c2 = c1 + 3 * D_ATT
    c3 = c2 + D_ATT

    att = jnp.dot(h, w_ref[:, c1:c2], preferred_element_type=F32)
    head0 = lax.broadcasted_iota(jnp.int32, (tm, LANES), 1) < ATT_HEAD_DIM
    n_slab = 3 * D_ATT // LANES
    for j in range(n_slab):
        xs = att[:, j * LANES:(j + 1) * LANES]
        if j < 2 * D_ATT // LANES:
            is_q = j < D_ATT // LANES
            x2 = xs * xs
            s0 = jnp.sum(jnp.where(head0, x2, 0.0), axis=-1, keepdims=True)
            s1 = jnp.sum(jnp.where(head0, 0.0, x2), axis=-1, keepdims=True)
            ss = jnp.where(head0, s0, s1)
            wn = qw_ref[...] if is_q else kw_ref[...]
            xs = xs * lax.rsqrt(ss * (1.0 / ATT_HEAD_DIM) + EPS) * wn
        att_s[j] = xs
    for j in range(n_slab):
        cols = slice(j * LANES, (j + 1) * LANES)
        src_ref, r_prev = att_s.at[j], 1
        for level, (o_ref, (_, r)) in enumerate(zip(att_refs, DILATED_PATTERNS)):
            step = r // r_prev
            rows_per = tm // r
            keep = level + 1 < len(DILATED_PATTERNS) and r > 1
            for res in range(r):
                if r == 1:
                    piece = src_ref[...]
                else:
                    start = (res % r_prev) * (tm // r_prev) + res // r_prev
                    piece = src_ref[pl.ds(start, rows_per, stride=step), :]
                o_ref[0, res, :, cols] = piece.astype(BF16)
                if keep:
                    perm_s[j, pl.ds(res * rows_per, rows_per), :] = piece
            if keep:
                src_ref, r_prev = perm_s.at[j], r

    dnqkv_ref[0] = jnp.dot(h, w_ref[:, 0:c0], preferred_element_type=F32)
    z = jnp.dot(h, w_ref[:, c0:c1], preferred_element_type=F32)
    dnz_ref[0] = (z * _sigmoid(z) * dnw_ref[...]).astype(BF16)
    gate = jnp.dot(h, w_ref[:, c2:c3], preferred_element_type=F32)
    attg_ref[0] = (gate * _sigmoid(gate)).astype(BF16)
    ba_ref[0] = jnp.dot(h, w_ref[:, c3:c3 + LANES], preferred_element_type=F32)


def _in_proj(x, norm_w, w_all, qw, kw, dnw, tm=512):
    bsz, seq, _ = x.shape
    ncol = w_all.shape[1]
    row = lambda b, i: (b, i, 0)
    fixed = lambda b, i: (0, 0)
    att_specs = [pl.BlockSpec((1, r, tm // r, 3 * D_ATT), lambda b, i: (b, 0, i, 0))
                 for _, r in DILATED_PATTERNS]
    att_shapes = [jax.ShapeDtypeStruct((bsz, r, seq // r, 3 * D_ATT), BF16)
                  for _, r in DILATED_PATTERNS]
    return pl.pallas_call(
        _in_proj_kernel,
        grid=(bsz, seq // tm),
        in_specs=[pl.BlockSpec((1, tm, D_MODEL), row),
                  pl.BlockSpec((1, D_MODEL), fixed),
                  pl.BlockSpec((D_MODEL, ncol), fixed),
                  pl.BlockSpec((1, LANES), fixed),
                  pl.BlockSpec((1, LANES), fixed),
                  pl.BlockSpec((1, D_DN), fixed)],
        out_specs=[pl.BlockSpec((1, tm, 3 * D_DN), row),
                   pl.BlockSpec((1, tm, D_DN), row),
                   pl.BlockSpec((1, tm, LANES), row),
                   pl.BlockSpec((1, tm, D_ATT), row)] + att_specs,
        out_shape=[jax.ShapeDtypeStruct((bsz, seq, 3 * D_DN), F32),
                   jax.ShapeDtypeStruct((bsz, seq, D_DN), BF16),
                   jax.ShapeDtypeStruct((bsz, seq, LANES), F32),
                   jax.ShapeDtypeStruct((bsz, seq, D_ATT), BF16)] + att_shapes,
        scratch_shapes=[pltpu.VMEM((3 * D_ATT // LANES, tm, LANES), F32),
                        pltpu.VMEM((3 * D_ATT // LANES, tm, LANES), F32)],
        compiler_params=pltpu.CompilerParams(
            dimension_semantics=("arbitrary", "arbitrary"), vmem_limit_bytes=VMEM_LIMIT),
        name="in_proj",
    )(x, norm_w, w_all, qw, kw, dnw)


def _t5_bucket(dist):
    max_exact = N_BUCKETS // 2
    d = np.maximum(dist, 1).astype(np.float64)
    large = max_exact + (np.log(d / max_exact) / math.log(MAX_DISTANCE / max_exact)
                         * (N_BUCKETS - max_exact)).astype(np.int32)
    large = np.minimum(large, N_BUCKETS - 1)
    return np.where(dist < max_exact, dist, large).astype(np.int32)


def _bucket_maps():
    maps = []
    for window, r in DILATED_PATTERNS:
        w_steps = window // r
        assert w_steps == BLK
        qi = np.arange(BLK)[:, None]
        kj = np.arange(2 * BLK)[None, :]
        step = qi - kj + BLK
        band = (step >= 0) & (step <= w_steps)
        buckets = _t5_bucket(np.clip(step, 0, None) * r)
        maps.append(np.where(band, buckets, -1).astype(np.int32))
    return np.stack(maps)


def _bias_kernel(rel_ref, bkt_ref, o_ref):
    col = lax.broadcasted_iota(jnp.int32, (BLK, 2 * BLK), 1)

    def head_body(h, carry):
        bkt = bkt_ref[0]
        acc = jnp.full(bkt.shape, NEG, F32)
        for b in range(N_BUCKETS):
            acc = jnp.where(bkt == b, rel_ref[h, b] * LOG2E, acc)
        o_ref[0, h, 0] = acc
        o_ref[0, h, 1] = jnp.where(col < BLK, NEG, acc)
        o_ref[0, h, 2] = jnp.concatenate([acc[:, BLK:], jnp.full((BLK, BLK), NEG, F32)], axis=1)
        return carry
    lax.fori_loop(0, ATT_HEADS, head_body, 0)


def _bias_tables(rel_bias):
    bkt = jnp.asarray(_bucket_maps())
    n_pat = bkt.shape[0]
    return pl.pallas_call(
        _bias_kernel,
        grid=(n_pat,),
        in_specs=[pl.BlockSpec(memory_space=pltpu.SMEM),
                  pl.BlockSpec((1, BLK, 2 * BLK), lambda p: (p, 0, 0))],
        out_specs=pl.BlockSpec((1, ATT_HEADS, N_BIAS_VARIANTS, BLK, 2 * BLK),
                               lambda p: (p, 0, 0, 0, 0)),
        out_shape=jax.ShapeDtypeStruct((n_pat, ATT_HEADS, N_BIAS_VARIANTS, BLK, 2 * BLK), F32),
        name="bias_tables",
    )(rel_bias, bkt)


def _dn_kernel(*refs, tile):
    nh = DN_HEADS
    n_stream = 3 * nh
    x_refs = refs[:n_stream]
    (cw_ref, ba_ref, z_ref, alog_ref, dtb_ref, cf_ref, cb_ref,
     o_ref, state_s, xs) = refs[n_stream:]
    n_chunk = SUPER // CHUNK
    heads = range(nh)
    lead = SUBLANES

    @pl.when(pl.program_id(1) == 0)
    def _():
        state_s[...] = jnp.zeros(state_s.shape, F32)
        xs[:, 0:lead, :] = jnp.zeros((n_stream, lead, LANES), F32)

    def copy_body(i, carry):
        r0 = pl.multiple_of(i * SUPER, SUPER)
        for j in range(n_stream):
            xs[j, pl.ds(r0 + lead, SUPER), :] = x_refs[j][0, pl.ds(r0, SUPER), :]
        return carry
    lax.fori_loop(0, tile // SUPER, copy_body, 0)

    eye = lambda: cf_ref[0]
    neg_outside_incl = lambda: cf_ref[1]
    neg_in_strict = lambda: cf_ref[2]
    in_sub = lambda: cf_ref[3]
    cum_mat = lambda: cb_ref[0]
    chunk_cols = lambda c: cb_ref[1 + c]
    is_beta_row = lax.broadcasted_iota(jnp.int32, (SUBLANES, SUPER), 0) < nh

    a_coef = -LOG2E * jnp.exp(alog_ref[...])
    dt_b = dtb_ref[...]

    def conv_silu(j, r0):
        def tap(s):
            w = 0.5 * cw_ref[CONV_WIDTH - 1 - s, j]
            return xs[j, pl.ds(r0 + lead - s, SUPER), :] * jnp.tile(w, (SUPER // SUBLANES, 1))
        h = tap(0)
        for s in range(1, CONV_WIDTH):
            h = h + tap(s)
        return h * jnp.tanh(h) + h

    def l2n(x):
        return x * lax.rsqrt(jnp.sum(x * x, axis=-1, keepdims=True) + EPS)

    def lane_bcast(x, j):
        return jnp.broadcast_to(x[:, j:j + 1], x.shape)

    def row_bcast(x, i, n_rows):
        return jnp.broadcast_to(x[i:i + 1, :], (n_rows, x.shape[1]))

    def gate_rows(r0):
        pre = ba_ref[0, pl.ds(r0, SUPER), :].T[0:SUBLANES, :]
        xg = pre + dt_b
        g = a_coef * (jnp.maximum(xg, 0.0) + jnp.log(1.0 + jnp.exp(-jnp.abs(xg))))
        g_hi = g.astype(BF16)
        g_r = g - g_hi.astype(F32)
        g_mid = g_r.astype(BF16)
        g_lo = (g_r - g_mid.astype(F32)).astype(BF16)
        cs = jnp.dot(jnp.concatenate([g_hi, g_mid, g_lo], axis=0), cum_mat(),
                     preferred_element_type=F32)
        gc_r = cs[0:SUBLANES] + cs[SUBLANES:2 * SUBLANES] + cs[2 * SUBLANES:3 * SUBLANES]
        gates = jnp.where(is_beta_row, _sigmoid(pre), gc_r)
        gates_c = jnp.concatenate([gates, jnp.zeros((SUPER - SUBLANES, SUPER), F32)], axis=0).T
        return gc_r, gates_c

    def body(it, carry):
        r0s = [pl.multiple_of((it * DN_UNROLL + u) * SUPER, SUPER) for u in range(DN_UNROLL)]
        items = [(u, h) for u in range(DN_UNROLL) for h in heads]
        idx = range(len(items))
        gate = [gate_rows(r0) for r0 in r0s]

        q = [l2n(conv_silu(h, r0s[u])) * (DN_HEAD_DIM ** -0.5) for u, h in items]
        k = [l2n(conv_silu(nh + h, r0s[u])) for u, h in items]
        v = [conv_silu(2 * nh + h, r0s[u]) for u, h in items]

        beta = [lane_bcast(gate[u][1], h) for u, h in items]
        gc = [lane_bcast(gate[u][1], nh + h) for u, h in items]
        decay = [jnp.exp2(gc[i] - row_bcast(gate[u][0], nh + h, SUPER) + neg_outside_incl())
                 for i, (u, h) in enumerate(items)]
        gc_last = [jnp.concatenate([row_bcast(gc[i], c * CHUNK + CHUNK - 1, CHUNK)
                                    for c in range(n_chunk)], axis=0) for i in idx]
        e_gc = [jnp.exp2(gc[i]) for i in idx]
        kb = [k[i] * beta[i] for i in idx]

        kq = [_mm_nt(jnp.concatenate([kb[i], q[i]], axis=0), k[i]) for i in idx]
        neg_a = [kq[i][:SUPER] * decay[i] * neg_in_strict() for i in idx]
        attn = [(kq[i][SUPER:] * decay[i]).astype(BF16) for i in idx]
        rhs = [jnp.concatenate([v[i] * beta[i], kb[i] * e_gc[i]], axis=1).astype(BF16) for i in idx]

        x1 = [neg_a[i] * in_sub() for i in idx]
        neg_l = [(neg_a[i] - x1[i]).astype(BF16) for i in idx]
        x1b = [x1[i].astype(BF16) for i in idx]
        x2 = [_mm(x1b[i], x1b[i]) for i in idx]
        x2b = [x2[i].astype(BF16) for i in idx]
        x4 = [_mm(x2b[i], x2b[i]) for i in idx]
        p1 = [eye() + x1[i] + x2[i] + _mm(x1b[i], x2b[i]) for i in idx]
        t_d = [p1[i] + _mm(p1[i], x4[i]) for i in idx]
        t_db = [t_d[i].astype(BF16) for i in idx]
        y1 = [_mm(t_db[i], neg_l[i]) for i in idx]
        td_rhs = [_mm(t_db[i], rhs[i]).astype(BF16) for i in idx]
        y1b = [y1[i].astype(BF16) for i in idx]
        y2 = [_mm(y1b[i], y1b[i]) for i in idx]
        y2b = [y2[i].astype(BF16) for i in idx]
        y4 = [_mm(y2b[i], y2b[i]) for i in idx]
        q1 = [eye() + y1[i] + y2[i] + _mm(y1b[i], y2b[i]) for i in idx]
        q2 = [q1[i] + _mm(q1[i], y4[i]) for i in idx]

        uw = [_mm(q2[i], td_rhs[i]).astype(BF16) for i in idx]
        aw = [_mm(attn[i], uw[i]) for i in idx]
        q_t = [(q[i] * e_gc[i] - aw[i][:, LANES:]).astype(BF16) for i in idx]
        kt_t = [(k[i] * jnp.exp2(gc_last[i] - gc[i])).T.astype(BF16) for i in idx]
        kw = [[_mm(kt_t[i] * chunk_cols(c), uw[i])
               for c in range(n_chunk)] for i in idx]

        state = [state_s[h] for h in heads]
        for u in range(DN_UNROLL):
            outs = [[] for _ in heads]
            for c in range(n_chunk):
                sl = slice(c * CHUNK, (c + 1) * CHUNK)
                s_bf = [state[h].astype(BF16) for h in heads]
                for h in heads:
                    i = u * nh + h
                    outs[h].append(_mm(q_t[i][sl], s_bf[h]) + aw[i][sl, :LANES])
                for h in heads:
                    i = u * nh + h
                    g_last = jnp.exp2(row_bcast(gc[i], c * CHUNK + CHUNK - 1, LANES))
                    state[h] = (state[h] * g_last - _mm(kw[i][c][:, LANES:], s_bf[h])
                                + kw[i][c][:, :LANES])
            rows = pl.ds(r0s[u], SUPER)
            for h in heads:
                o = jnp.concatenate(outs[h], axis=0)
                ms = jnp.mean(o * o, axis=-1, keepdims=True)
                z = z_ref[0, rows, h * LANES:(h + 1) * LANES].astype(F32)
                y = o * lax.rsqrt(ms + EPS) * z
                o_ref[0, rows, h * LANES:(h + 1) * LANES] = y.astype(o_ref.dtype)
        for h in heads:
            state_s[h] = state[h]
        return carry

    lax.fori_loop(0, tile // (SUPER * DN_UNROLL), body, 0)
    xs[:, 0:lead, :] = xs[:, tile:tile + lead, :]


def _dn_constants():
    ri = np.arange(SUPER)[:, None]
    ci = np.arange(SUPER)[None, :]
    same_chunk = (ri // CHUNK) == (ci // CHUNK)
    cf = np.stack([
        (ri == ci).astype(np.float32),
        np.where(same_chunk & (ri >= ci), 0.0, NEG).astype(np.float32),
        -(same_chunk & (ri > ci)).astype(np.float32),
        ((ri // SUB) == (ci // SUB)).astype(np.float32)])
    cb = np.stack([(same_chunk & (ri <= ci)).astype(np.float32)]
                  + [np.broadcast_to((ci // CHUNK) == c, (SUPER, SUPER)).astype(np.float32)
                     for c in range(SUPER // CHUNK)])
    return jnp.asarray(cf), jnp.asarray(cb, dtype=BF16)


def _deltanet(dn_qkv, dn_z, ba, conv_w, a_log, dt_bias, tile=1024):
    bsz, seq, _ = dn_qkv.shape
    n_stream = 3 * DN_HEADS
    seq_tile = lambda b, t: (b, t, 0)
    fixed = lambda b, t: (0, 0)
    cf, cb = _dn_constants()
    cw_tiles = jnp.broadcast_to(conv_w.astype(F32).reshape(CONV_WIDTH, n_stream, 1, LANES),
                                (CONV_WIDTH, n_stream, SUBLANES, LANES))

    def slab(j):
        return pl.BlockSpec((1, tile, LANES), lambda b, t: (b, t, j))

    def whole(a):
        return pl.BlockSpec(a.shape, lambda b, t: (0,) * a.ndim)

    return pl.pallas_call(
        functools.partial(_dn_kernel, tile=tile),
        grid=(bsz, seq // tile),
        in_specs=[slab(j) for j in range(n_stream)]
        + [whole(cw_tiles),
           pl.BlockSpec((1, tile, LANES), seq_tile),
           pl.BlockSpec((1, tile, D_DN), seq_tile),
           pl.BlockSpec((SUBLANES, LANES), fixed),
           pl.BlockSpec((SUBLANES, LANES), fixed),
           whole(cf),
           whole(cb)],
        out_specs=pl.BlockSpec((1, tile, D_DN), seq_tile),
        out_shape=jax.ShapeDtypeStruct((bsz, seq, D_DN), BF16),
        scratch_shapes=[pltpu.VMEM((DN_HEADS, DN_HEAD_DIM, DN_HEAD_DIM), F32),
                        pltpu.VMEM((n_stream, tile + SUBLANES, LANES), F32)],
        compiler_params=pltpu.CompilerParams(
            dimension_semantics=("arbitrary", "arbitrary"), vmem_limit_bytes=VMEM_LIMIT),
        name="deltanet",
    )(*([dn_qkv] * n_stream), cw_tiles, ba, dn_z, a_log, dt_bias, cf, cb)


def _part_pitch(seq, r):
    return seq // r + 1 if r >= 2 * SUBLANES else None


def _part_rows(seq, r):
    pitch = _part_pitch(seq, r)
    return seq if pitch is None else -(-(r * pitch) // SUBLANES) * SUBLANES


def _att_kernel(*refs, seq):
    n_pat = len(DILATED_PATTERNS)
    qkv_refs = [refs[3 * p:3 * p + 3] for p in range(n_pat)]
    g_ref, bias_ref, o_ref = refs[3 * n_pat:3 * n_pat + 3]
    part_s = refs[3 * n_pat + 3:]
    tile = 512
    lane = lax.broadcasted_iota(jnp.int32, (BLK, LANES), 1)
    head0 = lane < ATT_HEAD_DIM
    ones_v = jnp.ones((2 * BLK, LANES), BF16)
    zero_q = jnp.zeros((BLK, LANES), BF16)

    def store_part(p, r, res, n, vals):
        pitch = _part_pitch(seq, r)
        if pitch is None:
            tok0 = res + n * (BLK * r)
            rows = pl.ds(tok0, BLK) if r == 1 else pl.ds(tok0, BLK, stride=r)
        else:
            rows = pl.ds(res * pitch + n * BLK, BLK)
        for a, val in enumerate(vals):
            part_s[p][a, rows, :] = val

    def load_part(p, r, a, t0):
        pitch = _part_pitch(seq, r)
        if pitch is None:
            return part_s[p][a, pl.ds(t0, tile), :]
        m0 = t0 // r
        return jnp.concatenate([part_s[p][a, pl.ds(m0 + jj, r, stride=pitch), :]
                                for jj in range(tile // r)], axis=0)

    def group(p, r, n_blk, bi0):
        q_ref, k_ref, v_ref = qkv_refs[p]
        idx = []
        for g in range(ATT_GROUP):
            bi = bi0 + g
            base = pl.multiple_of(bi * BLK, BLK)
            kbase = pl.multiple_of(jnp.maximum(base - BLK, 0), BLK)
            n = bi % n_blk
            variant = jnp.where(bi == 0, 2, jnp.where(n == 0, 1, 0))
            idx.append((base, kbase, variant, (bi // n_blk, n)))
        scores = []
        for base, kbase, variant, _ in idx:
            q = q_ref[0, pl.ds(base, BLK), :]
            q2 = jnp.concatenate([jnp.where(head0, q, zero_q), jnp.where(head0, zero_q, q)], axis=0)
            k = k_ref[0, pl.ds(kbase, 2 * BLK), :]
            bias = jnp.concatenate([bias_ref[p, 0, variant], bias_ref[p, 1, variant]], axis=0)
            scores.append(lax.dot_general(q2, k, (((1,), (1,)), ((), ())),
                                          preferred_element_type=F32) + bias)
        maxes, probs = [], []
        for s in scores:
            m = jnp.max(s, axis=-1, keepdims=True)
            maxes.append(m)
            probs.append(jnp.exp2(s - m).astype(BF16))
        for (base, kbase, variant, (res, n)), m, e in zip(idx, maxes, probs):
            v2 = jnp.concatenate([v_ref[0, pl.ds(kbase, 2 * BLK), :], ones_v], axis=1)
            pv = jnp.dot(e, v2, preferred_element_type=F32)
            store_part(p, r, res, n,
                       (jnp.where(head0, m[:BLK], m[BLK:]),
                        jnp.where(head0, pv[:BLK, LANES:], pv[BLK:, LANES:]),
                        jnp.where(head0, pv[:BLK, :LANES], pv[BLK:, :LANES])))

    for p, (window, r) in enumerate(DILATED_PATTERNS):
        n_blk = seq // (r * BLK)

        def group_body(i, carry, p=p, r=r, n_blk=n_blk):
            group(p, r, n_blk, i * ATT_GROUP)
            return carry
        lax.fori_loop(0, seq // (BLK * ATT_GROUP), group_body, 0)

    def out_body(i, carry):
        t0 = pl.multiple_of(i * tile, tile)
        rows = pl.ds(t0, tile)
        part = [[load_part(p, r, a, t0) for a in range(3)]
                for p, (_, r) in enumerate(DILATED_PATTERNS)]
        m = functools.reduce(jnp.maximum, [pt[0] for pt in part])
        w = [jnp.exp2(pt[0] - m) for pt in part]
        den = sum(wp * pt[1] for wp, pt in zip(w, part))
        num = sum(wp * pt[2] for wp, pt in zip(w, part))
        g = g_ref[0, rows, :].astype(F32)
        o_ref[0, rows, :] = (num / den * g).astype(o_ref.dtype)
        return carry
    lax.fori_loop(0, seq // tile, out_body, 0)


def _dilated_attn(att_perm, att_g, bias):
    bsz, seq, _ = att_g.shape
    n_pair = ATT_HEADS // 2
    n_pat = bias.shape[0]

    def col(off):
        return pl.BlockSpec((1, seq, LANES), lambda b, j: (b, 0, off + j))

    qkv_specs, qkv_args = [], []
    for a in att_perm:
        qkv_specs += [col(0), col(n_pair), col(2 * n_pair)]
        qkv_args += [a, a, a]
    return pl.pallas_call(
        functools.partial(_att_kernel, seq=seq),
        grid=(bsz, n_pair),
        in_specs=qkv_specs
        + [col(0),
           pl.BlockSpec((n_pat, 2, N_BIAS_VARIANTS, BLK, 2 * BLK), lambda b, j: (0, j, 0, 0, 0))],
        out_specs=col(0),
        out_shape=jax.ShapeDtypeStruct((bsz, seq, D_ATT), BF16),
        scratch_shapes=[pltpu.VMEM((3, _part_rows(seq, r), LANES), F32)
                        for _, r in DILATED_PATTERNS],
        compiler_params=pltpu.CompilerParams(
            dimension_semantics=("arbitrary", "arbitrary"), vmem_limit_bytes=VMEM_LIMIT),
        name="dilated_attn",
    )(*qkv_args, att_g, bias)


def _out_proj_kernel(x_ref, ydn_ref, yatt_ref, w_ref, o_ref):
    o_ref[...] = (x_ref[...]
                  + jnp.dot(ydn_ref[...], w_ref[0:D_DN, :], preferred_element_type=F32)
                  + jnp.dot(yatt_ref[...], w_ref[D_DN:D_MIX, :], preferred_element_type=F32))


def _out_proj(x2d, y_dn, y_att, w_out, tm=1024):
    n = x2d.shape[0]
    row = lambda i: (i, 0)
    return pl.pallas_call(
        _out_proj_kernel,
        grid=(n // tm,),
        in_specs=[pl.BlockSpec((tm, D_MODEL), row),
                  pl.BlockSpec((tm, D_DN), row),
                  pl.BlockSpec((tm, D_ATT), row),
                  pl.BlockSpec((D_MIX, D_MODEL), lambda i: (0, 0))],
        out_specs=pl.BlockSpec((tm, D_MODEL), row),
        out_shape=jax.ShapeDtypeStruct((n, D_MODEL), F32),
        compiler_params=pltpu.CompilerParams(
            dimension_semantics=("arbitrary",), vmem_limit_bytes=VMEM_LIMIT),
        name="out_proj",
    )(x2d, y_dn, y_att, w_out)


def _gate_rows(v):
    col = jnp.pad(v.reshape(-1, 1).astype(F32), ((DN_HEADS, SUBLANES - 2 * DN_HEADS), (0, 0)))
    return jnp.broadcast_to(col, (SUBLANES, LANES))


def _layer(x, norm_w, w_in, conv_w, a_log, dt_bias, dn_norm_w, q_norm_w, k_norm_w, bias, w_out):
    bsz, seq, _ = x.shape
    x2d = x.reshape(bsz * seq, D_MODEL)
    c_dn = 4 * D_DN
    c_ba = c_dn + 2 * DN_HEADS
    w_ba = jnp.pad(w_in[:, c_dn:c_ba], ((0, 0), (0, LANES - 2 * DN_HEADS)))
    w_all = jnp.concatenate([w_in[:, :c_dn], w_in[:, c_ba:], w_ba], axis=1).astype(BF16)

    qw = jnp.tile(q_norm_w.reshape(1, ATT_HEAD_DIM).astype(F32), (1, 2)) * (ATT_HEAD_DIM ** -0.5 * LOG2E)
    kw = jnp.tile(k_norm_w.reshape(1, ATT_HEAD_DIM).astype(F32), (1, 2))
    dnw = jnp.tile(dn_norm_w.reshape(1, DN_HEAD_DIM).astype(F32), (1, DN_HEADS))
    dn_qkv, dn_z, ba, att_g, *att_perm = _in_proj(x, norm_w.reshape(1, D_MODEL), w_all, qw, kw, dnw)
    att_perm = [a.reshape(bsz, seq, 3 * D_ATT) for a in att_perm]

    y_dn = _deltanet(dn_qkv, dn_z, ba, conv_w, _gate_rows(a_log), _gate_rows(dt_bias))
    y_att = _dilated_attn(att_perm, att_g, bias)
    out = _out_proj(x2d, y_dn.reshape(bsz * seq, D_DN), y_att.reshape(bsz * seq, D_ATT),
                    w_out.astype(BF16))
    return out.reshape(bsz, seq, D_MODEL)


def kernel(x, norm_w, w_in, conv_w, a_log, dt_bias, dn_norm_w, q_norm_w, k_norm_w, rel_bias, w_out):
    bias = _bias_tables(rel_bias.astype(F32))
    for layer in range(norm_w.shape[0]):
        x = _layer(x, norm_w[layer], w_in[layer], conv_w[layer], a_log[layer], dt_bias[layer],
                   dn_norm_w[layer], q_norm_w[layer], k_norm_w[layer], bias, w_out[layer])
    return x
```

```python
import functools
import math

import numpy as np
import jax
import jax.numpy as jnp
from jax import lax
from jax.experimental import pallas as pl
from jax.experimental.pallas import tpu as pltpu

D_MODEL = 1024
D_DN = 512
DN_HEADS = 4
DN_HEAD_DIM = 128
CONV_WIDTH = 4
CHUNK = 64
D_ATT = 512
ATT_HEADS = 8
ATT_HEAD_DIM = 64
DILATED_PATTERNS = ((128, 1), (512, 4), (2048, 16))
N_BUCKETS = 32
MAX_DISTANCE = 2048
D_MIX = D_DN + D_ATT
EPS = 1e-6

LANES = 128
SUBLANES = 8
BLK = 128
ATT_GROUP = 8
SUPER = 128
DN_UNROLL = 4
SUB = 8
NEG = -1e30
LOG2E = math.log2(math.e)
N_BIAS_VARIANTS = 3
VMEM_LIMIT = 56 * 1024 * 1024

F32 = jnp.float32
BF16 = jnp.bfloat16


def _mm(a, b):
    return jnp.dot(a.astype(BF16), b.astype(BF16), preferred_element_type=F32)


def _mm_nt(a, b):
    return lax.dot_general(a.astype(BF16), b.astype(BF16), (((1,), (1,)), ((), ())),
                           preferred_element_type=F32)


def _sigmoid(x):
    return 0.5 * jnp.tanh(0.5 * x) + 0.5


def _in_proj_kernel(x_ref, nw_ref, w_ref, qw_ref, kw_ref, dnw_ref,
                    dnqkv_ref, dnz_ref, ba_ref, attg_ref, *att_refs_and_scratch):
    att_refs = att_refs_and_scratch[:len(DILATED_PATTERNS)]
    att_s, perm_s = att_refs_and_scratch[len(DILATED_PATTERNS):]
    tm = x_ref.shape[1]
    x = x_ref[0]
    ms = jnp.mean(x * x, axis=-1, keepdims=True)
    h = (x * lax.rsqrt(ms + EPS) * nw_ref[...]).astype(BF16)
    c0 = 3 * D_DN
    c1 = c0 + D_DN
    c2 = c1 + 3 * D_ATT
    c3 = c2 + D_ATT

    head0 = lax.broadcasted_iota(jnp.int32, (tm, LANES), 1) < ATT_HEAD_DIM
    n_slab = 3 * D_ATT // LANES
    pair = 2 * LANES
    for jp in range(n_slab // 2):
        att = jnp.dot(h, w_ref[:, c1 + jp * pair:c1 + (jp + 1) * pair], preferred_element_type=F32)
        for j in (2 * jp, 2 * jp + 1):
            xs = att[:, (j - 2 * jp) * LANES:(j - 2 * jp + 1) * LANES]
            if j < 2 * D_ATT // LANES:
                is_q = j < D_ATT // LANES
                x2 = xs * xs
                s0 = jnp.sum(jnp.where(head0, x2, 0.0), axis=-1, keepdims=True)
                s1 = jnp.sum(jnp.where(head0, 0.0, x2), axis=-1, keepdims=True)
                ss = jnp.where(head0, s0, s1)
                wn = qw_ref[...] if is_q else kw_ref[...]
                xs = xs * lax.rsqrt(ss * (1.0 / ATT_HEAD_DIM) + EPS) * wn
            att_s[j] = xs
            cols = slice(j * LANES, (j + 1) * LANES)
            src_ref, r_prev = att_s.at[j], 1
            for level, (o_ref, (_, r)) in enumerate(zip(att_refs, DILATED_PATTERNS)):
                step = r // r_prev
                rows_per = tm // r
                keep = level + 1 < len(DILATED_PATTERNS) and r > 1
                for res in range(r):
                    if r == 1:
                        piece = src_ref[...]
                    else:
                        start = (res % r_prev) * (tm // r_prev) + res // r_prev
                        piece = src_ref[pl.ds(start, rows_per, stride=step), :]
                    o_ref[0, res, :, cols] = piece.astype(BF16)
                    if keep:
                        perm_s[j, pl.ds(res * rows_per, rows_per), :] = piece
                if keep:
                    src_ref, r_prev = perm_s.at[j], r
        dnqkv_ref[0, :, jp * pair:(jp + 1) * pair] = jnp.dot(
            h, w_ref[:, jp * pair:(jp + 1) * pair], preferred_element_type=F32)
    z = jnp.dot(h, w_ref[:, c0:c1], preferred_element_type=F32)
    dnz_ref[0] = (z * _sigmoid(z) * dnw_ref[...]).astype(BF16)
    gate = jnp.dot(h, w_ref[:, c2:c3], preferred_element_type=F32)
    attg_ref[0] = (gate * _sigmoid(gate)).astype(BF16)
    ba_ref[0] = jnp.dot(h, w_ref[:, c3:c3 + LANES], preferred_element_type=F32)


def _in_proj(x, norm_w, w_all, qw, kw, dnw, tm=512):
    bsz, seq, _ = x.shape
    ncol = w_all.shape[1]
    row = lambda b, i: (b, i, 0)
    fixed = lambda b, i: (0, 0)
    att_specs = [pl.BlockSpec((1, r, tm // r, 3 * D_ATT), lambda b, i: (b, 0, i, 0))
                 for _, r in DILATED_PATTERNS]
    att_shapes = [jax.ShapeDtypeStruct((bsz, r, seq // r, 3 * D_ATT), BF16)
                  for _, r in DILATED_PATTERNS]
    return pl.pallas_call(
        _in_proj_kernel,
        grid=(bsz, seq // tm),
        in_specs=[pl.BlockSpec((1, tm, D_MODEL), row),
                  pl.BlockSpec((1, D_MODEL), fixed),
                  pl.BlockSpec((D_MODEL, ncol), fixed),
                  pl.BlockSpec((1, LANES), fixed),
                  pl.BlockSpec((1, LANES), fixed),
                  pl.BlockSpec((1, D_DN), fixed)],
        out_specs=[pl.BlockSpec((1, tm, 3 * D_DN), row),
                   pl.BlockSpec((1, tm, D_DN), row),
                   pl.BlockSpec((1, tm, LANES), row),
                   pl.BlockSpec((1, tm, D_ATT), row)] + att_specs,
        out_shape=[jax.ShapeDtypeStruct((bsz, seq, 3 * D_DN), F32),
                   jax.ShapeDtypeStruct((bsz, seq, D_DN), BF16),
                   jax.ShapeDtypeStruct((bsz, seq, LANES), F32),
                   jax.ShapeDtypeStruct((bsz, seq, D_ATT), BF16)] + att_shapes,
        scratch_shapes=[pltpu.VMEM((3 * D_ATT // LANES, tm, LANES), F32),
                        pltpu.VMEM((3 * D_ATT // LANES, tm, LANES), F32)],
        compiler_params=pltpu.CompilerParams(
            dimension_semantics=("arbitrary", "arbitrary"), vmem_limit_bytes=VMEM_LIMIT),
        name="in_proj",
    )(x, norm_w, w_all, qw, kw, dnw)


def _t5_bucket(dist):
    max_exact = N_BUCKETS // 2
    d = np.maximum(dist, 1).astype(np.float64)
    large = max_exact + (np.log(d / max_exact) / math.log(MAX_DISTANCE / max_exact)
                         * (N_BUCKETS - max_exact)).astype(np.int32)
    large = np.minimum(large, N_BUCKETS - 1)
    return np.where(dist < max_exact, dist, large).astype(np.int32)


def _bucket_maps():
    maps = []
    for window, r in DILATED_PATTERNS:
        w_steps = window // r
        assert w_steps == BLK
        qi = np.arange(BLK)[:, None]
        kj = np.arange(2 * BLK)[None, :]
        step = qi - kj + BLK
        band = (step >= 0) & (step <= w_steps)
        buckets = _t5_bucket(np.clip(step, 0, None) * r)
        maps.append(np.where(band, buckets, -1).astype(np.int32))
    return np.stack(maps)


def _bias_kernel(rel_ref, bkt_ref, o_ref):
    col = lax.broadcasted_iota(jnp.int32, (BLK, 2 * BLK), 1)

    def head_body(h, carry):
        bkt = bkt_ref[0]
        acc = jnp.full(bkt.shape, NEG, F32)
        for b in range(N_BUCKETS):
            acc = jnp.where(bkt == b, rel_ref[h, b] * LOG2E, acc)
        o_ref[0, h, 0] = acc
        o_ref[0, h, 1] = jnp.where(col < BLK, NEG, acc)
        o_ref[0, h, 2] = jnp.concatenate([acc[:, BLK:], jnp.full((BLK, BLK), NEG, F32)], axis=1)
        return carry
    lax.fori_loop(0, ATT_HEADS, head_body, 0)


def _bias_tables(rel_bias):
    bkt = jnp.asarray(_bucket_maps())
    n_pat = bkt.shape[0]
    return pl.pallas_call(
        _bias_kernel,
        grid=(n_pat,),
        in_specs=[pl.BlockSpec(memory_space=pltpu.SMEM),
                  pl.BlockSpec((1, BLK, 2 * BLK), lambda p: (p, 0, 0))],
        out_specs=pl.BlockSpec((1, ATT_HEADS, N_BIAS_VARIANTS, BLK, 2 * BLK),
                               lambda p: (p, 0, 0, 0, 0)),
        out_shape=jax.ShapeDtypeStruct((n_pat, ATT_HEADS, N_BIAS_VARIANTS, BLK, 2 * BLK), F32),
        name="bias_tables",
    )(rel_bias, bkt)


def _dn_kernel(*refs, tile):
    nh = DN_HEADS
    n_stream = 3 * nh
    x_refs = refs[:n_stream]
    (cw_ref, ba_ref, z_ref, alog_ref, dtb_ref, cf_ref, cb_ref,
     o_ref, state_s, xs) = refs[n_stream:]
    n_chunk = SUPER // CHUNK
    heads = range(nh)
    lead = SUBLANES

    @pl.when(pl.program_id(1) == 0)
    def _():
        state_s[...] = jnp.zeros(state_s.shape, F32)
        xs[:, 0:lead, :] = jnp.zeros((n_stream, lead, LANES), F32)

    def copy_body(i, carry):
        r0 = pl.multiple_of(i * SUPER, SUPER)
        for j in range(n_stream):
            xs[j, pl.ds(r0 + lead, SUPER), :] = x_refs[j][0, pl.ds(r0, SUPER), :]
        return carry
    lax.fori_loop(0, tile // SUPER, copy_body, 0)

    eye = lambda: cf_ref[0]
    neg_outside_incl = lambda: cf_ref[1]
    neg_in_strict = lambda: cf_ref[2]
    in_sub = lambda: cf_ref[3]
    cum_mat = lambda: cb_ref[0]
    chunk_cols = lambda c: cb_ref[1 + c]
    is_beta_row = lax.broadcasted_iota(jnp.int32, (SUBLANES, SUPER), 0) < nh

    a_coef = -LOG2E * jnp.exp(alog_ref[...])
    dt_b = dtb_ref[...]

    def conv_silu(j, r0):
        def tap(s):
            w = 0.5 * cw_ref[CONV_WIDTH - 1 - s, j]
            return xs[j, pl.ds(r0 + lead - s, SUPER), :] * jnp.tile(w, (SUPER // SUBLANES, 1))
        h = tap(0)
        for s in range(1, CONV_WIDTH):
            h = h + tap(s)
        return h * jnp.tanh(h) + h

    def l2n(x):
        return x * lax.rsqrt(jnp.sum(x * x, axis=-1, keepdims=True) + EPS)

    def lane_bcast(x, j):
        return jnp.broadcast_to(x[:, j:j + 1], x.shape)

    def row_bcast(x, i, n_rows):
        return jnp.broadcast_to(x[i:i + 1, :], (n_rows, x.shape[1]))

    def gate_rows(r0):
        pre = ba_ref[0, pl.ds(r0, SUPER), :].T[0:SUBLANES, :]
        xg = pre + dt_b
        g = a_coef * (jnp.maximum(xg, 0.0) + jnp.log(1.0 + jnp.exp(-jnp.abs(xg))))
        g_hi = g.astype(BF16)
        g_r = g - g_hi.astype(F32)
        g_mid = g_r.astype(BF16)
        g_lo = (g_r - g_mid.astype(F32)).astype(BF16)
        cs = jnp.dot(jnp.concatenate([g_hi, g_mid, g_lo], axis=0), cum_mat(),
                     preferred_element_type=F32)
        gc_r = cs[0:SUBLANES] + cs[SUBLANES:2 * SUBLANES] + cs[2 * SUBLANES:3 * SUBLANES]
        gates = jnp.where(is_beta_row, _sigmoid(pre), gc_r)
        gates_c = jnp.concatenate([gates, jnp.zeros((SUPER - SUBLANES, SUPER), F32)], axis=0).T
        return gc_r, gates_c

    def body(it, carry):
        r0s = [pl.multiple_of((it * DN_UNROLL + u) * SUPER, SUPER) for u in range(DN_UNROLL)]
        items = [(u, h) for u in range(DN_UNROLL) for h in heads]
        idx = range(len(items))
        gate = [gate_rows(r0) for r0 in r0s]

        q = [l2n(conv_silu(h, r0s[u])) * (DN_HEAD_DIM ** -0.5) for u, h in items]
        k = [l2n(conv_silu(nh + h, r0s[u])) for u, h in items]
        v = [conv_silu(2 * nh + h, r0s[u]) for u, h in items]

        beta = [lane_bcast(gate[u][1], h) for u, h in items]
        gc = [lane_bcast(gate[u][1], nh + h) for u, h in items]
        decay = [jnp.exp2(gc[i] - row_bcast(gate[u][0], nh + h, SUPER) + neg_outside_incl())
                 for i, (u, h) in enumerate(items)]
        gc_last = [jnp.concatenate([row_bcast(gc[i], c * CHUNK + CHUNK - 1, CHUNK)
                                    for c in range(n_chunk)], axis=0) for i in idx]
        e_gc = [jnp.exp2(gc[i]) for i in idx]
        kb = [k[i] * beta[i] for i in idx]

        kq = [_mm_nt(jnp.concatenate([kb[i], q[i]], axis=0), k[i]) for i in idx]
        neg_a = [kq[i][:SUPER] * decay[i] * neg_in_strict() for i in idx]
        attn = [(kq[i][SUPER:] * decay[i]).astype(BF16) for i in idx]
        rhs = [jnp.concatenate([v[i] * beta[i], kb[i] * e_gc[i]], axis=1).astype(BF16) for i in idx]

        x1 = [neg_a[i] * in_sub() for i in idx]
        neg_l = [(neg_a[i] - x1[i]).astype(BF16) for i in idx]
        x1b = [x1[i].astype(BF16) for i in idx]
        x2 = [_mm(x1b[i], x1b[i]) for i in idx]
        x2b = [x2[i].astype(BF16) for i in idx]
        x4 = [_mm(x2b[i], x2b[i]) for i in idx]
        p1 = [eye() + x1[i] + x2[i] + _mm(x1b[i], x2b[i]) for i in idx]
        t_d = [p1[i] + _mm(p1[i], x4[i]) for i in idx]
        t_db = [t_d[i].astype(BF16) for i in idx]
        y1 = [_mm(t_db[i], neg_l[i]) for i in idx]
        td_rhs = [_mm(t_db[i], rhs[i]).astype(BF16) for i in idx]
        y1b = [y1[i].astype(BF16) for i in idx]
        y2 = [_mm(y1b[i], y1b[i]) for i in idx]
        y2b = [y2[i].astype(BF16) for i in idx]
        y4 = [_mm(y2b[i], y2b[i]) for i in idx]
        q1 = [eye() + y1[i] + y2[i] + _mm(y1b[i], y2b[i]) for i in idx]
        q2 = [q1[i] + _mm(q1[i], y4[i]) for i in idx]

        uw = [_mm(q2[i], td_rhs[i]).astype(BF16) for i in idx]
        aw = [_mm(attn[i], uw[i]) for i in idx]
        q_t = [(q[i] * e_gc[i] - aw[i][:, LANES:]).astype(BF16) for i in idx]
        kt_t = [(k[i] * jnp.exp2(gc_last[i] - gc[i])).T.astype(BF16) for i in idx]
        kw = [[_mm(kt_t[i] * chunk_cols(c), uw[i])
               for c in range(n_chunk)] for i in idx]

        state = [state_s[h] for h in heads]
        for u in range(DN_UNROLL):
            outs = [[] for _ in heads]
            for c in range(n_chunk):
                sl = slice(c * CHUNK, (c + 1) * CHUNK)
                s_bf = [state[h].astype(BF16) for h in heads]
                for h in heads:
                    i = u * nh + h
                    outs[h].append(_mm(q_t[i][sl], s_bf[h]) + aw[i][sl, :LANES])
                for h in heads:
                    i = u * nh + h
                    g_last = jnp.exp2(row_bcast(gc[i], c * CHUNK + CHUNK - 1, LANES))
                    state[h] = (state[h] * g_last - _mm(kw[i][c][:, LANES:], s_bf[h])
                                + kw[i][c][:, :LANES])
            rows = pl.ds(r0s[u], SUPER)
            for h in heads:
                o = jnp.concatenate(outs[h], axis=0)
                ms = jnp.mean(o * o, axis=-1, keepdims=True)
                z = z_ref[0, rows, h * LANES:(h + 1) * LANES].astype(F32)
                y = o * lax.rsqrt(ms + EPS) * z
                o_ref[0, rows, h * LANES:(h + 1) * LANES] = y.astype(o_ref.dtype)
        for h in heads:
            state_s[h] = state[h]
        return carry

    lax.fori_loop(0, tile // (SUPER * DN_UNROLL), body, 0)
    xs[:, 0:lead, :] = xs[:, tile:tile + lead, :]


def _dn_constants():
    ri = np.arange(SUPER)[:, None]
    ci = np.arange(SUPER)[None, :]
    same_chunk = (ri // CHUNK) == (ci // CHUNK)
    cf = np.stack([
        (ri == ci).astype(np.float32),
        np.where(same_chunk & (ri >= ci), 0.0, NEG).astype(np.float32),
        -(same_chunk & (ri > ci)).astype(np.float32),
        ((ri // SUB) == (ci // SUB)).astype(np.float32)])
    cb = np.stack([(same_chunk & (ri <= ci)).astype(np.float32)]
                  + [np.broadcast_to((ci // CHUNK) == c, (SUPER, SUPER)).astype(np.float32)
                     for c in range(SUPER // CHUNK)])
    return jnp.asarray(cf), jnp.asarray(cb, dtype=BF16)


def _deltanet(dn_qkv, dn_z, ba, conv_w, a_log, dt_bias, tile=1024):
    bsz, seq, _ = dn_qkv.shape
    n_stream = 3 * DN_HEADS
    seq_tile = lambda b, t: (b, t, 0)
    fixed = lambda b, t: (0, 0)
    cf, cb = _dn_constants()
    cw_tiles = jnp.broadcast_to(conv_w.astype(F32).reshape(CONV_WIDTH, n_stream, 1, LANES),
                                (CONV_WIDTH, n_stream, SUBLANES, LANES))

    def slab(j):
        return pl.BlockSpec((1, tile, LANES), lambda b, t: (b, t, j))

    def whole(a):
        return pl.BlockSpec(a.shape, lambda b, t: (0,) * a.ndim)

    return pl.pallas_call(
        functools.partial(_dn_kernel, tile=tile),
        grid=(bsz, seq // tile),
        in_specs=[slab(j) for j in range(n_stream)]
        + [whole(cw_tiles),
           pl.BlockSpec((1, tile, LANES), seq_tile),
           pl.BlockSpec((1, tile, D_DN), seq_tile),
           pl.BlockSpec((SUBLANES, LANES), fixed),
           pl.BlockSpec((SUBLANES, LANES), fixed),
           whole(cf),
           whole(cb)],
        out_specs=pl.BlockSpec((1, tile, D_DN), seq_tile),
        out_shape=jax.ShapeDtypeStruct((bsz, seq, D_DN), BF16),
        scratch_shapes=[pltpu.VMEM((DN_HEADS, DN_HEAD_DIM, DN_HEAD_DIM), F32),
                        pltpu.VMEM((n_stream, tile + SUBLANES, LANES), F32)],
        compiler_params=pltpu.CompilerParams(
            dimension_semantics=("arbitrary", "arbitrary"), vmem_limit_bytes=VMEM_LIMIT),
        name="deltanet",
    )(*([dn_qkv] * n_stream), cw_tiles, ba, dn_z, a_log, dt_bias, cf, cb)


def _part_pitch(seq, r):
    return seq // r + 1 if r >= 2 * SUBLANES else None


def _part_rows(seq, r):
    pitch = _part_pitch(seq, r)
    return seq if pitch is None else -(-(r * pitch) // SUBLANES) * SUBLANES


def _att_kernel(*refs, seq):
    n_pat = len(DILATED_PATTERNS)
    qkv_refs = [refs[3 * p:3 * p + 3] for p in range(n_pat)]
    g_ref, bias_ref, o_ref = refs[3 * n_pat:3 * n_pat + 3]
    part_s = refs[3 * n_pat + 3:]
    tile = 512
    lane = lax.broadcasted_iota(jnp.int32, (BLK, LANES), 1)
    head0 = lane < ATT_HEAD_DIM
    ones_v = jnp.ones((2 * BLK, LANES), BF16)
    zero_q = jnp.zeros((BLK, LANES), BF16)

    def store_part(p, r, res, n, vals):
        pitch = _part_pitch(seq, r)
        if pitch is None:
            tok0 = res + n * (BLK * r)
            rows = pl.ds(tok0, BLK) if r == 1 else pl.ds(tok0, BLK, stride=r)
        else:
            rows = pl.ds(res * pitch + n * BLK, BLK)
        for a, val in enumerate(vals):
            part_s[p][a, rows, :] = val

    def load_part(p, r, a, t0):
        pitch = _part_pitch(seq, r)
        if pitch is None:
            return part_s[p][a, pl.ds(t0, tile), :]
        m0 = t0 // r
        return jnp.concatenate([part_s[p][a, pl.ds(m0 + jj, r, stride=pitch), :]
                                for jj in range(tile // r)], axis=0)

    def group(p, r, n_blk, bi0):
        q_ref, k_ref, v_ref = qkv_refs[p]
        idx = []
        for g in range(ATT_GROUP):
            bi = bi0 + g
            base = pl.multiple_of(bi * BLK, BLK)
            kbase = pl.multiple_of(jnp.maximum(base - BLK, 0), BLK)
            n = bi % n_blk
            variant = jnp.where(bi == 0, 2, jnp.where(n == 0, 1, 0))
            idx.append((base, kbase, variant, (bi // n_blk, n)))
        scores = []
        for base, kbase, variant, _ in idx:
            q = q_ref[0, pl.ds(base, BLK), :]
            q2 = jnp.concatenate([jnp.where(head0, q, zero_q), jnp.where(head0, zero_q, q)], axis=0)
            k = k_ref[0, pl.ds(kbase, 2 * BLK), :]
            bias = jnp.concatenate([bias_ref[p, 0, variant], bias_ref[p, 1, variant]], axis=0)
            scores.append(lax.dot_general(q2, k, (((1,), (1,)), ((), ())),
                                          preferred_element_type=F32) + bias)
        maxes, probs = [], []
        for s in scores:
            m = jnp.max(s, axis=-1, keepdims=True)
            maxes.append(m)
            probs.append(jnp.exp2(s - m).astype(BF16))
        for (base, kbase, variant, (res, n)), m, e in zip(idx, maxes, probs):
            v2 = jnp.concatenate([v_ref[0, pl.ds(kbase, 2 * BLK), :], ones_v], axis=1)
            pv = jnp.dot(e, v2, preferred_element_type=F32)
            store_part(p, r, res, n,
                       (jnp.where(head0, m[:BLK], m[BLK:]),
                        jnp.where(head0, pv[:BLK, LANES:], pv[BLK:, LANES:]),
                        jnp.where(head0, pv[:BLK, :LANES], pv[BLK:, :LANES])))

    for p, (window, r) in enumerate(DILATED_PATTERNS):
        n_blk = seq // (r * BLK)

        def group_body(i, carry, p=p, r=r, n_blk=n_blk):
            group(p, r, n_blk, i * ATT_GROUP)
            return carry
        lax.fori_loop(0, seq // (BLK * ATT_GROUP), group_body, 0)

    def out_body(i, carry):
        t0 = pl.multiple_of(i * tile, tile)
        rows = pl.ds(t0, tile)
        part = [[load_part(p, r, a, t0) for a in range(3)]
                for p, (_, r) in enumerate(DILATED_PATTERNS)]
        m = functools.reduce(jnp.maximum, [pt[0] for pt in part])
        w = [jnp.exp2(pt[0] - m) for pt in part]
        den = sum(wp * pt[1] for wp, pt in zip(w, part))
        num = sum(wp * pt[2] for wp, pt in zip(w, part))
        g = g_ref[0, rows, :].astype(F32)
        o_ref[0, rows, :] = (num / den * g).astype(o_ref.dtype)
        return carry
    lax.fori_loop(0, seq // tile, out_body, 0)


def _dilated_attn(att_perm, att_g, bias):
    bsz, seq, _ = att_g.shape
    n_pair = ATT_HEADS // 2
    n_pat = bias.shape[0]

    def col(off):
        return pl.BlockSpec((1, seq, LANES), lambda b, j: (b, 0, off + j))

    qkv_specs, qkv_args = [], []
    for a in att_perm:
        qkv_specs += [col(0), col(n_pair), col(2 * n_pair)]
        qkv_args += [a, a, a]
    return pl.pallas_call(
        functools.partial(_att_kernel, seq=seq),
        grid=(bsz, n_pair),
        in_specs=qkv_specs
        + [col(0),
           pl.BlockSpec((n_pat, 2, N_BIAS_VARIANTS, BLK, 2 * BLK), lambda b, j: (0, j, 0, 0, 0))],
        out_specs=col(0),
        out_shape=jax.ShapeDtypeStruct((bsz, seq, D_ATT), BF16),
        scratch_shapes=[pltpu.VMEM((3, _part_rows(seq, r), LANES), F32)
                        for _, r in DILATED_PATTERNS],
        compiler_params=pltpu.CompilerParams(
            dimension_semantics=("arbitrary", "arbitrary"), vmem_limit_bytes=VMEM_LIMIT),
        name="dilated_attn",
    )(*qkv_args, att_g, bias)


def _out_proj_kernel(x_ref, ydn_ref, yatt_ref, w_ref, o_ref):
    o_ref[...] = (x_ref[...]
                  + jnp.dot(ydn_ref[...], w_ref[0:D_DN, :], preferred_element_type=F32)
                  + jnp.dot(yatt_ref[...], w_ref[D_DN:D_MIX, :], preferred_element_type=F32))


def _out_proj(x2d, y_dn, y_att, w_out, tm=2048):
    n = x2d.shape[0]
    row = lambda i: (i, 0)
    return pl.pallas_call(
        _out_proj_kernel,
        grid=(n // tm,),
        in_specs=[pl.BlockSpec((tm, D_MODEL), row),
                  pl.BlockSpec((tm, D_DN), row),
                  pl.BlockSpec((tm, D_ATT), row),
                  pl.BlockSpec((D_MIX, D_MODEL), lambda i: (0, 0))],
        out_specs=pl.BlockSpec((tm, D_MODEL), row),
        out_shape=jax.ShapeDtypeStruct((n, D_MODEL), F32),
        compiler_params=pltpu.CompilerParams(
            dimension_semantics=("arbitrary",), vmem_limit_bytes=VMEM_LIMIT),
        name="out_proj",
    )(x2d, y_dn, y_att, w_out)


def _gate_rows(v):
    col = jnp.pad(v.reshape(-1, 1).astype(F32), ((DN_HEADS, SUBLANES - 2 * DN_HEADS), (0, 0)))
    return jnp.broadcast_to(col, (SUBLANES, LANES))


def _layer(x, norm_w, w_in, conv_w, a_log, dt_bias, dn_norm_w, q_norm_w, k_norm_w, bias, w_out):
    bsz, seq, _ = x.shape
    x2d = x.reshape(bsz * seq, D_MODEL)
    c_dn = 4 * D_DN
    c_ba = c_dn + 2 * DN_HEADS
    w_ba = jnp.pad(w_in[:, c_dn:c_ba], ((0, 0), (0, LANES - 2 * DN_HEADS)))
    w_all = jnp.concatenate([w_in[:, :c_dn], w_in[:, c_ba:], w_ba], axis=1).astype(BF16)

    qw = jnp.tile(q_norm_w.reshape(1, ATT_HEAD_DIM).astype(F32), (1, 2)) * (ATT_HEAD_DIM ** -0.5 * LOG2E)
    kw = jnp.tile(k_norm_w.reshape(1, ATT_HEAD_DIM).astype(F32), (1, 2))
    dnw = jnp.tile(dn_norm_w.reshape(1, DN_HEAD_DIM).astype(F32), (1, DN_HEADS))
    dn_qkv, dn_z, ba, att_g, *att_perm = _in_proj(x, norm_w.reshape(1, D_MODEL), w_all, qw, kw, dnw)
    att_perm = [a.reshape(bsz, seq, 3 * D_ATT) for a in att_perm]

    y_dn = _deltanet(dn_qkv, dn_z, ba, conv_w, _gate_rows(a_log), _gate_rows(dt_bias))
    y_att = _dilated_attn(att_perm, att_g, bias)
    out = _out_proj(x2d, y_dn.reshape(bsz * seq, D_DN), y_att.reshape(bsz * seq, D_ATT),
                    w_out.astype(BF16))
    return out.reshape(bsz, seq, D_MODEL)


def kernel(x, norm_w, w_in, conv_w, a_log, dt_bias, dn_norm_w, q_norm_w, k_norm_w, rel_bias, w_out):
    bias = _bias_tables(rel_bias.astype(F32))
    for layer in range(norm_w.shape[0]):
        x = _layer(x, norm_w[layer], w_in[layer], conv_w[layer], a_log[layer], dt_bias[layer],
                   dn_norm_w[layer], q_norm_w[layer], k_norm_w[layer], bias, w_out[layer])
    return x
```

```python
import functools
import math

import numpy as np
import jax
import jax.numpy as jnp
from jax import lax
from jax.experimental import pallas as pl
from jax.experimental.pallas import tpu as pltpu

D_MODEL = 1024
D_DN = 512
DN_HEADS = 4
DN_HEAD_DIM = 128
CONV_WIDTH = 4
CHUNK = 64
D_ATT = 512
ATT_HEADS = 8
ATT_HEAD_DIM = 64
DILATED_PATTERNS = ((128, 1), (512, 4), (2048, 16))
N_BUCKETS = 32
MAX_DISTANCE = 2048
D_MIX = D_DN + D_ATT
EPS = 1e-6

LANES = 128
SUBLANES = 8
BLK = 128
ATT_GROUP = 8
SUPER = 128
DN_UNROLL = 1
DN_BATCH = 4
SUB = 8
NEG = -1e30
LOG2E = math.log2(math.e)
N_BIAS_VARIANTS = 3
VMEM_LIMIT = 56 * 1024 * 1024

F32 = jnp.float32
BF16 = jnp.bfloat16


def _mm(a, b):
    return jnp.dot(a.astype(BF16), b.astype(BF16), preferred_element_type=F32)


def _mm_nt(a, b):
    return lax.dot_general(a.astype(BF16), b.astype(BF16), (((1,), (1,)), ((), ())),
                           preferred_element_type=F32)


def _sigmoid(x):
    return 0.5 * jnp.tanh(0.5 * x) + 0.5


def _in_proj_kernel(x_ref, nw_ref, w_ref, qw_ref, kw_ref, dnw_ref,
                    dnqkv_ref, dnz_ref, ba_ref, attg_ref, *att_refs_and_scratch):
    att_refs = att_refs_and_scratch[:len(DILATED_PATTERNS)]
    att_s, perm_s = att_refs_and_scratch[len(DILATED_PATTERNS):]
    tm = x_ref.shape[1]
    x = x_ref[0]
    ms = jnp.mean(x * x, axis=-1, keepdims=True)
    h = (x * lax.rsqrt(ms + EPS) * nw_ref[...]).astype(BF16)
    c0 = 3 * D_DN
    c1 = c0 + D_DN
    c2 = c1 + 3 * D_ATT
    c3 = c2 + D_ATT

    att = jnp.dot(h, w_ref[:, c1:c2], preferred_element_type=F32)
    head0 = lax.broadcasted_iota(jnp.int32, (tm, LANES), 1) < ATT_HEAD_DIM
    n_slab = 3 * D_ATT // LANES
    for j in range(n_slab):
        xs = att[:, j * LANES:(j + 1) * LANES]
        if j < 2 * D_ATT // LANES:
            is_q = j < D_ATT // LANES
            x2 = xs * xs
            s0 = jnp.sum(jnp.where(head0, x2, 0.0), axis=-1, keepdims=True)
            s1 = jnp.sum(jnp.where(head0, 0.0, x2), axis=-1, keepdims=True)
            ss = jnp.where(head0, s0, s1)
            wn = qw_ref[...] if is_q else kw_ref[...]
            xs = xs * lax.rsqrt(ss * (1.0 / ATT_HEAD_DIM) + EPS) * wn
        att_s[j] = xs
    for j in range(n_slab):
        cols = slice(j * LANES, (j + 1) * LANES)
        src_ref, r_prev = att_s.at[j], 1
        for level, (o_ref, (_, r)) in enumerate(zip(att_refs, DILATED_PATTERNS)):
            step = r // r_prev
            rows_per = tm // r
            keep = level + 1 < len(DILATED_PATTERNS) and r > 1
            for res in range(r):
                if r == 1:
                    piece = src_ref[...]
                else:
                    start = (res % r_prev) * (tm // r_prev) + res // r_prev
                    piece = src_ref[pl.ds(start, rows_per, stride=step), :]
                o_ref[0, res, :, cols] = piece.astype(BF16)
                if keep:
                    perm_s[j, pl.ds(res * rows_per, rows_per), :] = piece
            if keep:
                src_ref, r_prev = perm_s.at[j], r

    dnqkv_ref[0] = jnp.dot(h, w_ref[:, 0:c0], preferred_element_type=F32)
    z = jnp.dot(h, w_ref[:, c0:c1], preferred_element_type=F32)
    dnz_ref[0] = (z * _sigmoid(z) * dnw_ref[...]).astype(BF16)
    gate = jnp.dot(h, w_ref[:, c2:c3], preferred_element_type=F32)
    attg_ref[0] = (gate * _sigmoid(gate)).astype(BF16)
    ba_ref[0] = jnp.dot(h, w_ref[:, c3:c3 + LANES], preferred_element_type=F32)


def _in_proj(x, norm_w, w_all, qw, kw, dnw, tm=512):
    bsz, seq, _ = x.shape
    ncol = w_all.shape[1]
    row = lambda b, i: (b, i, 0)
    fixed = lambda b, i: (0, 0)
    att_specs = [pl.BlockSpec((1, r, tm // r, 3 * D_ATT), lambda b, i: (b, 0, i, 0))
                 for _, r in DILATED_PATTERNS]
    att_shapes = [jax.ShapeDtypeStruct((bsz, r, seq // r, 3 * D_ATT), BF16)
                  for _, r in DILATED_PATTERNS]
    return pl.pallas_call(
        _in_proj_kernel,
        grid=(bsz, seq // tm),
        in_specs=[pl.BlockSpec((1, tm, D_MODEL), row),
                  pl.BlockSpec((1, D_MODEL), fixed),
                  pl.BlockSpec((D_MODEL, ncol), fixed),
                  pl.BlockSpec((1, LANES), fixed),
                  pl.BlockSpec((1, LANES), fixed),
                  pl.BlockSpec((1, D_DN), fixed)],
        out_specs=[pl.BlockSpec((1, tm, 3 * D_DN), row),
                   pl.BlockSpec((1, tm, D_DN), row),
                   pl.BlockSpec((1, tm, LANES), row),
                   pl.BlockSpec((1, tm, D_ATT), row)] + att_specs,
        out_shape=[jax.ShapeDtypeStruct((bsz, seq, 3 * D_DN), F32),
                   jax.ShapeDtypeStruct((bsz, seq, D_DN), BF16),
                   jax.ShapeDtypeStruct((bsz, seq, LANES), F32),
                   jax.ShapeDtypeStruct((bsz, seq, D_ATT), BF16)] + att_shapes,
        scratch_shapes=[pltpu.VMEM((3 * D_ATT // LANES, tm, LANES), F32),
                        pltpu.VMEM((3 * D_ATT // LANES, tm, LANES), F32)],
        compiler_params=pltpu.CompilerParams(
            dimension_semantics=("arbitrary", "arbitrary"), vmem_limit_bytes=VMEM_LIMIT),
        name="in_proj",
    )(x, norm_w, w_all, qw, kw, dnw)


def _t5_bucket(dist):
    max_exact = N_BUCKETS // 2
    d = np.maximum(dist, 1).astype(np.float64)
    large = max_exact + (np.log(d / max_exact) / math.log(MAX_DISTANCE / max_exact)
                         * (N_BUCKETS - max_exact)).astype(np.int32)
    large = np.minimum(large, N_BUCKETS - 1)
    return np.where(dist < max_exact, dist, large).astype(np.int32)


def _bucket_maps():
    maps = []
    for window, r in DILATED_PATTERNS:
        w_steps = window // r
        assert w_steps == BLK
        qi = np.arange(BLK)[:, None]
        kj = np.arange(2 * BLK)[None, :]
        step = qi - kj + BLK
        band = (step >= 0) & (step <= w_steps)
        buckets = _t5_bucket(np.clip(step, 0, None) * r)
        maps.append(np.where(band, buckets, -1).astype(np.int32))
    return np.stack(maps)


def _bias_kernel(rel_ref, bkt_ref, o_ref):
    col = lax.broadcasted_iota(jnp.int32, (BLK, 2 * BLK), 1)

    def head_body(h, carry):
        bkt = bkt_ref[0]
        acc = jnp.full(bkt.shape, NEG, F32)
        for b in range(N_BUCKETS):
            acc = jnp.where(bkt == b, rel_ref[h, b] * LOG2E, acc)
        o_ref[0, h, 0] = acc
        o_ref[0, h, 1] = jnp.where(col < BLK, NEG, acc)
        o_ref[0, h, 2] = jnp.concatenate([acc[:, BLK:], jnp.full((BLK, BLK), NEG, F32)], axis=1)
        return carry
    lax.fori_loop(0, ATT_HEADS, head_body, 0)


def _bias_tables(rel_bias):
    bkt = jnp.asarray(_bucket_maps())
    n_pat = bkt.shape[0]
    return pl.pallas_call(
        _bias_kernel,
        grid=(n_pat,),
        in_specs=[pl.BlockSpec(memory_space=pltpu.SMEM),
                  pl.BlockSpec((1, BLK, 2 * BLK), lambda p: (p, 0, 0))],
        out_specs=pl.BlockSpec((1, ATT_HEADS, N_BIAS_VARIANTS, BLK, 2 * BLK),
                               lambda p: (p, 0, 0, 0, 0)),
        out_shape=jax.ShapeDtypeStruct((n_pat, ATT_HEADS, N_BIAS_VARIANTS, BLK, 2 * BLK), F32),
        name="bias_tables",
    )(rel_bias, bkt)


def _dn_kernel(*refs, tile):
    nh = DN_HEADS
    n_stream = 3 * nh
    x_refs = refs[:n_stream]
    (cw_ref, ba_ref, z_ref, alog_ref, dtb_ref, cf_ref, cb_ref,
     o_ref, state_s, xs) = refs[n_stream:]
    n_chunk = SUPER // CHUNK
    heads = range(nh)
    lead = SUBLANES

    nb = DN_BATCH

    @pl.when(pl.program_id(1) == 0)
    def _():
        state_s[...] = jnp.zeros(state_s.shape, F32)
        xs[:, 0:lead, :] = jnp.zeros((nb * n_stream, lead, LANES), F32)

    def copy_body(i, carry):
        r0 = pl.multiple_of(i * SUPER, SUPER)
        for bb in range(nb):
            for j in range(n_stream):
                xs[bb * n_stream + j, pl.ds(r0 + lead, SUPER), :] = x_refs[j][bb, pl.ds(r0, SUPER), :]
        return carry
    lax.fori_loop(0, tile // SUPER, copy_body, 0)

    eye = lambda: cf_ref[0]
    neg_outside_incl = lambda: cf_ref[1]
    neg_in_strict = lambda: cf_ref[2]
    in_sub = lambda: cf_ref[3]
    cum_mat = lambda: cb_ref[0]
    chunk_cols = lambda c: cb_ref[1 + c]
    is_beta_row = lax.broadcasted_iota(jnp.int32, (SUBLANES, SUPER), 0) < nh

    a_coef = -LOG2E * jnp.exp(alog_ref[...])
    dt_b = dtb_ref[...]

    def conv_silu(bb, j, r0):
        def tap(s):
            w = 0.5 * cw_ref[CONV_WIDTH - 1 - s, j]
            return (xs[bb * n_stream + j, pl.ds(r0 + lead - s, SUPER), :]
                    * jnp.tile(w, (SUPER // SUBLANES, 1)))
        h = tap(0)
        for s in range(1, CONV_WIDTH):
            h = h + tap(s)
        return h * jnp.tanh(h) + h

    def l2n(x):
        return x * lax.rsqrt(jnp.sum(x * x, axis=-1, keepdims=True) + EPS)

    def lane_bcast(x, j):
        return jnp.broadcast_to(x[:, j:j + 1], x.shape)

    def row_bcast(x, i, n_rows):
        return jnp.broadcast_to(x[i:i + 1, :], (n_rows, x.shape[1]))

    def gate_rows(bb, r0):
        pre = ba_ref[bb, pl.ds(r0, SUPER), :].T[0:SUBLANES, :]
        xg = pre + dt_b
        g = a_coef * (jnp.maximum(xg, 0.0) + jnp.log(1.0 + jnp.exp(-jnp.abs(xg))))
        g_hi = g.astype(BF16)
        g_r = g - g_hi.astype(F32)
        g_mid = g_r.astype(BF16)
        g_lo = (g_r - g_mid.astype(F32)).astype(BF16)
        cs = jnp.dot(jnp.concatenate([g_hi, g_mid, g_lo], axis=0), cum_mat(),
                     preferred_element_type=F32)
        gc_r = cs[0:SUBLANES] + cs[SUBLANES:2 * SUBLANES] + cs[2 * SUBLANES:3 * SUBLANES]
        gates = jnp.where(is_beta_row, _sigmoid(pre), gc_r)
        gates_c = jnp.concatenate([gates, jnp.zeros((SUPER - SUBLANES, SUPER), F32)], axis=0).T
        return gc_r, gates_c

    def body(it, carry):
        r0s = [pl.multiple_of((it * DN_UNROLL + u) * SUPER, SUPER) for u in range(DN_UNROLL)]
        items = [(u, bb, h) for u in range(DN_UNROLL) for bb in range(nb) for h in heads]
        idx = range(len(items))
        gate = {(u, bb): gate_rows(bb, r0s[u]) for u in range(DN_UNROLL) for bb in range(nb)}

        q = [l2n(conv_silu(bb, h, r0s[u])) * (DN_HEAD_DIM ** -0.5) for u, bb, h in items]
        k = [l2n(conv_silu(bb, nh + h, r0s[u])) for u, bb, h in items]
        v = [conv_silu(bb, 2 * nh + h, r0s[u]) for u, bb, h in items]

        beta = [lane_bcast(gate[u, bb][1], h) for u, bb, h in items]
        gc = [lane_bcast(gate[u, bb][1], nh + h) for u, bb, h in items]
        decay = [jnp.exp2(gc[i] - row_bcast(gate[u, bb][0], nh + h, SUPER) + neg_outside_incl())
                 for i, (u, bb, h) in enumerate(items)]
        gc_last = [jnp.concatenate([row_bcast(gc[i], c * CHUNK + CHUNK - 1, CHUNK)
                                    for c in range(n_chunk)], axis=0) for i in idx]
        e_gc = [jnp.exp2(gc[i]) for i in idx]
        kb = [k[i] * beta[i] for i in idx]

        kq = [_mm_nt(jnp.concatenate([kb[i], q[i]], axis=0), k[i]) for i in idx]
        neg_a = [kq[i][:SUPER] * decay[i] * neg_in_strict() for i in idx]
        attn = [(kq[i][SUPER:] * decay[i]).astype(BF16) for i in idx]
        rhs = [jnp.concatenate([v[i] * beta[i], kb[i] * e_gc[i]], axis=1).astype(BF16) for i in idx]

        x1 = [neg_a[i] * in_sub() for i in idx]
        neg_l = [(neg_a[i] - x1[i]).astype(BF16) for i in idx]
        x1b = [x1[i].astype(BF16) for i in idx]
        x2 = [_mm(x1b[i], x1b[i]) for i in idx]
        x2b = [x2[i].astype(BF16) for i in idx]
        x4 = [_mm(x2b[i], x2b[i]) for i in idx]
        p1 = [eye() + x1[i] + x2[i] + _mm(x1b[i], x2b[i]) for i in idx]
        t_d = [p1[i] + _mm(p1[i], x4[i]) for i in idx]
        t_db = [t_d[i].astype(BF16) for i in idx]
        y1 = [_mm(t_db[i], neg_l[i]) for i in idx]
        td_rhs = [_mm(t_db[i], rhs[i]).astype(BF16) for i in idx]
        y1b = [y1[i].astype(BF16) for i in idx]
        y2 = [_mm(y1b[i], y1b[i]) for i in idx]
        y2b = [y2[i].astype(BF16) for i in idx]
        y4 = [_mm(y2b[i], y2b[i]) for i in idx]
        q1 = [eye() + y1[i] + y2[i] + _mm(y1b[i], y2b[i]) for i in idx]
        q2 = [q1[i] + _mm(q1[i], y4[i]) for i in idx]

        uw = [_mm(q2[i], td_rhs[i]).astype(BF16) for i in idx]
        aw = [_mm(attn[i], uw[i]) for i in idx]
        q_t = [(q[i] * e_gc[i] - aw[i][:, LANES:]).astype(BF16) for i in idx]
        kt_t = [(k[i] * jnp.exp2(gc_last[i] - gc[i])).T.astype(BF16) for i in idx]
        kw = [[_mm(kt_t[i] * chunk_cols(c), uw[i])
               for c in range(n_chunk)] for i in idx]

        chains = [(bb, h) for bb in range(nb) for h in heads]
        state = [state_s[bb * nh + h] for bb, h in chains]
        for u in range(DN_UNROLL):
            outs = [[] for _ in chains]
            for c in range(n_chunk):
                sl = slice(c * CHUNK, (c + 1) * CHUNK)
                s_bf = [st.astype(BF16) for st in state]
                for j in range(len(chains)):
                    i = u * len(chains) + j
                    outs[j].append(_mm(q_t[i][sl], s_bf[j]) + aw[i][sl, :LANES])
                for j in range(len(chains)):
                    i = u * len(chains) + j
                    g_last = jnp.exp2(row_bcast(gc[i], c * CHUNK + CHUNK - 1, LANES))
                    state[j] = (state[j] * g_last - _mm(kw[i][c][:, LANES:], s_bf[j])
                                + kw[i][c][:, :LANES])
            rows = pl.ds(r0s[u], SUPER)
            for j, (bb, h) in enumerate(chains):
                o = jnp.concatenate(outs[j], axis=0)
                ms = jnp.mean(o * o, axis=-1, keepdims=True)
                z = z_ref[bb, rows, h * LANES:(h + 1) * LANES].astype(F32)
                y = o * lax.rsqrt(ms + EPS) * z
                o_ref[bb, rows, h * LANES:(h + 1) * LANES] = y.astype(o_ref.dtype)
        for j, (bb, h) in enumerate(chains):
            state_s[bb * nh + h] = state[j]
        return carry

    lax.fori_loop(0, tile // (SUPER * DN_UNROLL), body, 0)
    xs[:, 0:lead, :] = xs[:, tile:tile + lead, :]


def _dn_constants():
    ri = np.arange(SUPER)[:, None]
    ci = np.arange(SUPER)[None, :]
    same_chunk = (ri // CHUNK) == (ci // CHUNK)
    cf = np.stack([
        (ri == ci).astype(np.float32),
        np.where(same_chunk & (ri >= ci), 0.0, NEG).astype(np.float32),
        -(same_chunk & (ri > ci)).astype(np.float32),
        ((ri // SUB) == (ci // SUB)).astype(np.float32)])
    cb = np.stack([(same_chunk & (ri <= ci)).astype(np.float32)]
                  + [np.broadcast_to((ci // CHUNK) == c, (SUPER, SUPER)).astype(np.float32)
                     for c in range(SUPER // CHUNK)])
    return jnp.asarray(cf), jnp.asarray(cb, dtype=BF16)


def _deltanet(dn_qkv, dn_z, ba, conv_w, a_log, dt_bias, tile=256):
    bsz, seq, _ = dn_qkv.shape
    assert bsz % DN_BATCH == 0
    n_stream = 3 * DN_HEADS
    seq_tile = lambda b, t: (b, t, 0)
    fixed = lambda b, t: (0, 0)
    cf, cb = _dn_constants()
    cw_tiles = jnp.broadcast_to(conv_w.astype(F32).reshape(CONV_WIDTH, n_stream, 1, LANES),
                                (CONV_WIDTH, n_stream, SUBLANES, LANES))

    def slab(j):
        return pl.BlockSpec((DN_BATCH, tile, LANES), lambda b, t: (b, t, j))

    def whole(a):
        return pl.BlockSpec(a.shape, lambda b, t: (0,) * a.ndim)

    return pl.pallas_call(
        functools.partial(_dn_kernel, tile=tile),
        grid=(bsz // DN_BATCH, seq // tile),
        in_specs=[slab(j) for j in range(n_stream)]
        + [whole(cw_tiles),
           pl.BlockSpec((DN_BATCH, tile, LANES), seq_tile),
           pl.BlockSpec((DN_BATCH, tile, D_DN), seq_tile),
           pl.BlockSpec((SUBLANES, LANES), fixed),
           pl.BlockSpec((SUBLANES, LANES), fixed),
           whole(cf),
           whole(cb)],
        out_specs=pl.BlockSpec((DN_BATCH, tile, D_DN), seq_tile),
        out_shape=jax.ShapeDtypeStruct((bsz, seq, D_DN), BF16),
        scratch_shapes=[pltpu.VMEM((DN_BATCH * DN_HEADS, DN_HEAD_DIM, DN_HEAD_DIM), F32),
                        pltpu.VMEM((DN_BATCH * n_stream, tile + SUBLANES, LANES), F32)],
        compiler_params=pltpu.CompilerParams(
            dimension_semantics=("arbitrary", "arbitrary"), vmem_limit_bytes=VMEM_LIMIT),
        name="deltanet",
    )(*([dn_qkv] * n_stream), cw_tiles, ba, dn_z, a_log, dt_bias, cf, cb)


def _part_pitch(seq, r):
    return seq // r + 1 if r >= 2 * SUBLANES else None


def _part_rows(seq, r):
    pitch = _part_pitch(seq, r)
    return seq if pitch is None else -(-(r * pitch) // SUBLANES) * SUBLANES


def _att_kernel(*refs, seq):
    n_pat = len(DILATED_PATTERNS)
    qkv_refs = [refs[3 * p:3 * p + 3] for p in range(n_pat)]
    g_ref, bias_ref, o_ref = refs[3 * n_pat:3 * n_pat + 3]
    part_s = refs[3 * n_pat + 3:]
    tile = 512
    lane = lax.broadcasted_iota(jnp.int32, (BLK, LANES), 1)
    head0 = lane < ATT_HEAD_DIM
    ones_v = jnp.ones((2 * BLK, LANES), BF16)
    zero_q = jnp.zeros((BLK, LANES), BF16)

    def store_part(p, r, res, n, vals):
        pitch = _part_pitch(seq, r)
        if pitch is None:
            tok0 = res + n * (BLK * r)
            rows = pl.ds(tok0, BLK) if r == 1 else pl.ds(tok0, BLK, stride=r)
        else:
            rows = pl.ds(res * pitch + n * BLK, BLK)
        for a, val in enumerate(vals):
            part_s[p][a, rows, :] = val

    def load_part(p, r, a, t0):
        pitch = _part_pitch(seq, r)
        if pitch is None:
            return part_s[p][a, pl.ds(t0, tile), :]
        m0 = t0 // r
        return jnp.concatenate([part_s[p][a, pl.ds(m0 + jj, r, stride=pitch), :]
                                for jj in range(tile // r)], axis=0)

    def group(p, r, n_blk, bi0):
        q_ref, k_ref, v_ref = qkv_refs[p]
        idx = []
        for g in range(ATT_GROUP):
            bi = bi0 + g
            base = pl.multiple_of(bi * BLK, BLK)
            kbase = pl.multiple_of(jnp.maximum(base - BLK, 0), BLK)
            n = bi % n_blk
            variant = jnp.where(bi == 0, 2, jnp.where(n == 0, 1, 0))
            idx.append((base, kbase, variant, (bi // n_blk, n)))
        scores = []
        for base, kbase, variant, _ in idx:
            q = q_ref[0, pl.ds(base, BLK), :]
            q2 = jnp.concatenate([jnp.where(head0, q, zero_q), jnp.where(head0, zero_q, q)], axis=0)
            k = k_ref[0, pl.ds(kbase, 2 * BLK), :]
            bias = jnp.concatenate([bias_ref[p, 0, variant], bias_ref[p, 1, variant]], axis=0)
            scores.append(lax.dot_general(q2, k, (((1,), (1,)), ((), ())),
                                          preferred_element_type=F32) + bias)
        maxes, probs = [], []
        for s in scores:
            m = jnp.max(s, axis=-1, keepdims=True)
            maxes.append(m)
            probs.append(jnp.exp2(s - m).astype(BF16))
        for (base, kbase, variant, (res, n)), m, e in zip(idx, maxes, probs):
            v2 = jnp.concatenate([v_ref[0, pl.ds(kbase, 2 * BLK), :], ones_v], axis=1)
            pv = jnp.dot(e, v2, preferred_element_type=F32)
            store_part(p, r, res, n,
                       (jnp.where(head0, m[:BLK], m[BLK:]),
                        jnp.where(head0, pv[:BLK, LANES:], pv[BLK:, LANES:]),
                        jnp.where(head0, pv[:BLK, :LANES], pv[BLK:, :LANES])))

    for p, (window, r) in enumerate(DILATED_PATTERNS):
        n_blk = seq // (r * BLK)

        def group_body(i, carry, p=p, r=r, n_blk=n_blk):
            group(p, r, n_blk, i * ATT_GROUP)
            return carry
        lax.fori_loop(0, seq // (BLK * ATT_GROUP), group_body, 0)

    def out_body(i, carry):
        t0 = pl.multiple_of(i * tile, tile)
        rows = pl.ds(t0, tile)
        part = [[load_part(p, r, a, t0) for a in range(3)]
                for p, (_, r) in enumerate(DILATED_PATTERNS)]
        m = functools.reduce(jnp.maximum, [pt[0] for pt in part])
        w = [jnp.exp2(pt[0] - m) for pt in part]
        den = sum(wp * pt[1] for wp, pt in zip(w, part))
        num = sum(wp * pt[2] for wp, pt in zip(w, part))
        g = g_ref[0, rows, :].astype(F32)
        o_ref[0, rows, :] = (num / den * g).astype(o_ref.dtype)
        return carry
    lax.fori_loop(0, seq // tile, out_body, 0)


def _dilated_attn(att_perm, att_g, bias):
    bsz, seq, _ = att_g.shape
    n_pair = ATT_HEADS // 2
    n_pat = bias.shape[0]

    def col(off):
        return pl.BlockSpec((1, seq, LANES), lambda b, j: (b, 0, off + j))

    qkv_specs, qkv_args = [], []
    for a in att_perm:
        qkv_specs += [col(0), col(n_pair), col(2 * n_pair)]
        qkv_args += [a, a, a]
    return pl.pallas_call(
        functools.partial(_att_kernel, seq=seq),
        grid=(bsz, n_pair),
        in_specs=qkv_specs
        + [col(0),
           pl.BlockSpec((n_pat, 2, N_BIAS_VARIANTS, BLK, 2 * BLK), lambda b, j: (0, j, 0, 0, 0))],
        out_specs=col(0),
        out_shape=jax.ShapeDtypeStruct((bsz, seq, D_ATT), BF16),
        scratch_shapes=[pltpu.VMEM((3, _part_rows(seq, r), LANES), F32)
                        for _, r in DILATED_PATTERNS],
        compiler_params=pltpu.CompilerParams(
            dimension_semantics=("arbitrary", "arbitrary"), vmem_limit_bytes=VMEM_LIMIT),
        name="dilated_attn",
    )(*qkv_args, att_g, bias)


def _out_proj_kernel(x_ref, ydn_ref, yatt_ref, w_ref, o_ref):
    o_ref[...] = (x_ref[...]
                  + jnp.dot(ydn_ref[...], w_ref[0:D_DN, :], preferred_element_type=F32)
                  + jnp.dot(yatt_ref[...], w_ref[D_DN:D_MIX, :], preferred_element_type=F32))


def _out_proj(x2d, y_dn, y_att, w_out, tm=1024):
    n = x2d.shape[0]
    row = lambda i: (i, 0)
    return pl.pallas_call(
        _out_proj_kernel,
        grid=(n // tm,),
        in_specs=[pl.BlockSpec((tm, D_MODEL), row),
                  pl.BlockSpec((tm, D_DN), row),
                  pl.BlockSpec((tm, D_ATT), row),
                  pl.BlockSpec((D_MIX, D_MODEL), lambda i: (0, 0))],
        out_specs=pl.BlockSpec((tm, D_MODEL), row),
        out_shape=jax.ShapeDtypeStruct((n, D_MODEL), F32),
        compiler_params=pltpu.CompilerParams(
            dimension_semantics=("arbitrary",), vmem_limit_bytes=VMEM_LIMIT),
        name="out_proj",
    )(x2d, y_dn, y_att, w_out)


def _gate_rows(v):
    col = jnp.pad(v.reshape(-1, 1).astype(F32), ((DN_HEADS, SUBLANES - 2 * DN_HEADS), (0, 0)))
    return jnp.broadcast_to(col, (SUBLANES, LANES))


def _layer(x, norm_w, w_in, conv_w, a_log, dt_bias, dn_norm_w, q_norm_w, k_norm_w, bias, w_out):
    bsz, seq, _ = x.shape
    x2d = x.reshape(bsz * seq, D_MODEL)
    c_dn = 4 * D_DN
    c_ba = c_dn + 2 * DN_HEADS
    w_ba = jnp.pad(w_in[:, c_dn:c_ba], ((0, 0), (0, LANES - 2 * DN_HEADS)))
    w_all = jnp.concatenate([w_in[:, :c_dn], w_in[:, c_ba:], w_ba], axis=1).astype(BF16)

    qw = jnp.tile(q_norm_w.reshape(1, ATT_HEAD_DIM).astype(F32), (1, 2)) * (ATT_HEAD_DIM ** -0.5 * LOG2E)
    kw = jnp.tile(k_norm_w.reshape(1, ATT_HEAD_DIM).astype(F32), (1, 2))
    dnw = jnp.tile(dn_norm_w.reshape(1, DN_HEAD_DIM).astype(F32), (1, DN_HEADS))
    dn_qkv, dn_z, ba, att_g, *att_perm = _in_proj(x, norm_w.reshape(1, D_MODEL), w_all, qw, kw, dnw)
    att_perm = [a.reshape(bsz, seq, 3 * D_ATT) for a in att_perm]

    y_dn = _deltanet(dn_qkv, dn_z, ba, conv_w, _gate_rows(a_log), _gate_rows(dt_bias))
    y_att = _dilated_attn(att_perm, att_g, bias)
    out = _out_proj(x2d, y_dn.reshape(bsz * seq, D_DN), y_att.reshape(bsz * seq, D_ATT),
                    w_out.astype(BF16))
    return out.reshape(bsz, seq, D_MODEL)


def kernel(x, norm_w, w_in, conv_w, a_log, dt_bias, dn_norm_w, q_norm_w, k_norm_w, rel_bias, w_out):
    bias = _bias_tables(rel_bias.astype(F32))
    for layer in range(norm_w.shape[0]):
        x = _layer(x, norm_w[layer], w_in[layer], conv_w[layer], a_log[layer], dt_bias[layer],
                   dn_norm_w[layer], q_norm_w[layer], k_norm_w[layer], bias, w_out[layer])
    return x
```

```python
import functools
import math

import numpy as np
import jax
import jax.numpy as jnp
from jax import lax
from jax.experimental import pallas as pl
from jax.experimental.pallas import tpu as pltpu

D_MODEL = 1024
D_DN = 512
DN_HEADS = 4
DN_HEAD_DIM = 128
CONV_WIDTH = 4
CHUNK = 64
D_ATT = 512
ATT_HEADS = 8
ATT_HEAD_DIM = 64
DILATED_PATTERNS = ((128, 1), (512, 4), (2048, 16))
N_BUCKETS = 32
MAX_DISTANCE = 2048
D_MIX = D_DN + D_ATT
EPS = 1e-6

LANES = 128
SUBLANES = 8
BLK = 128
ATT_GROUP = 8
SUPER = 128
DN_UNROLL = 1
DN_BATCH = 4
SUB = 8
NEG = -1e30
LOG2E = math.log2(math.e)
N_BIAS_VARIANTS = 3
VMEM_LIMIT = 56 * 1024 * 1024

F32 = jnp.float32
BF16 = jnp.bfloat16


def _mm(a, b):
    return jnp.dot(a.astype(BF16), b.astype(BF16), preferred_element_type=F32)


def _mm_nt(a, b):
    return lax.dot_general(a.astype(BF16), b.astype(BF16), (((1,), (1,)), ((), ())),
                           preferred_element_type=F32)


def _sigmoid(x):
    return 0.5 * jnp.tanh(0.5 * x) + 0.5


def _in_proj_kernel(x_ref, nw_ref, w_ref, qw_ref, kw_ref, dnw_ref,
                    dnqkv_ref, dnz_ref, ba_ref, attg_ref, *att_refs_and_scratch):
    att_refs = att_refs_and_scratch[:len(DILATED_PATTERNS)]
    att_s, perm_s = att_refs_and_scratch[len(DILATED_PATTERNS):]
    tm = x_ref.shape[1]
    x = x_ref[0]
    ms = jnp.mean(x * x, axis=-1, keepdims=True)
    h = (x * lax.rsqrt(ms + EPS) * nw_ref[...]).astype(BF16)
    c0 = 3 * D_DN
    c1 = c0 + D_DN
    c2 = c1 + 3 * D_ATT
    c3 = c2 + D_ATT

    att = jnp.dot(h, w_ref[:, c1:c2], preferred_element_type=F32)
    head0 = lax.broadcasted_iota(jnp.int32, (tm, LANES), 1) < ATT_HEAD_DIM
    n_slab = 3 * D_ATT // LANES
    for j in range(n_slab):
        xs = att[:, j * LANES:(j + 1) * LANES]
        if j < 2 * D_ATT // LANES:
            is_q = j < D_ATT // LANES
            x2 = xs * xs
            s0 = jnp.sum(jnp.where(head0, x2, 0.0), axis=-1, keepdims=True)
            s1 = jnp.sum(jnp.where(head0, 0.0, x2), axis=-1, keepdims=True)
            ss = jnp.where(head0, s0, s1)
            wn = qw_ref[...] if is_q else kw_ref[...]
            xs = xs * lax.rsqrt(ss * (1.0 / ATT_HEAD_DIM) + EPS) * wn
        att_s[j] = xs
    for j in range(n_slab):
        cols = slice(j * LANES, (j + 1) * LANES)
        src_ref, r_prev = att_s.at[j], 1
        for level, (o_ref, (_, r)) in enumerate(zip(att_refs, DILATED_PATTERNS)):
            step = r // r_prev
            rows_per = tm // r
            keep = level + 1 < len(DILATED_PATTERNS) and r > 1
            for res in range(r):
                if r == 1:
                    piece = src_ref[...]
                else:
                    start = (res % r_prev) * (tm // r_prev) + res // r_prev
                    piece = src_ref[pl.ds(start, rows_per, stride=step), :]
                o_ref[0, res, :, cols] = piece.astype(BF16)
                if keep:
                    perm_s[j, pl.ds(res * rows_per, rows_per), :] = piece
            if keep:
                src_ref, r_prev = perm_s.at[j], r

    dnqkv_ref[0] = jnp.dot(h, w_ref[:, 0:c0], preferred_element_type=F32).astype(BF16)
    z = jnp.dot(h, w_ref[:, c0:c1], preferred_element_type=F32)
    dnz_ref[0] = (z * _sigmoid(z) * dnw_ref[...]).astype(BF16)
    gate = jnp.dot(h, w_ref[:, c2:c3], preferred_element_type=F32)
    attg_ref[0] = (gate * _sigmoid(gate)).astype(BF16)
    ba_ref[0] = jnp.dot(h, w_ref[:, c3:c3 + LANES], preferred_element_type=F32)


def _in_proj(x, norm_w, w_all, qw, kw, dnw, tm=512):
    bsz, seq, _ = x.shape
    ncol = w_all.shape[1]
    row = lambda b, i: (b, i, 0)
    fixed = lambda b, i: (0, 0)
    att_specs = [pl.BlockSpec((1, r, tm // r, 3 * D_ATT), lambda b, i: (b, 0, i, 0))
                 for _, r in DILATED_PATTERNS]
    att_shapes = [jax.ShapeDtypeStruct((bsz, r, seq // r, 3 * D_ATT), BF16)
                  for _, r in DILATED_PATTERNS]
    return pl.pallas_call(
        _in_proj_kernel,
        grid=(bsz, seq // tm),
        in_specs=[pl.BlockSpec((1, tm, D_MODEL), row),
                  pl.BlockSpec((1, D_MODEL), fixed),
                  pl.BlockSpec((D_MODEL, ncol), fixed),
                  pl.BlockSpec((1, LANES), fixed),
                  pl.BlockSpec((1, LANES), fixed),
                  pl.BlockSpec((1, D_DN), fixed)],
        out_specs=[pl.BlockSpec((1, tm, 3 * D_DN), row),
                   pl.BlockSpec((1, tm, D_DN), row),
                   pl.BlockSpec((1, tm, LANES), row),
                   pl.BlockSpec((1, tm, D_ATT), row)] + att_specs,
        out_shape=[jax.ShapeDtypeStruct((bsz, seq, 3 * D_DN), BF16),
                   jax.ShapeDtypeStruct((bsz, seq, D_DN), BF16),
                   jax.ShapeDtypeStruct((bsz, seq, LANES), F32),
                   jax.ShapeDtypeStruct((bsz, seq, D_ATT), BF16)] + att_shapes,
        scratch_shapes=[pltpu.VMEM((3 * D_ATT // LANES, tm, LANES), F32),
                        pltpu.VMEM((3 * D_ATT // LANES, tm, LANES), F32)],
        compiler_params=pltpu.CompilerParams(
            dimension_semantics=("arbitrary", "arbitrary"), vmem_limit_bytes=VMEM_LIMIT),
        name="in_proj",
    )(x, norm_w, w_all, qw, kw, dnw)


def _t5_bucket(dist):
    max_exact = N_BUCKETS // 2
    d = np.maximum(dist, 1).astype(np.float64)
    large = max_exact + (np.log(d / max_exact) / math.log(MAX_DISTANCE / max_exact)
                         * (N_BUCKETS - max_exact)).astype(np.int32)
    large = np.minimum(large, N_BUCKETS - 1)
    return np.where(dist < max_exact, dist, large).astype(np.int32)


def _bucket_maps():
    maps = []
    for window, r in DILATED_PATTERNS:
        w_steps = window // r
        assert w_steps == BLK
        qi = np.arange(BLK)[:, None]
        kj = np.arange(2 * BLK)[None, :]
        step = qi - kj + BLK
        band = (step >= 0) & (step <= w_steps)
        buckets = _t5_bucket(np.clip(step, 0, None) * r)
        maps.append(np.where(band, buckets, -1).astype(np.int32))
    return np.stack(maps)


def _bias_kernel(rel_ref, bkt_ref, o_ref):
    col = lax.broadcasted_iota(jnp.int32, (BLK, 2 * BLK), 1)

    def head_body(h, carry):
        bkt = bkt_ref[0]
        acc = jnp.full(bkt.shape, NEG, F32)
        for b in range(N_BUCKETS):
            acc = jnp.where(bkt == b, rel_ref[h, b] * LOG2E, acc)
        o_ref[0, h, 0] = acc
        o_ref[0, h, 1] = jnp.where(col < BLK, NEG, acc)
        o_ref[0, h, 2] = jnp.concatenate([acc[:, BLK:], jnp.full((BLK, BLK), NEG, F32)], axis=1)
        return carry
    lax.fori_loop(0, ATT_HEADS, head_body, 0)


def _bias_tables(rel_bias):
    bkt = jnp.asarray(_bucket_maps())
    n_pat = bkt.shape[0]
    return pl.pallas_call(
        _bias_kernel,
        grid=(n_pat,),
        in_specs=[pl.BlockSpec(memory_space=pltpu.SMEM),
                  pl.BlockSpec((1, BLK, 2 * BLK), lambda p: (p, 0, 0))],
        out_specs=pl.BlockSpec((1, ATT_HEADS, N_BIAS_VARIANTS, BLK, 2 * BLK),
                               lambda p: (p, 0, 0, 0, 0)),
        out_shape=jax.ShapeDtypeStruct((n_pat, ATT_HEADS, N_BIAS_VARIANTS, BLK, 2 * BLK), F32),
        name="bias_tables",
    )(rel_bias, bkt)


def _dn_kernel(*refs, tile):
    nh = DN_HEADS
    n_stream = 3 * nh
    x_refs = refs[:n_stream]
    (cw_ref, ba_ref, z_ref, alog_ref, dtb_ref, cf_ref, cb_ref,
     o_ref, state_s, xs) = refs[n_stream:]
    n_chunk = SUPER // CHUNK
    heads = range(nh)
    lead = SUBLANES

    nb = DN_BATCH

    @pl.when(pl.program_id(1) == 0)
    def _():
        state_s[...] = jnp.zeros(state_s.shape, F32)
        xs[:, 0:lead, :] = jnp.zeros((nb * n_stream, lead, LANES), F32)

    def copy_body(i, carry):
        r0 = pl.multiple_of(i * SUPER, SUPER)
        for bb in range(nb):
            for j in range(n_stream):
                xs[bb * n_stream + j, pl.ds(r0 + lead, SUPER), :] = (
                    x_refs[j][bb, pl.ds(r0, SUPER), :].astype(F32))
        return carry
    lax.fori_loop(0, tile // SUPER, copy_body, 0)

    eye = lambda: cf_ref[0]
    neg_outside_incl = lambda: cf_ref[1]
    neg_in_strict = lambda: cf_ref[2]
    in_sub = lambda: cf_ref[3]
    cum_mat = lambda: cb_ref[0]
    chunk_cols = lambda c: cb_ref[1 + c]
    is_beta_row = lax.broadcasted_iota(jnp.int32, (SUBLANES, SUPER), 0) < nh

    a_coef = -LOG2E * jnp.exp(alog_ref[...])
    dt_b = dtb_ref[...]

    def conv_silu(bb, j, r0):
        def tap(s):
            w = 0.5 * cw_ref[CONV_WIDTH - 1 - s, j]
            return (xs[bb * n_stream + j, pl.ds(r0 + lead - s, SUPER), :]
                    * jnp.tile(w, (SUPER // SUBLANES, 1)))
        h = tap(0)
        for s in range(1, CONV_WIDTH):
            h = h + tap(s)
        return h * jnp.tanh(h) + h

    def l2n(x):
        return x * lax.rsqrt(jnp.sum(x * x, axis=-1, keepdims=True) + EPS)

    def lane_bcast(x, j):
        return jnp.broadcast_to(x[:, j:j + 1], x.shape)

    def row_bcast(x, i, n_rows):
        return jnp.broadcast_to(x[i:i + 1, :], (n_rows, x.shape[1]))

    def gate_rows(bb, r0):
        pre = ba_ref[bb, pl.ds(r0, SUPER), :].T[0:SUBLANES, :]
        xg = pre + dt_b
        g = a_coef * (jnp.maximum(xg, 0.0) + jnp.log(1.0 + jnp.exp(-jnp.abs(xg))))
        g_hi = g.astype(BF16)
        g_r = g - g_hi.astype(F32)
        g_mid = g_r.astype(BF16)
        g_lo = (g_r - g_mid.astype(F32)).astype(BF16)
        cs = jnp.dot(jnp.concatenate([g_hi, g_mid, g_lo], axis=0), cum_mat(),
                     preferred_element_type=F32)
        gc_r = cs[0:SUBLANES] + cs[SUBLANES:2 * SUBLANES] + cs[2 * SUBLANES:3 * SUBLANES]
        gates = jnp.where(is_beta_row, _sigmoid(pre), gc_r)
        gates_c = jnp.concatenate([gates, jnp.zeros((SUPER - SUBLANES, SUPER), F32)], axis=0).T
        return gc_r, gates_c

    def body(it, carry):
        r0s = [pl.multiple_of((it * DN_UNROLL + u) * SUPER, SUPER) for u in range(DN_UNROLL)]
        items = [(u, bb, h) for u in range(DN_UNROLL) for bb in range(nb) for h in heads]
        idx = range(len(items))
        gate = {(u, bb): gate_rows(bb, r0s[u]) for u in range(DN_UNROLL) for bb in range(nb)}

        q = [l2n(conv_silu(bb, h, r0s[u])) * (DN_HEAD_DIM ** -0.5) for u, bb, h in items]
        k = [l2n(conv_silu(bb, nh + h, r0s[u])) for u, bb, h in items]
        v = [conv_silu(bb, 2 * nh + h, r0s[u]) for u, bb, h in items]

        beta = [lane_bcast(gate[u, bb][1], h) for u, bb, h in items]
        gc = [lane_bcast(gate[u, bb][1], nh + h) for u, bb, h in items]
        decay = [jnp.exp2(gc[i] - row_bcast(gate[u, bb][0], nh + h, SUPER) + neg_outside_incl())
                 for i, (u, bb, h) in enumerate(items)]
        gc_last = [jnp.concatenate([row_bcast(gc[i], c * CHUNK + CHUNK - 1, CHUNK)
                                    for c in range(n_chunk)], axis=0) for i in idx]
        e_gc = [jnp.exp2(gc[i]) for i in idx]
        kb = [k[i] * beta[i] for i in idx]

        kq = [_mm_nt(jnp.concatenate([kb[i], q[i]], axis=0), k[i]) for i in idx]
        neg_a = [kq[i][:SUPER] * decay[i] * neg_in_strict() for i in idx]
        attn = [(kq[i][SUPER:] * decay[i]).astype(BF16) for i in idx]
        rhs = [jnp.concatenate([v[i] * beta[i], kb[i] * e_gc[i]], axis=1).astype(BF16) for i in idx]

        x1 = [neg_a[i] * in_sub() for i in idx]
        neg_l = [(neg_a[i] - x1[i]).astype(BF16) for i in idx]
        x1b = [x1[i].astype(BF16) for i in idx]
        x2 = [_mm(x1b[i], x1b[i]) for i in idx]
        x2b = [x2[i].astype(BF16) for i in idx]
        x4 = [_mm(x2b[i], x2b[i]) for i in idx]
        p1 = [eye() + x1[i] + x2[i] + _mm(x1b[i], x2b[i]) for i in idx]
        t_d = [p1[i] + _mm(p1[i], x4[i]) for i in idx]
        t_db = [t_d[i].astype(BF16) for i in idx]
        y1 = [_mm(t_db[i], neg_l[i]) for i in idx]
        td_rhs = [_mm(t_db[i], rhs[i]).astype(BF16) for i in idx]
        y1b = [y1[i].astype(BF16) for i in idx]
        y2 = [_mm(y1b[i], y1b[i]) for i in idx]
        y2b = [y2[i].astype(BF16) for i in idx]
        y4 = [_mm(y2b[i], y2b[i]) for i in idx]
        q1 = [eye() + y1[i] + y2[i] + _mm(y1b[i], y2b[i]) for i in idx]
        q2 = [q1[i] + _mm(q1[i], y4[i]) for i in idx]

        uw = [_mm(q2[i], td_rhs[i]).astype(BF16) for i in idx]
        aw = [_mm(attn[i], uw[i]) for i in idx]
        q_t = [(q[i] * e_gc[i] - aw[i][:, LANES:]).astype(BF16) for i in idx]
        kt_t = [(k[i] * jnp.exp2(gc_last[i] - gc[i])).T.astype(BF16) for i in idx]
        kw = [[_mm(kt_t[i] * chunk_cols(c), uw[i])
               for c in range(n_chunk)] for i in idx]

        chains = [(bb, h) for bb in range(nb) for h in heads]
        state = [state_s[bb * nh + h] for bb, h in chains]
        for u in range(DN_UNROLL):
            outs = [[] for _ in chains]
            for c in range(n_chunk):
                sl = slice(c * CHUNK, (c + 1) * CHUNK)
                s_bf = [st.astype(BF16) for st in state]
                for j in range(len(chains)):
                    i = u * len(chains) + j
                    outs[j].append(_mm(q_t[i][sl], s_bf[j]) + aw[i][sl, :LANES])
                for j in range(len(chains)):
                    i = u * len(chains) + j
                    g_last = jnp.exp2(row_bcast(gc[i], c * CHUNK + CHUNK - 1, LANES))
                    state[j] = (state[j] * g_last - _mm(kw[i][c][:, LANES:], s_bf[j])
                                + kw[i][c][:, :LANES])
            rows = pl.ds(r0s[u], SUPER)
            for j, (bb, h) in enumerate(chains):
                o = jnp.concatenate(outs[j], axis=0)
                ms = jnp.mean(o * o, axis=-1, keepdims=True)
                z = z_ref[bb, rows, h * LANES:(h + 1) * LANES].astype(F32)
                y = o * lax.rsqrt(ms + EPS) * z
                o_ref[bb, rows, h * LANES:(h + 1) * LANES] = y.astype(o_ref.dtype)
        for j, (bb, h) in enumerate(chains):
            state_s[bb * nh + h] = state[j]
        return carry

    lax.fori_loop(0, tile // (SUPER * DN_UNROLL), body, 0)
    xs[:, 0:lead, :] = xs[:, tile:tile + lead, :]


def _dn_constants():
    ri = np.arange(SUPER)[:, None]
    ci = np.arange(SUPER)[None, :]
    same_chunk = (ri // CHUNK) == (ci // CHUNK)
    cf = np.stack([
        (ri == ci).astype(np.float32),
        np.where(same_chunk & (ri >= ci), 0.0, NEG).astype(np.float32),
        -(same_chunk & (ri > ci)).astype(np.float32),
        ((ri // SUB) == (ci // SUB)).astype(np.float32)])
    cb = np.stack([(same_chunk & (ri <= ci)).astype(np.float32)]
                  + [np.broadcast_to((ci // CHUNK) == c, (SUPER, SUPER)).astype(np.float32)
                     for c in range(SUPER // CHUNK)])
    return jnp.asarray(cf), jnp.asarray(cb, dtype=BF16)


def _deltanet(dn_qkv, dn_z, ba, conv_w, a_log, dt_bias, tile=256):
    bsz, seq, _ = dn_qkv.shape
    assert bsz % DN_BATCH == 0
    n_stream = 3 * DN_HEADS
    seq_tile = lambda b, t: (b, t, 0)
    fixed = lambda b, t: (0, 0)
    cf, cb = _dn_constants()
    cw_tiles = jnp.broadcast_to(conv_w.astype(F32).reshape(CONV_WIDTH, n_stream, 1, LANES),
                                (CONV_WIDTH, n_stream, SUBLANES, LANES))

    def slab(j):
        return pl.BlockSpec((DN_BATCH, tile, LANES), lambda b, t: (b, t, j))

    def whole(a):
        return pl.BlockSpec(a.shape, lambda b, t: (0,) * a.ndim)

    return pl.pallas_call(
        functools.partial(_dn_kernel, tile=tile),
        grid=(bsz // DN_BATCH, seq // tile),
        in_specs=[slab(j) for j in range(n_stream)]
        + [whole(cw_tiles),
           pl.BlockSpec((DN_BATCH, tile, LANES), seq_tile),
           pl.BlockSpec((DN_BATCH, tile, D_DN), seq_tile),
           pl.BlockSpec((SUBLANES, LANES), fixed),
           pl.BlockSpec((SUBLANES, LANES), fixed),
           whole(cf),
           whole(cb)],
        out_specs=pl.BlockSpec((DN_BATCH, tile, D_DN), seq_tile),
        out_shape=jax.ShapeDtypeStruct((bsz, seq, D_DN), BF16),
        scratch_shapes=[pltpu.VMEM((DN_BATCH * DN_HEADS, DN_HEAD_DIM, DN_HEAD_DIM), F32),
                        pltpu.VMEM((DN_BATCH * n_stream, tile + SUBLANES, LANES), F32)],
        compiler_params=pltpu.CompilerParams(
            dimension_semantics=("arbitrary", "arbitrary"), vmem_limit_bytes=VMEM_LIMIT),
        name="deltanet",
    )(*([dn_qkv] * n_stream), cw_tiles, ba, dn_z, a_log, dt_bias, cf, cb)


def _part_pitch(seq, r):
    return seq // r + 1 if r >= 2 * SUBLANES else None


def _part_rows(seq, r):
    pitch = _part_pitch(seq, r)
    return seq if pitch is None else -(-(r * pitch) // SUBLANES) * SUBLANES


def _att_kernel(*refs, seq):
    n_pat = len(DILATED_PATTERNS)
    qkv_refs = [refs[3 * p:3 * p + 3] for p in range(n_pat)]
    g_ref, bias_ref, o_ref = refs[3 * n_pat:3 * n_pat + 3]
    part_s = refs[3 * n_pat + 3:]
    tile = 512
    lane = lax.broadcasted_iota(jnp.int32, (BLK, LANES), 1)
    head0 = lane < ATT_HEAD_DIM
    ones_v = jnp.ones((2 * BLK, LANES), BF16)
    zero_q = jnp.zeros((BLK, LANES), BF16)

    def store_part(p, r, res, n, vals):
        pitch = _part_pitch(seq, r)
        if pitch is None:
            tok0 = res + n * (BLK * r)
            rows = pl.ds(tok0, BLK) if r == 1 else pl.ds(tok0, BLK, stride=r)
        else:
            rows = pl.ds(res * pitch + n * BLK, BLK)
        for a, val in enumerate(vals):
            part_s[p][a, rows, :] = val

    def load_part(p, r, a, t0):
        pitch = _part_pitch(seq, r)
        if pitch is None:
            return part_s[p][a, pl.ds(t0, tile), :]
        m0 = t0 // r
        return jnp.concatenate([part_s[p][a, pl.ds(m0 + jj, r, stride=pitch), :]
                                for jj in range(tile // r)], axis=0)

    def group(p, r, n_blk, bi0):
        q_ref, k_ref, v_ref = qkv_refs[p]
        idx = []
        for g in range(ATT_GROUP):
            bi = bi0 + g
            base = pl.multiple_of(bi * BLK, BLK)
            kbase = pl.multiple_of(jnp.maximum(base - BLK, 0), BLK)
            n = bi % n_blk
            variant = jnp.where(bi == 0, 2, jnp.where(n == 0, 1, 0))
            idx.append((base, kbase, variant, (bi // n_blk, n)))
        scores = []
        for base, kbase, variant, _ in idx:
            q = q_ref[0, pl.ds(base, BLK), :]
            q2 = jnp.concatenate([jnp.where(head0, q, zero_q), jnp.where(head0, zero_q, q)], axis=0)
            k = k_ref[0, pl.ds(kbase, 2 * BLK), :]
            bias = jnp.concatenate([bias_ref[p, 0, variant], bias_ref[p, 1, variant]], axis=0)
            scores.append(lax.dot_general(q2, k, (((1,), (1,)), ((), ())),
                                          preferred_element_type=F32) + bias)
        maxes, probs = [], []
        for s in scores:
            m = jnp.max(s, axis=-1, keepdims=True)
            maxes.append(m)
            probs.append(jnp.exp2(s - m).astype(BF16))
        for (base, kbase, variant, (res, n)), m, e in zip(idx, maxes, probs):
            v2 = jnp.concatenate([v_ref[0, pl.ds(kbase, 2 * BLK), :], ones_v], axis=1)
            pv = jnp.dot(e, v2, preferred_element_type=F32)
            store_part(p, r, res, n,
                       (jnp.where(head0, m[:BLK], m[BLK:]),
                        jnp.where(head0, pv[:BLK, LANES:], pv[BLK:, LANES:]),
                        jnp.where(head0, pv[:BLK, :LANES], pv[BLK:, :LANES])))

    for p, (window, r) in enumerate(DILATED_PATTERNS):
        n_blk = seq // (r * BLK)

        def group_body(i, carry, p=p, r=r, n_blk=n_blk):
            group(p, r, n_blk, i * ATT_GROUP)
            return carry
        lax.fori_loop(0, seq // (BLK * ATT_GROUP), group_body, 0)

    def out_body(i, carry):
        t0 = pl.multiple_of(i * tile, tile)
        rows = pl.ds(t0, tile)
        part = [[load_part(p, r, a, t0) for a in range(3)]
                for p, (_, r) in enumerate(DILATED_PATTERNS)]
        m = functools.reduce(jnp.maximum, [pt[0] for pt in part])
        w = [jnp.exp2(pt[0] - m) for pt in part]
        den = sum(wp * pt[1] for wp, pt in zip(w, part))
        num = sum(wp * pt[2] for wp, pt in zip(w, part))
        g = g_ref[0, rows, :].astype(F32)
        o_ref[0, rows, :] = (num / den * g).astype(o_ref.dtype)
        return carry
    lax.fori_loop(0, seq // tile, out_body, 0)


def _dilated_attn(att_perm, att_g, bias):
    bsz, seq, _ = att_g.shape
    n_pair = ATT_HEADS // 2
    n_pat = bias.shape[0]

    def col(off):
        return pl.BlockSpec((1, seq, LANES), lambda b, j: (b, 0, off + j))

    qkv_specs, qkv_args = [], []
    for a in att_perm:
        qkv_specs += [col(0), col(n_pair), col(2 * n_pair)]
        qkv_args += [a, a, a]
    return pl.pallas_call(
        functools.partial(_att_kernel, seq=seq),
        grid=(bsz, n_pair),
        in_specs=qkv_specs
        + [col(0),
           pl.BlockSpec((n_pat, 2, N_BIAS_VARIANTS, BLK, 2 * BLK), lambda b, j: (0, j, 0, 0, 0))],
        out_specs=col(0),
        out_shape=jax.ShapeDtypeStruct((bsz, seq, D_ATT), BF16),
        scratch_shapes=[pltpu.VMEM((3, _part_rows(seq, r), LANES), F32)
                        for _, r in DILATED_PATTERNS],
        compiler_params=pltpu.CompilerParams(
            dimension_semantics=("arbitrary", "arbitrary"), vmem_limit_bytes=VMEM_LIMIT),
        name="dilated_attn",
    )(*qkv_args, att_g, bias)


def _out_proj_kernel(x_ref, ydn_ref, yatt_ref, w_ref, o_ref):
    o_ref[...] = (x_ref[...]
                  + jnp.dot(ydn_ref[...], w_ref[0:D_DN, :], preferred_element_type=F32)
                  + jnp.dot(yatt_ref[...], w_ref[D_DN:D_MIX, :], preferred_element_type=F32))


def _out_proj(x2d, y_dn, y_att, w_out, tm=1024):
    n = x2d.shape[0]
    row = lambda i: (i, 0)
    return pl.pallas_call(
        _out_proj_kernel,
        grid=(n // tm,),
        in_specs=[pl.BlockSpec((tm, D_MODEL), row),
                  pl.BlockSpec((tm, D_DN), row),
                  pl.BlockSpec((tm, D_ATT), row),
                  pl.BlockSpec((D_MIX, D_MODEL), lambda i: (0, 0))],
        out_specs=pl.BlockSpec((tm, D_MODEL), row),
        out_shape=jax.ShapeDtypeStruct((n, D_MODEL), F32),
        compiler_params=pltpu.CompilerParams(
            dimension_semantics=("arbitrary",), vmem_limit_bytes=VMEM_LIMIT),
        name="out_proj",
    )(x2d, y_dn, y_att, w_out)


def _gate_rows(v):
    col = jnp.pad(v.reshape(-1, 1).astype(F32), ((DN_HEADS, SUBLANES - 2 * DN_HEADS), (0, 0)))
    return jnp.broadcast_to(col, (SUBLANES, LANES))


def _layer(x, norm_w, w_in, conv_w, a_log, dt_bias, dn_norm_w, q_norm_w, k_norm_w, bias, w_out):
    bsz, seq, _ = x.shape
    x2d = x.reshape(bsz * seq, D_MODEL)
    c_dn = 4 * D_DN
    c_ba = c_dn + 2 * DN_HEADS
    w_ba = jnp.pad(w_in[:, c_dn:c_ba], ((0, 0), (0, LANES - 2 * DN_HEADS)))
    w_all = jnp.concatenate([w_in[:, :c_dn], w_in[:, c_ba:], w_ba], axis=1).astype(BF16)

    qw = jnp.tile(q_norm_w.reshape(1, ATT_HEAD_DIM).astype(F32), (1, 2)) * (ATT_HEAD_DIM ** -0.5 * LOG2E)
    kw = jnp.tile(k_norm_w.reshape(1, ATT_HEAD_DIM).astype(F32), (1, 2))
    dnw = jnp.tile(dn_norm_w.reshape(1, DN_HEAD_DIM).astype(F32), (1, DN_HEADS))
    dn_qkv, dn_z, ba, att_g, *att_perm = _in_proj(x, norm_w.reshape(1, D_MODEL), w_all, qw, kw, dnw)
    att_perm = [a.reshape(bsz, seq, 3 * D_ATT) for a in att_perm]

    y_dn = _deltanet(dn_qkv, dn_z, ba, conv_w, _gate_rows(a_log), _gate_rows(dt_bias))
    y_att = _dilated_attn(att_perm, att_g, bias)
    out = _out_proj(x2d, y_dn.reshape(bsz * seq, D_DN), y_att.reshape(bsz * seq, D_ATT),
                    w_out.astype(BF16))
    return out.reshape(bsz, seq, D_MODEL)


def kernel(x, norm_w, w_in, conv_w, a_log, dt_bias, dn_norm_w, q_norm_w, k_norm_w, rel_bias, w_out):
    bias = _bias_tables(rel_bias.astype(F32))
    for layer in range(norm_w.shape[0]):
        x = _layer(x, norm_w[layer], w_in[layer], conv_w[layer], a_log[layer], dt_bias[layer],
                   dn_norm_w[layer], q_norm_w[layer], k_norm_w[layer], bias, w_out[layer])
    return x
```

```python
import functools
import math

import numpy as np
import jax
import jax.numpy as jnp
from jax import lax
from jax.experimental import pallas as pl
from jax.experimental.pallas import tpu as pltpu

D_MODEL = 1024
D_DN = 512
DN_HEADS = 4
DN_HEAD_DIM = 128
CONV_WIDTH = 4
CHUNK = 64
D_ATT = 512
ATT_HEADS = 8
ATT_HEAD_DIM = 64
DILATED_PATTERNS = ((128, 1), (512, 4), (2048, 16))
N_BUCKETS = 32
MAX_DISTANCE = 2048
D_MIX = D_DN + D_ATT
EPS = 1e-6

LANES = 128
SUBLANES = 8
BLK = 128
ATT_GROUP = 8
SUPER = 128
DN_UNROLL = 1
DN_BATCH = 4
SUB = 8
NEG = -1e30
LOG2E = math.log2(math.e)
N_BIAS_VARIANTS = 3
IN_PROJ_ROWS = 512
DN_TILE = 256
ATT_MERGE_ROWS = 1024
OUT_PROJ_ROWS = 1024
V7X_VMEM_BYTES = 64 * 1024 * 1024
VMEM_LIMIT = V7X_VMEM_BYTES * 7 // 8

F32 = jnp.float32
BF16 = jnp.bfloat16


def _mm(a, b):
    return jnp.dot(a.astype(BF16), b.astype(BF16), preferred_element_type=F32)


def _mm_nt(a, b):
    return lax.dot_general(a.astype(BF16), b.astype(BF16), (((1,), (1,)), ((), ())),
                           preferred_element_type=F32)


def _sigmoid(x):
    return 0.5 * jnp.tanh(0.5 * x) + 0.5


def _in_proj_kernel(x_ref, nw_ref, w_ref, qw_ref, kw_ref, dnw_ref,
                    dnqkv_ref, dnz_ref, ba_ref, attg_ref, *att_refs_and_scratch):
    att_refs = att_refs_and_scratch[:len(DILATED_PATTERNS)]
    att_s, perm_s = att_refs_and_scratch[len(DILATED_PATTERNS):]
    tm = x_ref.shape[1]
    x = x_ref[0]
    ms = jnp.mean(x * x, axis=-1, keepdims=True)
    h = (x * lax.rsqrt(ms + EPS) * nw_ref[...]).astype(BF16)
    c0 = 3 * D_DN
    c1 = c0 + D_DN
    c2 = c1 + 3 * D_ATT
    c3 = c2 + D_ATT

    att = jnp.dot(h, w_ref[:, c1:c2], preferred_element_type=F32)
    head0 = lax.broadcasted_iota(jnp.int32, (tm, LANES), 1) < ATT_HEAD_DIM
    n_slab = 3 * D_ATT // LANES
    for j in range(n_slab):
        xs = att[:, j * LANES:(j + 1) * LANES]
        if j < 2 * D_ATT // LANES:
            is_q = j < D_ATT // LANES
            x2 = xs * xs
            s0 = jnp.sum(jnp.where(head0, x2, 0.0), axis=-1, keepdims=True)
            s1 = jnp.sum(jnp.where(head0, 0.0, x2), axis=-1, keepdims=True)
            ss = jnp.where(head0, s0, s1)
            wn = qw_ref[...] if is_q else kw_ref[...]
            xs = xs * lax.rsqrt(ss * (1.0 / ATT_HEAD_DIM) + EPS) * wn
        att_s[j] = xs
    for j in range(n_slab):
        cols = slice(j * LANES, (j + 1) * LANES)
        src_ref, r_prev = att_s.at[j], 1
        for level, (o_ref, (_, r)) in enumerate(zip(att_refs, DILATED_PATTERNS)):
            step = r // r_prev
            rows_per = tm // r
            keep = level + 1 < len(DILATED_PATTERNS) and r > 1
            for res in range(r):
                if r == 1:
                    piece = src_ref[...]
                else:
                    start = (res % r_prev) * (tm // r_prev) + res // r_prev
                    piece = src_ref[pl.ds(start, rows_per, stride=step), :]
                o_ref[0, res, :, cols] = piece.astype(BF16)
                if keep:
                    perm_s[j, pl.ds(res * rows_per, rows_per), :] = piece
            if keep:
                src_ref, r_prev = perm_s.at[j], r

    dnqkv_ref[0] = jnp.dot(h, w_ref[:, 0:c0], preferred_element_type=F32)
    z = jnp.dot(h, w_ref[:, c0:c1], preferred_element_type=F32)
    dnz_ref[0] = (z * _sigmoid(z) * dnw_ref[...]).astype(BF16)
    gate = jnp.dot(h, w_ref[:, c2:c3], preferred_element_type=F32)
    attg_ref[0] = (gate * _sigmoid(gate)).astype(BF16)
    ba_ref[0] = jnp.dot(h, w_ref[:, c3:c3 + LANES], preferred_element_type=F32)


def _in_proj(x, norm_w, w_all, qw, kw, dnw):
    bsz, seq, _ = x.shape
    tm = IN_PROJ_ROWS
    ncol = w_all.shape[1]
    row = lambda b, i: (b, i, 0)
    fixed = lambda b, i: (0, 0)
    att_specs = [pl.BlockSpec((1, r, tm // r, 3 * D_ATT), lambda b, i: (b, 0, i, 0))
                 for _, r in DILATED_PATTERNS]
    att_shapes = [jax.ShapeDtypeStruct((bsz, r, seq // r, 3 * D_ATT), BF16)
                  for _, r in DILATED_PATTERNS]
    return pl.pallas_call(
        _in_proj_kernel,
        grid=(bsz, seq // tm),
        in_specs=[pl.BlockSpec((1, tm, D_MODEL), row),
                  pl.BlockSpec((1, D_MODEL), fixed),
                  pl.BlockSpec((D_MODEL, ncol), fixed),
                  pl.BlockSpec((1, LANES), fixed),
                  pl.BlockSpec((1, LANES), fixed),
                  pl.BlockSpec((1, D_DN), fixed)],
        out_specs=[pl.BlockSpec((1, tm, 3 * D_DN), row),
                   pl.BlockSpec((1, tm, D_DN), row),
                   pl.BlockSpec((1, tm, LANES), row),
                   pl.BlockSpec((1, tm, D_ATT), row)] + att_specs,
        out_shape=[jax.ShapeDtypeStruct((bsz, seq, 3 * D_DN), F32),
                   jax.ShapeDtypeStruct((bsz, seq, D_DN), BF16),
                   jax.ShapeDtypeStruct((bsz, seq, LANES), F32),
                   jax.ShapeDtypeStruct((bsz, seq, D_ATT), BF16)] + att_shapes,
        scratch_shapes=[pltpu.VMEM((3 * D_ATT // LANES, tm, LANES), F32),
                        pltpu.VMEM((3 * D_ATT // LANES, tm, LANES), F32)],
        compiler_params=pltpu.CompilerParams(
            dimension_semantics=("arbitrary", "arbitrary"), vmem_limit_bytes=VMEM_LIMIT),
        name="in_proj",
    )(x, norm_w, w_all, qw, kw, dnw)


def _t5_bucket(dist):
    max_exact = N_BUCKETS // 2
    d = np.maximum(dist, 1).astype(np.float64)
    large = max_exact + (np.log(d / max_exact) / math.log(MAX_DISTANCE / max_exact)
                         * (N_BUCKETS - max_exact)).astype(np.int32)
    large = np.minimum(large, N_BUCKETS - 1)
    return np.where(dist < max_exact, dist, large).astype(np.int32)


def _bucket_maps():
    maps = []
    for window, r in DILATED_PATTERNS:
        w_steps = window // r
        assert w_steps == BLK
        qi = np.arange(BLK)[:, None]
        kj = np.arange(2 * BLK)[None, :]
        step = qi - kj + BLK
        band = (step >= 0) & (step <= w_steps)
        buckets = _t5_bucket(np.clip(step, 0, None) * r)
        maps.append(np.where(band, buckets, -1).astype(np.int32))
    return np.stack(maps)


def _bias_kernel(rel_ref, bkt_ref, o_ref):
    col = lax.broadcasted_iota(jnp.int32, (BLK, 2 * BLK), 1)

    def head_body(h, carry):
        bkt = bkt_ref[0]
        acc = jnp.full(bkt.shape, NEG, F32)
        for b in range(N_BUCKETS):
            acc = jnp.where(bkt == b, rel_ref[h, b] * LOG2E, acc)
        o_ref[0, h, 0] = acc
        o_ref[0, h, 1] = jnp.where(col < BLK, NEG, acc)
        o_ref[0, h, 2] = jnp.concatenate([acc[:, BLK:], jnp.full((BLK, BLK), NEG, F32)], axis=1)
        return carry
    lax.fori_loop(0, ATT_HEADS, head_body, 0)


def _bias_tables(rel_bias):
    bkt = jnp.asarray(_bucket_maps())
    n_pat = bkt.shape[0]
    return pl.pallas_call(
        _bias_kernel,
        grid=(n_pat,),
        in_specs=[pl.BlockSpec(memory_space=pltpu.SMEM),
                  pl.BlockSpec((1, BLK, 2 * BLK), lambda p: (p, 0, 0))],
        out_specs=pl.BlockSpec((1, ATT_HEADS, N_BIAS_VARIANTS, BLK, 2 * BLK),
                               lambda p: (p, 0, 0, 0, 0)),
        out_shape=jax.ShapeDtypeStruct((n_pat, ATT_HEADS, N_BIAS_VARIANTS, BLK, 2 * BLK), F32),
        name="bias_tables",
    )(rel_bias, bkt)


def _dn_kernel(*refs, tile):
    nh = DN_HEADS
    n_stream = 3 * nh
    x_refs = refs[:n_stream]
    (cw_ref, ba_ref, z_ref, alog_ref, dtb_ref, cf_ref, cb_ref,
     o_ref, state_s, xs) = refs[n_stream:]
    n_chunk = SUPER // CHUNK
    heads = range(nh)
    lead = SUBLANES

    nb = DN_BATCH

    @pl.when(pl.program_id(1) == 0)
    def _():
        state_s[...] = jnp.zeros(state_s.shape, F32)
        xs[:, 0:lead, :] = jnp.zeros((nb * n_stream, lead, LANES), F32)

    def copy_body(i, carry):
        r0 = pl.multiple_of(i * SUPER, SUPER)
        for bb in range(nb):
            for j in range(n_stream):
                xs[bb * n_stream + j, pl.ds(r0 + lead, SUPER), :] = x_refs[j][bb, pl.ds(r0, SUPER), :]
        return carry
    lax.fori_loop(0, tile // SUPER, copy_body, 0)

    eye = lambda: cf_ref[0]
    neg_outside_incl = lambda: cf_ref[1]
    neg_in_strict = lambda: cf_ref[2]
    in_sub = lambda: cf_ref[3]
    cum_mat = lambda: cb_ref[0]
    chunk_cols = lambda c: cb_ref[1 + c]
    is_beta_row = lax.broadcasted_iota(jnp.int32, (SUBLANES, SUPER), 0) < nh

    a_coef = -LOG2E * jnp.exp(alog_ref[...])
    dt_b = dtb_ref[...]

    def conv_silu(bb, j, r0):
        def tap(s):
            w = 0.5 * cw_ref[CONV_WIDTH - 1 - s, j]
            return (xs[bb * n_stream + j, pl.ds(r0 + lead - s, SUPER), :]
                    * jnp.tile(w, (SUPER // SUBLANES, 1)))
        h = tap(0)
        for s in range(1, CONV_WIDTH):
            h = h + tap(s)
        return h * jnp.tanh(h) + h

    def l2n(x):
        return x * lax.rsqrt(jnp.sum(x * x, axis=-1, keepdims=True) + EPS)

    def lane_bcast(x, j):
        return jnp.broadcast_to(x[:, j:j + 1], x.shape)

    def row_bcast(x, i, n_rows):
        return jnp.broadcast_to(x[i:i + 1, :], (n_rows, x.shape[1]))

    def gate_rows(bb, r0):
        pre = ba_ref[bb, pl.ds(r0, SUPER), :].T[0:SUBLANES, :]
        xg = pre + dt_b
        g = a_coef * (jnp.maximum(xg, 0.0) + jnp.log(1.0 + jnp.exp(-jnp.abs(xg))))
        g_hi = g.astype(BF16)
        g_r = g - g_hi.astype(F32)
        g_mid = g_r.astype(BF16)
        g_lo = (g_r - g_mid.astype(F32)).astype(BF16)
        cs = jnp.dot(jnp.concatenate([g_hi, g_mid, g_lo], axis=0), cum_mat(),
                     preferred_element_type=F32)
        gc_r = cs[0:SUBLANES] + cs[SUBLANES:2 * SUBLANES] + cs[2 * SUBLANES:3 * SUBLANES]
        gates = jnp.where(is_beta_row, _sigmoid(pre), gc_r)
        gates_c = jnp.concatenate([gates, jnp.zeros((SUPER - SUBLANES, SUPER), F32)], axis=0).T
        return gc_r, gates_c

    def body(it, carry):
        r0s = [pl.multiple_of((it * DN_UNROLL + u) * SUPER, SUPER) for u in range(DN_UNROLL)]
        items = [(u, bb, h) for u in range(DN_UNROLL) for bb in range(nb) for h in heads]
        idx = range(len(items))
        gate = {(u, bb): gate_rows(bb, r0s[u]) for u in range(DN_UNROLL) for bb in range(nb)}

        q = [l2n(conv_silu(bb, h, r0s[u])) * (DN_HEAD_DIM ** -0.5) for u, bb, h in items]
        k = [l2n(conv_silu(bb, nh + h, r0s[u])) for u, bb, h in items]
        v = [conv_silu(bb, 2 * nh + h, r0s[u]) for u, bb, h in items]

        beta = [lane_bcast(gate[u, bb][1], h) for u, bb, h in items]
        gc = [lane_bcast(gate[u, bb][1], nh + h) for u, bb, h in items]
        decay = [jnp.exp2(gc[i] - row_bcast(gate[u, bb][0], nh + h, SUPER) + neg_outside_incl())
                 for i, (u, bb, h) in enumerate(items)]
        gc_last = [jnp.concatenate([row_bcast(gc[i], c * CHUNK + CHUNK - 1, CHUNK)
                                    for c in range(n_chunk)], axis=0) for i in idx]
        e_gc = [jnp.exp2(gc[i]) for i in idx]
        kb = [k[i] * beta[i] for i in idx]

        kq = [_mm_nt(jnp.concatenate([kb[i], q[i]], axis=0), k[i]) for i in idx]
        neg_a = [kq[i][:SUPER] * decay[i] * neg_in_strict() for i in idx]
        attn = [(kq[i][SUPER:] * decay[i]).astype(BF16) for i in idx]
        rhs = [jnp.concatenate([v[i] * beta[i], kb[i] * e_gc[i]], axis=1).astype(BF16) for i in idx]

        x1 = [neg_a[i] * in_sub() for i in idx]
        neg_l = [(neg_a[i] - x1[i]).astype(BF16) for i in idx]
        x1b = [x1[i].astype(BF16) for i in idx]
        x2 = [_mm(x1b[i], x1b[i]) for i in idx]
        x2b = [x2[i].astype(BF16) for i in idx]
        x4 = [_mm(x2b[i], x2b[i]) for i in idx]
        p1 = [eye() + x1[i] + x2[i] + _mm(x1b[i], x2b[i]) for i in idx]
        t_d = [p1[i] + _mm(p1[i], x4[i]) for i in idx]
        t_db = [t_d[i].astype(BF16) for i in idx]
        y1 = [_mm(t_db[i], neg_l[i]) for i in idx]
        td_rhs = [_mm(t_db[i], rhs[i]).astype(BF16) for i in idx]
        y1b = [y1[i].astype(BF16) for i in idx]
        y2 = [_mm(y1b[i], y1b[i]) for i in idx]
        y2b = [y2[i].astype(BF16) for i in idx]
        y4 = [_mm(y2b[i], y2b[i]) for i in idx]
        q1 = [eye() + y1[i] + y2[i] + _mm(y1b[i], y2b[i]) for i in idx]
        q2 = [q1[i] + _mm(q1[i], y4[i]) for i in idx]

        uw = [_mm(q2[i], td_rhs[i]).astype(BF16) for i in idx]
        aw = [_mm(attn[i], uw[i]) for i in idx]
        q_t = [(q[i] * e_gc[i] - aw[i][:, LANES:]).astype(BF16) for i in idx]
        kt_t = [(k[i] * jnp.exp2(gc_last[i] - gc[i])).T.astype(BF16) for i in idx]
        kw = [[_mm(kt_t[i] * chunk_cols(c), uw[i])
               for c in range(n_chunk)] for i in idx]

        chains = [(bb, h) for bb in range(nb) for h in heads]
        state = [state_s[bb * nh + h] for bb, h in chains]
        for u in range(DN_UNROLL):
            outs = [[] for _ in chains]
            for c in range(n_chunk):
                sl = slice(c * CHUNK, (c + 1) * CHUNK)
                s_bf = [st.astype(BF16) for st in state]
                for j in range(len(chains)):
                    i = u * len(chains) + j
                    outs[j].append(_mm(q_t[i][sl], s_bf[j]) + aw[i][sl, :LANES])
                for j in range(len(chains)):
                    i = u * len(chains) + j
                    g_last = jnp.exp2(row_bcast(gc[i], c * CHUNK + CHUNK - 1, LANES))
                    state[j] = (state[j] * g_last - _mm(kw[i][c][:, LANES:], s_bf[j])
                                + kw[i][c][:, :LANES])
            rows = pl.ds(r0s[u], SUPER)
            for j, (bb, h) in enumerate(chains):
                o = jnp.concatenate(outs[j], axis=0)
                ms = jnp.mean(o * o, axis=-1, keepdims=True)
                z = z_ref[bb, rows, h * LANES:(h + 1) * LANES].astype(F32)
                y = o * lax.rsqrt(ms + EPS) * z
                o_ref[bb, rows, h * LANES:(h + 1) * LANES] = y.astype(o_ref.dtype)
        for j, (bb, h) in enumerate(chains):
            state_s[bb * nh + h] = state[j]
        return carry

    lax.fori_loop(0, tile // (SUPER * DN_UNROLL), body, 0)
    xs[:, 0:lead, :] = xs[:, tile:tile + lead, :]


def _dn_constants():
    ri = np.arange(SUPER)[:, None]
    ci = np.arange(SUPER)[None, :]
    same_chunk = (ri // CHUNK) == (ci // CHUNK)
    cf = np.stack([
        (ri == ci).astype(np.float32),
        np.where(same_chunk & (ri >= ci), 0.0, NEG).astype(np.float32),
        -(same_chunk & (ri > ci)).astype(np.float32),
        ((ri // SUB) == (ci // SUB)).astype(np.float32)])
    cb = np.stack([(same_chunk & (ri <= ci)).astype(np.float32)]
                  + [np.broadcast_to((ci // CHUNK) == c, (SUPER, SUPER)).astype(np.float32)
                     for c in range(SUPER // CHUNK)])
    return jnp.asarray(cf), jnp.asarray(cb, dtype=BF16)


def _deltanet(dn_qkv, dn_z, ba, conv_w, a_log, dt_bias):
    bsz, seq, _ = dn_qkv.shape
    tile = DN_TILE
    assert bsz % DN_BATCH == 0
    n_stream = 3 * DN_HEADS
    seq_tile = lambda b, t: (b, t, 0)
    fixed = lambda b, t: (0, 0)
    cf, cb = _dn_constants()
    cw_tiles = jnp.broadcast_to(conv_w.astype(F32).reshape(CONV_WIDTH, n_stream, 1, LANES),
                                (CONV_WIDTH, n_stream, SUBLANES, LANES))

    def slab(j):
        return pl.BlockSpec((DN_BATCH, tile, LANES), lambda b, t: (b, t, j))

    def whole(a):
        return pl.BlockSpec(a.shape, lambda b, t: (0,) * a.ndim)

    return pl.pallas_call(
        functools.partial(_dn_kernel, tile=tile),
        grid=(bsz // DN_BATCH, seq // tile),
        in_specs=[slab(j) for j in range(n_stream)]
        + [whole(cw_tiles),
           pl.BlockSpec((DN_BATCH, tile, LANES), seq_tile),
           pl.BlockSpec((DN_BATCH, tile, D_DN), seq_tile),
           pl.BlockSpec((SUBLANES, LANES), fixed),
           pl.BlockSpec((SUBLANES, LANES), fixed),
           whole(cf),
           whole(cb)],
        out_specs=pl.BlockSpec((DN_BATCH, tile, D_DN), seq_tile),
        out_shape=jax.ShapeDtypeStruct((bsz, seq, D_DN), BF16),
        scratch_shapes=[pltpu.VMEM((DN_BATCH * DN_HEADS, DN_HEAD_DIM, DN_HEAD_DIM), F32),
                        pltpu.VMEM((DN_BATCH * n_stream, tile + SUBLANES, LANES), F32)],
        compiler_params=pltpu.CompilerParams(
            dimension_semantics=("arbitrary", "arbitrary"), vmem_limit_bytes=VMEM_LIMIT),
        name="deltanet",
    )(*([dn_qkv] * n_stream), cw_tiles, ba, dn_z, a_log, dt_bias, cf, cb)


def _part_pitch(seq, r):
    return seq // r + 1 if r >= 2 * SUBLANES else None


def _part_rows(seq, r):
    pitch = _part_pitch(seq, r)
    return seq if pitch is None else -(-(r * pitch) // SUBLANES) * SUBLANES


def _att_kernel(*refs, seq):
    n_pat = len(DILATED_PATTERNS)
    qkv_refs = [refs[3 * p:3 * p + 3] for p in range(n_pat)]
    g_ref, bias_ref, o_ref = refs[3 * n_pat:3 * n_pat + 3]
    part_s = refs[3 * n_pat + 3:]
    tile = ATT_MERGE_ROWS
    lane = lax.broadcasted_iota(jnp.int32, (BLK, LANES), 1)
    head0 = lane < ATT_HEAD_DIM
    ones_v = jnp.ones((2 * BLK, LANES), BF16)
    zero_q = jnp.zeros((BLK, LANES), BF16)

    def store_part(p, r, res, n, vals):
        pitch = _part_pitch(seq, r)
        if pitch is None:
            tok0 = res + n * (BLK * r)
            rows = pl.ds(tok0, BLK) if r == 1 else pl.ds(tok0, BLK, stride=r)
        else:
            rows = pl.ds(res * pitch + n * BLK, BLK)
        for a, val in enumerate(vals):
            part_s[p][a, rows, :] = val

    def load_part(p, r, a, t0):
        pitch = _part_pitch(seq, r)
        if pitch is None:
            return part_s[p][a, pl.ds(t0, tile), :]
        m0 = t0 // r
        return jnp.concatenate([part_s[p][a, pl.ds(m0 + jj, r, stride=pitch), :]
                                for jj in range(tile // r)], axis=0)

    def group(p, r, n_blk, bi0):
        q_ref, k_ref, v_ref = qkv_refs[p]
        idx = []
        for g in range(ATT_GROUP):
            bi = bi0 + g
            base = pl.multiple_of(bi * BLK, BLK)
            kbase = pl.multiple_of(jnp.maximum(base - BLK, 0), BLK)
            n = bi % n_blk
            variant = jnp.where(bi == 0, 2, jnp.where(n == 0, 1, 0))
            idx.append((base, kbase, variant, (bi // n_blk, n)))
        scores = []
        for base, kbase, variant, _ in idx:
            q = q_ref[0, pl.ds(base, BLK), :]
            q2 = jnp.concatenate([jnp.where(head0, q, zero_q), jnp.where(head0, zero_q, q)], axis=0)
            k = k_ref[0, pl.ds(kbase, 2 * BLK), :]
            bias = jnp.concatenate([bias_ref[p, 0, variant], bias_ref[p, 1, variant]], axis=0)
            scores.append(lax.dot_general(q2, k, (((1,), (1,)), ((), ())),
                                          preferred_element_type=F32) + bias)
        maxes, probs = [], []
        for s in scores:
            m = jnp.max(s, axis=-1, keepdims=True)
            maxes.append(m)
            probs.append(jnp.exp2(s - m).astype(BF16))
        for (base, kbase, variant, (res, n)), m, e in zip(idx, maxes, probs):
            v2 = jnp.concatenate([v_ref[0, pl.ds(kbase, 2 * BLK), :], ones_v], axis=1)
            pv = jnp.dot(e, v2, preferred_element_type=F32)
            store_part(p, r, res, n,
                       (jnp.where(head0, m[:BLK], m[BLK:]),
                        jnp.where(head0, pv[:BLK, LANES:], pv[BLK:, LANES:]),
                        jnp.where(head0, pv[:BLK, :LANES], pv[BLK:, :LANES])))

    for p, (window, r) in enumerate(DILATED_PATTERNS):
        n_blk = seq // (r * BLK)

        def group_body(i, carry, p=p, r=r, n_blk=n_blk):
            group(p, r, n_blk, i * ATT_GROUP)
            return carry
        lax.fori_loop(0, seq // (BLK * ATT_GROUP), group_body, 0)

    def out_body(i, carry):
        t0 = pl.multiple_of(i * tile, tile)
        rows = pl.ds(t0, tile)
        part = [[load_part(p, r, a, t0) for a in range(3)]
                for p, (_, r) in enumerate(DILATED_PATTERNS)]
        m = functools.reduce(jnp.maximum, [pt[0] for pt in part])
        w = [jnp.exp2(pt[0] - m) for pt in part]
        den = sum(wp * pt[1] for wp, pt in zip(w, part))
        num = sum(wp * pt[2] for wp, pt in zip(w, part))
        g = g_ref[0, rows, :].astype(F32)
        o_ref[0, rows, :] = (num / den * g).astype(o_ref.dtype)
        return carry
    lax.fori_loop(0, seq // tile, out_body, 0)


def _dilated_attn(att_perm, att_g, bias):
    bsz, seq, _ = att_g.shape
    n_pair = ATT_HEADS // 2
    n_pat = bias.shape[0]

    def col(off):
        return pl.BlockSpec((1, seq, LANES), lambda b, j: (b, 0, off + j))

    qkv_specs, qkv_args = [], []
    for a in att_perm:
        qkv_specs += [col(0), col(n_pair), col(2 * n_pair)]
        qkv_args += [a, a, a]
    return pl.pallas_call(
        functools.partial(_att_kernel, seq=seq),
        grid=(bsz, n_pair),
        in_specs=qkv_specs
        + [col(0),
           pl.BlockSpec((n_pat, 2, N_BIAS_VARIANTS, BLK, 2 * BLK), lambda b, j: (0, j, 0, 0, 0))],
        out_specs=col(0),
        out_shape=jax.ShapeDtypeStruct((bsz, seq, D_ATT), BF16),
        scratch_shapes=[pltpu.VMEM((3, _part_rows(seq, r), LANES), F32)
                        for _, r in DILATED_PATTERNS],
        compiler_params=pltpu.CompilerParams(
            dimension_semantics=("arbitrary", "arbitrary"), vmem_limit_bytes=VMEM_LIMIT),
        name="dilated_attn",
    )(*qkv_args, att_g, bias)


def _out_proj_kernel(x_ref, ydn_ref, yatt_ref, w_ref, o_ref):
    o_ref[...] = (x_ref[...]
                  + jnp.dot(ydn_ref[...], w_ref[0:D_DN, :], preferred_element_type=F32)
                  + jnp.dot(yatt_ref[...], w_ref[D_DN:D_MIX, :], preferred_element_type=F32))


def _out_proj(x2d, y_dn, y_att, w_out):
    n = x2d.shape[0]
    tm = OUT_PROJ_ROWS
    row = lambda i: (i, 0)
    return pl.pallas_call(
        _out_proj_kernel,
        grid=(n // tm,),
        in_specs=[pl.BlockSpec((tm, D_MODEL), row),
                  pl.BlockSpec((tm, D_DN), row),
                  pl.BlockSpec((tm, D_ATT), row),
                  pl.BlockSpec((D_MIX, D_MODEL), lambda i: (0, 0))],
        out_specs=pl.BlockSpec((tm, D_MODEL), row),
        out_shape=jax.ShapeDtypeStruct((n, D_MODEL), F32),
        compiler_params=pltpu.CompilerParams(
            dimension_semantics=("arbitrary",), vmem_limit_bytes=VMEM_LIMIT),
        name="out_proj",
    )(x2d, y_dn, y_att, w_out)


def _gate_rows(v):
    col = jnp.pad(v.reshape(-1, 1).astype(F32), ((DN_HEADS, SUBLANES - 2 * DN_HEADS), (0, 0)))
    return jnp.broadcast_to(col, (SUBLANES, LANES))


def _layer(x, norm_w, w_in, conv_w, a_log, dt_bias, dn_norm_w, q_norm_w, k_norm_w, bias, w_out):
    bsz, seq, _ = x.shape
    x2d = x.reshape(bsz * seq, D_MODEL)
    c_dn = 4 * D_DN
    c_ba = c_dn + 2 * DN_HEADS
    w_ba = jnp.pad(w_in[:, c_dn:c_ba], ((0, 0), (0, LANES - 2 * DN_HEADS)))
    w_all = jnp.concatenate([w_in[:, :c_dn], w_in[:, c_ba:], w_ba], axis=1).astype(BF16)

    qw = jnp.tile(q_norm_w.reshape(1, ATT_HEAD_DIM).astype(F32), (1, 2)) * (ATT_HEAD_DIM ** -0.5 * LOG2E)
    kw = jnp.tile(k_norm_w.reshape(1, ATT_HEAD_DIM).astype(F32), (1, 2))
    dnw = jnp.tile(dn_norm_w.reshape(1, DN_HEAD_DIM).astype(F32), (1, DN_HEADS))
    dn_qkv, dn_z, ba, att_g, *att_perm = _in_proj(x, norm_w.reshape(1, D_MODEL), w_all, qw, kw, dnw)
    att_perm = [a.reshape(bsz, seq, 3 * D_ATT) for a in att_perm]

    y_dn = _deltanet(dn_qkv, dn_z, ba, conv_w, _gate_rows(a_log), _gate_rows(dt_bias))
    y_att = _dilated_attn(att_perm, att_g, bias)
    out = _out_proj(x2d, y_dn.reshape(bsz * seq, D_DN), y_att.reshape(bsz * seq, D_ATT),
                    w_out.astype(BF16))
    return out.reshape(bsz, seq, D_MODEL)


def kernel(x, norm_w, w_in, conv_w, a_log, dt_bias, dn_norm_w, q_norm_w, k_norm_w, rel_bias, w_out):
    bias = _bias_tables(rel_bias.astype(F32))
    for layer in range(norm_w.shape[0]):
        x = _layer(x, norm_w[layer], w_in[layer], conv_w[layer], a_log[layer], dt_bias[layer],
                   dn_norm_w[layer], q_norm_w[layer], k_norm_w[layer], bias, w_out[layer])
    return x
```

```python
import functools
import math

import numpy as np
import jax
import jax.numpy as jnp
from jax import lax
from jax.experimental import pallas as pl
from jax.experimental.pallas import tpu as pltpu

D_MODEL = 1024
D_DN = 512
DN_HEADS = 4
DN_HEAD_DIM = 128
CONV_WIDTH = 4
CHUNK = 64
D_ATT = 512
ATT_HEADS = 8
ATT_HEAD_DIM = 64
DILATED_PATTERNS = ((128, 1), (512, 4), (2048, 16))
N_BUCKETS = 32
MAX_DISTANCE = 2048
D_MIX = D_DN + D_ATT
EPS = 1e-6

LANES = 128
SUBLANES = 8
BLK = 128
ATT_GROUP = 8
SUPER = 128
DN_UNROLL = 1
DN_BATCH = 4
SUB = 8
NEG = -1e30
LOG2E = math.log2(math.e)
N_BIAS_VARIANTS = 3
IN_PROJ_ROWS = 512
DN_TILE = 256
ATT_MERGE_ROWS = 1024
OUT_PROJ_ROWS = 1024
V7X_VMEM_BYTES = 64 * 1024 * 1024
VMEM_LIMIT = V7X_VMEM_BYTES * 7 // 8

F32 = jnp.float32
BF16 = jnp.bfloat16


def _mm(a, b):
    return jnp.dot(a.astype(BF16), b.astype(BF16), preferred_element_type=F32)


def _mm_nt(a, b):
    return lax.dot_general(a.astype(BF16), b.astype(BF16), (((1,), (1,)), ((), ())),
                           preferred_element_type=F32)


def _sigmoid(x):
    return 0.5 * jnp.tanh(0.5 * x) + 0.5


def _in_proj_kernel(x_ref, nw_ref, w_ref, qw_ref, kw_ref, dnw_ref,
                    dn_ref, gates_ref, *att_refs_and_scratch):
    att_refs = att_refs_and_scratch[:len(DILATED_PATTERNS)]
    att_s, perm_s = att_refs_and_scratch[len(DILATED_PATTERNS):]
    tm = x_ref.shape[1]
    x = x_ref[0]
    ms = jnp.mean(x * x, axis=-1, keepdims=True)
    h = (x * lax.rsqrt(ms + EPS) * nw_ref[...]).astype(BF16)
    c0 = 3 * D_DN
    c1 = c0 + D_DN
    c2 = c1 + 3 * D_ATT
    c3 = c2 + D_ATT

    att = jnp.dot(h, w_ref[:, c1:c2], preferred_element_type=F32)
    head0 = lax.broadcasted_iota(jnp.int32, (tm, LANES), 1) < ATT_HEAD_DIM
    n_slab = 3 * D_ATT // LANES
    for j in range(n_slab):
        xs = att[:, j * LANES:(j + 1) * LANES]
        if j < 2 * D_ATT // LANES:
            is_q = j < D_ATT // LANES
            x2 = xs * xs
            s0 = jnp.sum(jnp.where(head0, x2, 0.0), axis=-1, keepdims=True)
            s1 = jnp.sum(jnp.where(head0, 0.0, x2), axis=-1, keepdims=True)
            ss = jnp.where(head0, s0, s1)
            wn = qw_ref[...] if is_q else kw_ref[...]
            xs = xs * lax.rsqrt(ss * (1.0 / ATT_HEAD_DIM) + EPS) * wn
        att_s[j] = xs
    for j in range(n_slab):
        cols = slice(j * LANES, (j + 1) * LANES)
        src_ref, r_prev = att_s.at[j], 1
        for level, (o_ref, (_, r)) in enumerate(zip(att_refs, DILATED_PATTERNS)):
            step = r // r_prev
            rows_per = tm // r
            keep = level + 1 < len(DILATED_PATTERNS) and r > 1
            for res in range(r):
                if r == 1:
                    piece = src_ref[...]
                else:
                    start = (res % r_prev) * (tm // r_prev) + res // r_prev
                    piece = src_ref[pl.ds(start, rows_per, stride=step), :]
                o_ref[0, res, :, cols] = piece.astype(BF16)
                if keep:
                    perm_s[j, pl.ds(res * rows_per, rows_per), :] = piece
            if keep:
                src_ref, r_prev = perm_s.at[j], r

    dn_ref[0, :, 0:c0] = jnp.dot(h, w_ref[:, 0:c0], preferred_element_type=F32)
    z = jnp.dot(h, w_ref[:, c0:c1], preferred_element_type=F32)
    gates_ref[0, :, 0:D_DN] = (z * _sigmoid(z) * dnw_ref[...]).astype(BF16)
    gate = jnp.dot(h, w_ref[:, c2:c3], preferred_element_type=F32)
    gates_ref[0, :, D_DN:D_MIX] = (gate * _sigmoid(gate)).astype(BF16)
    dn_ref[0, :, c0:c0 + LANES] = jnp.dot(h, w_ref[:, c3:c3 + LANES], preferred_element_type=F32)


def _in_proj(x, norm_w, w_all, qw, kw, dnw):
    bsz, seq, _ = x.shape
    tm = IN_PROJ_ROWS
    ncol = w_all.shape[1]
    row = lambda b, i: (b, i, 0)
    fixed = lambda b, i: (0, 0)
    att_specs = [pl.BlockSpec((1, r, tm // r, 3 * D_ATT), lambda b, i: (b, 0, i, 0))
                 for _, r in DILATED_PATTERNS]
    att_shapes = [jax.ShapeDtypeStruct((bsz, r, seq // r, 3 * D_ATT), BF16)
                  for _, r in DILATED_PATTERNS]
    return pl.pallas_call(
        _in_proj_kernel,
        grid=(bsz, seq // tm),
        in_specs=[pl.BlockSpec((1, tm, D_MODEL), row),
                  pl.BlockSpec((1, D_MODEL), fixed),
                  pl.BlockSpec((D_MODEL, ncol), fixed),
                  pl.BlockSpec((1, LANES), fixed),
                  pl.BlockSpec((1, LANES), fixed),
                  pl.BlockSpec((1, D_DN), fixed)],
        out_specs=[pl.BlockSpec((1, tm, 3 * D_DN + LANES), row),
                   pl.BlockSpec((1, tm, D_MIX), row)] + att_specs,
        out_shape=[jax.ShapeDtypeStruct((bsz, seq, 3 * D_DN + LANES), F32),
                   jax.ShapeDtypeStruct((bsz, seq, D_MIX), BF16)] + att_shapes,
        scratch_shapes=[pltpu.VMEM((3 * D_ATT // LANES, tm, LANES), F32),
                        pltpu.VMEM((3 * D_ATT // LANES, tm, LANES), F32)],
        compiler_params=pltpu.CompilerParams(
            dimension_semantics=("arbitrary", "arbitrary"), vmem_limit_bytes=VMEM_LIMIT),
        name="in_proj",
    )(x, norm_w, w_all, qw, kw, dnw)


def _t5_bucket(dist):
    max_exact = N_BUCKETS // 2
    d = np.maximum(dist, 1).astype(np.float64)
    large = max_exact + (np.log(d / max_exact) / math.log(MAX_DISTANCE / max_exact)
                         * (N_BUCKETS - max_exact)).astype(np.int32)
    large = np.minimum(large, N_BUCKETS - 1)
    return np.where(dist < max_exact, dist, large).astype(np.int32)


def _bucket_maps():
    maps = []
    for window, r in DILATED_PATTERNS:
        w_steps = window // r
        assert w_steps == BLK
        qi = np.arange(BLK)[:, None]
        kj = np.arange(2 * BLK)[None, :]
        step = qi - kj + BLK
        band = (step >= 0) & (step <= w_steps)
        buckets = _t5_bucket(np.clip(step, 0, None) * r)
        maps.append(np.where(band, buckets, -1).astype(np.int32))
    return np.stack(maps)


def _bias_kernel(rel_ref, bkt_ref, o_ref):
    col = lax.broadcasted_iota(jnp.int32, (BLK, 2 * BLK), 1)

    def head_body(h, carry):
        bkt = bkt_ref[0]
        acc = jnp.full(bkt.shape, NEG, F32)
        for b in range(N_BUCKETS):
            acc = jnp.where(bkt == b, rel_ref[h, b] * LOG2E, acc)
        o_ref[0, h, 0] = acc
        o_ref[0, h, 1] = jnp.where(col < BLK, NEG, acc)
        o_ref[0, h, 2] = jnp.concatenate([acc[:, BLK:], jnp.full((BLK, BLK), NEG, F32)], axis=1)
        return carry
    lax.fori_loop(0, ATT_HEADS, head_body, 0)


def _bias_tables(rel_bias):
    bkt = jnp.asarray(_bucket_maps())
    n_pat = bkt.shape[0]
    return pl.pallas_call(
        _bias_kernel,
        grid=(n_pat,),
        in_specs=[pl.BlockSpec(memory_space=pltpu.SMEM),
                  pl.BlockSpec((1, BLK, 2 * BLK), lambda p: (p, 0, 0))],
        out_specs=pl.BlockSpec((1, ATT_HEADS, N_BIAS_VARIANTS, BLK, 2 * BLK),
                               lambda p: (p, 0, 0, 0, 0)),
        out_shape=jax.ShapeDtypeStruct((n_pat, ATT_HEADS, N_BIAS_VARIANTS, BLK, 2 * BLK), F32),
        name="bias_tables",
    )(rel_bias, bkt)


def _dn_kernel(*refs, tile):
    nh = DN_HEADS
    n_stream = 3 * nh
    x_refs = refs[:n_stream]
    (ba_ref, cw_ref, z_ref, alog_ref, dtb_ref, cf_ref, cb_ref,
     o_ref, state_s, xs) = refs[n_stream:]
    n_chunk = SUPER // CHUNK
    heads = range(nh)
    lead = SUBLANES

    nb = DN_BATCH

    @pl.when(pl.program_id(1) == 0)
    def _():
        state_s[...] = jnp.zeros(state_s.shape, F32)
        xs[:, 0:lead, :] = jnp.zeros((nb * n_stream, lead, LANES), F32)

    def copy_body(i, carry):
        r0 = pl.multiple_of(i * SUPER, SUPER)
        for bb in range(nb):
            for j in range(n_stream):
                xs[bb * n_stream + j, pl.ds(r0 + lead, SUPER), :] = x_refs[j][bb, pl.ds(r0, SUPER), :]
        return carry
    lax.fori_loop(0, tile // SUPER, copy_body, 0)

    eye = lambda: cf_ref[0]
    neg_outside_incl = lambda: cf_ref[1]
    neg_in_strict = lambda: cf_ref[2]
    in_sub = lambda: cf_ref[3]
    cum_mat = lambda: cb_ref[0]
    chunk_cols = lambda c: cb_ref[1 + c]
    is_beta_row = lax.broadcasted_iota(jnp.int32, (SUBLANES, SUPER), 0) < nh

    a_coef = -LOG2E * jnp.exp(alog_ref[...])
    dt_b = dtb_ref[...]

    def conv_silu(bb, j, r0):
        def tap(s):
            w = 0.5 * cw_ref[CONV_WIDTH - 1 - s, j]
            return (xs[bb * n_stream + j, pl.ds(r0 + lead - s, SUPER), :]
                    * jnp.tile(w, (SUPER // SUBLANES, 1)))
        h = tap(0)
        for s in range(1, CONV_WIDTH):
            h = h + tap(s)
        return h * jnp.tanh(h) + h

    def l2n(x):
        return x * lax.rsqrt(jnp.sum(x * x, axis=-1, keepdims=True) + EPS)

    def lane_bcast(x, j):
        return jnp.broadcast_to(x[:, j:j + 1], x.shape)

    def row_bcast(x, i, n_rows):
        return jnp.broadcast_to(x[i:i + 1, :], (n_rows, x.shape[1]))

    def gate_rows(bb, r0):
        pre = ba_ref[bb, pl.ds(r0, SUPER), :].T[0:SUBLANES, :]
        xg = pre + dt_b
        g = a_coef * (jnp.maximum(xg, 0.0) + jnp.log(1.0 + jnp.exp(-jnp.abs(xg))))
        g_hi = g.astype(BF16)
        g_r = g - g_hi.astype(F32)
        g_mid = g_r.astype(BF16)
        g_lo = (g_r - g_mid.astype(F32)).astype(BF16)
        cs = jnp.dot(jnp.concatenate([g_hi, g_mid, g_lo], axis=0), cum_mat(),
                     preferred_element_type=F32)
        gc_r = cs[0:SUBLANES] + cs[SUBLANES:2 * SUBLANES] + cs[2 * SUBLANES:3 * SUBLANES]
        gates = jnp.where(is_beta_row, _sigmoid(pre), gc_r)
        gates_c = jnp.concatenate([gates, jnp.zeros((SUPER - SUBLANES, SUPER), F32)], axis=0).T
        return gc_r, gates_c

    def body(it, carry):
        r0s = [pl.multiple_of((it * DN_UNROLL + u) * SUPER, SUPER) for u in range(DN_UNROLL)]
        items = [(u, bb, h) for u in range(DN_UNROLL) for bb in range(nb) for h in heads]
        idx = range(len(items))
        gate = {(u, bb): gate_rows(bb, r0s[u]) for u in range(DN_UNROLL) for bb in range(nb)}

        q = [l2n(conv_silu(bb, h, r0s[u])) * (DN_HEAD_DIM ** -0.5) for u, bb, h in items]
        k = [l2n(conv_silu(bb, nh + h, r0s[u])) for u, bb, h in items]
        v = [conv_silu(bb, 2 * nh + h, r0s[u]) for u, bb, h in items]

        beta = [lane_bcast(gate[u, bb][1], h) for u, bb, h in items]
        gc = [lane_bcast(gate[u, bb][1], nh + h) for u, bb, h in items]
        decay = [jnp.exp2(gc[i] - row_bcast(gate[u, bb][0], nh + h, SUPER) + neg_outside_incl())
                 for i, (u, bb, h) in enumerate(items)]
        gc_last = [jnp.concatenate([row_bcast(gc[i], c * CHUNK + CHUNK - 1, CHUNK)
                                    for c in range(n_chunk)], axis=0) for i in idx]
        e_gc = [jnp.exp2(gc[i]) for i in idx]
        kb = [k[i] * beta[i] for i in idx]

        kq = [_mm_nt(jnp.concatenate([kb[i], q[i]], axis=0), k[i]) for i in idx]
        neg_a = [kq[i][:SUPER] * decay[i] * neg_in_strict() for i in idx]
        attn = [(kq[i][SUPER:] * decay[i]).astype(BF16) for i in idx]
        rhs = [jnp.concatenate([v[i] * beta[i], kb[i] * e_gc[i]], axis=1).astype(BF16) for i in idx]

        x1 = [neg_a[i] * in_sub() for i in idx]
        neg_l = [(neg_a[i] - x1[i]).astype(BF16) for i in idx]
        x1b = [x1[i].astype(BF16) for i in idx]
        x2 = [_mm(x1b[i], x1b[i]) for i in idx]
        x2b = [x2[i].astype(BF16) for i in idx]
        x4 = [_mm(x2b[i], x2b[i]) for i in idx]
        p1 = [eye() + x1[i] + x2[i] + _mm(x1b[i], x2b[i]) for i in idx]
        t_d = [p1[i] + _mm(p1[i], x4[i]) for i in idx]
        t_db = [t_d[i].astype(BF16) for i in idx]
        y1 = [_mm(t_db[i], neg_l[i]) for i in idx]
        td_rhs = [_mm(t_db[i], rhs[i]).astype(BF16) for i in idx]
        y1b = [y1[i].astype(BF16) for i in idx]
        y2 = [_mm(y1b[i], y1b[i]) for i in idx]
        y2b = [y2[i].astype(BF16) for i in idx]
        y4 = [_mm(y2b[i], y2b[i]) for i in idx]
        q1 = [eye() + y1[i] + y2[i] + _mm(y1b[i], y2b[i]) for i in idx]
        q2 = [q1[i] + _mm(q1[i], y4[i]) for i in idx]

        uw = [_mm(q2[i], td_rhs[i]).astype(BF16) for i in idx]
        aw = [_mm(attn[i], uw[i]) for i in idx]
        q_t = [(q[i] * e_gc[i] - aw[i][:, LANES:]).astype(BF16) for i in idx]
        kt_t = [(k[i] * jnp.exp2(gc_last[i] - gc[i])).T.astype(BF16) for i in idx]
        kw = [[_mm(kt_t[i] * chunk_cols(c), uw[i])
               for c in range(n_chunk)] for i in idx]

        chains = [(bb, h) for bb in range(nb) for h in heads]
        state = [state_s[bb * nh + h] for bb, h in chains]
        for u in range(DN_UNROLL):
            outs = [[] for _ in chains]
            for c in range(n_chunk):
                sl = slice(c * CHUNK, (c + 1) * CHUNK)
                s_bf = [st.astype(BF16) for st in state]
                for j in range(len(chains)):
                    i = u * len(chains) + j
                    outs[j].append(_mm(q_t[i][sl], s_bf[j]) + aw[i][sl, :LANES])
                for j in range(len(chains)):
                    i = u * len(chains) + j
                    g_last = jnp.exp2(row_bcast(gc[i], c * CHUNK + CHUNK - 1, LANES))
                    state[j] = (state[j] * g_last - _mm(kw[i][c][:, LANES:], s_bf[j])
                                + kw[i][c][:, :LANES])
            rows = pl.ds(r0s[u], SUPER)
            for j, (bb, h) in enumerate(chains):
                o = jnp.concatenate(outs[j], axis=0)
                ms = jnp.mean(o * o, axis=-1, keepdims=True)
                z = z_ref[bb, rows, h * LANES:(h + 1) * LANES].astype(F32)
                y = o * lax.rsqrt(ms + EPS) * z
                o_ref[bb, rows, h * LANES:(h + 1) * LANES] = y.astype(o_ref.dtype)
        for j, (bb, h) in enumerate(chains):
            state_s[bb * nh + h] = state[j]
        return carry

    lax.fori_loop(0, tile // (SUPER * DN_UNROLL), body, 0)
    xs[:, 0:lead, :] = xs[:, tile:tile + lead, :]


def _dn_constants():
    ri = np.arange(SUPER)[:, None]
    ci = np.arange(SUPER)[None, :]
    same_chunk = (ri // CHUNK) == (ci // CHUNK)
    cf = np.stack([
        (ri == ci).astype(np.float32),
        np.where(same_chunk & (ri >= ci), 0.0, NEG).astype(np.float32),
        -(same_chunk & (ri > ci)).astype(np.float32),
        ((ri // SUB) == (ci // SUB)).astype(np.float32)])
    cb = np.stack([(same_chunk & (ri <= ci)).astype(np.float32)]
                  + [np.broadcast_to((ci // CHUNK) == c, (SUPER, SUPER)).astype(np.float32)
                     for c in range(SUPER // CHUNK)])
    return jnp.asarray(cf), jnp.asarray(cb, dtype=BF16)


def _deltanet(dn_all, gates, conv_w, a_log, dt_bias):
    bsz, seq, _ = dn_all.shape
    tile = DN_TILE
    assert bsz % DN_BATCH == 0
    n_stream = 3 * DN_HEADS
    seq_tile = lambda b, t: (b, t, 0)
    fixed = lambda b, t: (0, 0)
    cf, cb = _dn_constants()
    cw_tiles = jnp.broadcast_to(conv_w.astype(F32).reshape(CONV_WIDTH, n_stream, 1, LANES),
                                (CONV_WIDTH, n_stream, SUBLANES, LANES))

    def slab(j):
        return pl.BlockSpec((DN_BATCH, tile, LANES), lambda b, t: (b, t, j))

    def whole(a):
        return pl.BlockSpec(a.shape, lambda b, t: (0,) * a.ndim)

    return pl.pallas_call(
        functools.partial(_dn_kernel, tile=tile),
        grid=(bsz // DN_BATCH, seq // tile),
        in_specs=[slab(j) for j in range(n_stream + 1)]
        + [whole(cw_tiles),
           pl.BlockSpec((DN_BATCH, tile, D_DN), seq_tile),
           pl.BlockSpec((SUBLANES, LANES), fixed),
           pl.BlockSpec((SUBLANES, LANES), fixed),
           whole(cf),
           whole(cb)],
        out_specs=pl.BlockSpec((DN_BATCH, tile, D_DN), seq_tile),
        out_shape=jax.ShapeDtypeStruct((bsz, seq, D_DN), BF16),
        scratch_shapes=[pltpu.VMEM((DN_BATCH * DN_HEADS, DN_HEAD_DIM, DN_HEAD_DIM), F32),
                        pltpu.VMEM((DN_BATCH * n_stream, tile + SUBLANES, LANES), F32)],
        compiler_params=pltpu.CompilerParams(
            dimension_semantics=("arbitrary", "arbitrary"), vmem_limit_bytes=VMEM_LIMIT),
        name="deltanet",
    )(*([dn_all] * (n_stream + 1)), cw_tiles, gates, a_log, dt_bias, cf, cb)


def _part_pitch(seq, r):
    return seq // r + 1 if r >= 2 * SUBLANES else None


def _part_rows(seq, r):
    pitch = _part_pitch(seq, r)
    return seq if pitch is None else -(-(r * pitch) // SUBLANES) * SUBLANES


def _att_kernel(*refs, seq):
    n_pat = len(DILATED_PATTERNS)
    qkv_refs = [refs[3 * p:3 * p + 3] for p in range(n_pat)]
    g_ref, bias_ref, o_ref = refs[3 * n_pat:3 * n_pat + 3]
    part_s = refs[3 * n_pat + 3:]
    tile = ATT_MERGE_ROWS
    lane = lax.broadcasted_iota(jnp.int32, (BLK, LANES), 1)
    head0 = lane < ATT_HEAD_DIM
    ones_v = jnp.ones((2 * BLK, LANES), BF16)
    zero_q = jnp.zeros((BLK, LANES), BF16)

    def store_part(p, r, res, n, vals):
        pitch = _part_pitch(seq, r)
        if pitch is None:
            tok0 = res + n * (BLK * r)
            rows = pl.ds(tok0, BLK) if r == 1 else pl.ds(tok0, BLK, stride=r)
        else:
            rows = pl.ds(res * pitch + n * BLK, BLK)
        for a, val in enumerate(vals):
            part_s[p][a, rows, :] = val

    def load_part(p, r, a, t0):
        pitch = _part_pitch(seq, r)
        if pitch is None:
            return part_s[p][a, pl.ds(t0, tile), :]
        m0 = t0 // r
        return jnp.concatenate([part_s[p][a, pl.ds(m0 + jj, r, stride=pitch), :]
                                for jj in range(tile // r)], axis=0)

    def group(p, r, n_blk, bi0):
        q_ref, k_ref, v_ref = qkv_refs[p]
        idx = []
        for g in range(ATT_GROUP):
            bi = bi0 + g
            base = pl.multiple_of(bi * BLK, BLK)
            kbase = pl.multiple_of(jnp.maximum(base - BLK, 0), BLK)
            n = bi % n_blk
            variant = jnp.where(bi == 0, 2, jnp.where(n == 0, 1, 0))
            idx.append((base, kbase, variant, (bi // n_blk, n)))
        scores = []
        for base, kbase, variant, _ in idx:
            q = q_ref[0, pl.ds(base, BLK), :]
            q2 = jnp.concatenate([jnp.where(head0, q, zero_q), jnp.where(head0, zero_q, q)], axis=0)
            k = k_ref[0, pl.ds(kbase, 2 * BLK), :]
            bias = jnp.concatenate([bias_ref[p, 0, variant], bias_ref[p, 1, variant]], axis=0)
            scores.append(lax.dot_general(q2, k, (((1,), (1,)), ((), ())),
                                          preferred_element_type=F32) + bias)
        maxes, probs = [], []
        for s in scores:
            m = jnp.max(s, axis=-1, keepdims=True)
            maxes.append(m)
            probs.append(jnp.exp2(s - m).astype(BF16))
        for (base, kbase, variant, (res, n)), m, e in zip(idx, maxes, probs):
            v2 = jnp.concatenate([v_ref[0, pl.ds(kbase, 2 * BLK), :], ones_v], axis=1)
            pv = jnp.dot(e, v2, preferred_element_type=F32)
            store_part(p, r, res, n,
                       (jnp.where(head0, m[:BLK], m[BLK:]),
                        jnp.where(head0, pv[:BLK, LANES:], pv[BLK:, LANES:]),
                        jnp.where(head0, pv[:BLK, :LANES], pv[BLK:, :LANES])))

    for p, (window, r) in enumerate(DILATED_PATTERNS):
        n_blk = seq // (r * BLK)

        def group_body(i, carry, p=p, r=r, n_blk=n_blk):
            group(p, r, n_blk, i * ATT_GROUP)
            return carry
        lax.fori_loop(0, seq // (BLK * ATT_GROUP), group_body, 0)

    def out_body(i, carry):
        t0 = pl.multiple_of(i * tile, tile)
        rows = pl.ds(t0, tile)
        part = [[load_part(p, r, a, t0) for a in range(3)]
                for p, (_, r) in enumerate(DILATED_PATTERNS)]
        m = functools.reduce(jnp.maximum, [pt[0] for pt in part])
        w = [jnp.exp2(pt[0] - m) for pt in part]
        den = sum(wp * pt[1] for wp, pt in zip(w, part))
        num = sum(wp * pt[2] for wp, pt in zip(w, part))
        g = g_ref[0, rows, :].astype(F32)
        o_ref[0, rows, :] = (num / den * g).astype(o_ref.dtype)
        return carry
    lax.fori_loop(0, seq // tile, out_body, 0)


def _dilated_attn(att_perm, gates, bias):
    bsz, seq, _ = gates.shape
    n_pair = ATT_HEADS // 2
    n_pat = bias.shape[0]

    def col(off):
        return pl.BlockSpec((1, seq, LANES), lambda b, j: (b, 0, off + j))

    qkv_specs, qkv_args = [], []
    for a in att_perm:
        qkv_specs += [col(0), col(n_pair), col(2 * n_pair)]
        qkv_args += [a, a, a]
    return pl.pallas_call(
        functools.partial(_att_kernel, seq=seq),
        grid=(bsz, n_pair),
        in_specs=qkv_specs
        + [col(D_DN // LANES),
           pl.BlockSpec((n_pat, 2, N_BIAS_VARIANTS, BLK, 2 * BLK), lambda b, j: (0, j, 0, 0, 0))],
        out_specs=col(0),
        out_shape=jax.ShapeDtypeStruct((bsz, seq, D_ATT), BF16),
        scratch_shapes=[pltpu.VMEM((3, _part_rows(seq, r), LANES), F32)
                        for _, r in DILATED_PATTERNS],
        compiler_params=pltpu.CompilerParams(
            dimension_semantics=("arbitrary", "arbitrary"), vmem_limit_bytes=VMEM_LIMIT),
        name="dilated_attn",
    )(*qkv_args, gates, bias)


def _out_proj_kernel(x_ref, ydn_ref, yatt_ref, w_ref, o_ref):
    o_ref[...] = (x_ref[...]
                  + jnp.dot(ydn_ref[...], w_ref[0:D_DN, :], preferred_element_type=F32)
                  + jnp.dot(yatt_ref[...], w_ref[D_DN:D_MIX, :], preferred_element_type=F32))


def _out_proj(x2d, y_dn, y_att, w_out):
    n = x2d.shape[0]
    tm = OUT_PROJ_ROWS
    row = lambda i: (i, 0)
    return pl.pallas_call(
        _out_proj_kernel,
        grid=(n // tm,),
        in_specs=[pl.BlockSpec((tm, D_MODEL), row),
                  pl.BlockSpec((tm, D_DN), row),
                  pl.BlockSpec((tm, D_ATT), row),
                  pl.BlockSpec((D_MIX, D_MODEL), lambda i: (0, 0))],
        out_specs=pl.BlockSpec((tm, D_MODEL), row),
        out_shape=jax.ShapeDtypeStruct((n, D_MODEL), F32),
        compiler_params=pltpu.CompilerParams(
            dimension_semantics=("arbitrary",), vmem_limit_bytes=VMEM_LIMIT),
        name="out_proj",
    )(x2d, y_dn, y_att, w_out)


def _gate_rows(v):
    col = jnp.pad(v.reshape(-1, 1).astype(F32), ((DN_HEADS, SUBLANES - 2 * DN_HEADS), (0, 0)))
    return jnp.broadcast_to(col, (SUBLANES, LANES))


def _layer(x, norm_w, w_in, conv_w, a_log, dt_bias, dn_norm_w, q_norm_w, k_norm_w, bias, w_out):
    bsz, seq, _ = x.shape
    x2d = x.reshape(bsz * seq, D_MODEL)
    c_dn = 4 * D_DN
    c_ba = c_dn + 2 * DN_HEADS
    w_ba = jnp.pad(w_in[:, c_dn:c_ba], ((0, 0), (0, LANES - 2 * DN_HEADS)))
    w_all = jnp.concatenate([w_in[:, :c_dn], w_in[:, c_ba:], w_ba], axis=1).astype(BF16)

    qw = jnp.tile(q_norm_w.reshape(1, ATT_HEAD_DIM).astype(F32), (1, 2)) * (ATT_HEAD_DIM ** -0.5 * LOG2E)
    kw = jnp.tile(k_norm_w.reshape(1, ATT_HEAD_DIM).astype(F32), (1, 2))
    dnw = jnp.tile(dn_norm_w.reshape(1, DN_HEAD_DIM).astype(F32), (1, DN_HEADS))
    dn_all, gates, *att_perm = _in_proj(x, norm_w.reshape(1, D_MODEL), w_all, qw, kw, dnw)
    att_perm = [a.reshape(bsz, seq, 3 * D_ATT) for a in att_perm]

    y_dn = _deltanet(dn_all, gates, conv_w, _gate_rows(a_log), _gate_rows(dt_bias))
    y_att = _dilated_attn(att_perm, gates, bias)
    out = _out_proj(x2d, y_dn.reshape(bsz * seq, D_DN), y_att.reshape(bsz * seq, D_ATT),
                    w_out.astype(BF16))
    return out.reshape(bsz, seq, D_MODEL)


def kernel(x, norm_w, w_in, conv_w, a_log, dt_bias, dn_norm_w, q_norm_w, k_norm_w, rel_bias, w_out):
    bias = _bias_tables(rel_bias.astype(F32))
    for layer in range(norm_w.shape[0]):
        x = _layer(x, norm_w[layer], w_in[layer], conv_w[layer], a_log[layer], dt_bias[layer],
                   dn_norm_w[layer], q_norm_w[layer], k_norm_w[layer], bias, w_out[layer])
    return x
```

```python
import functools
import math

import numpy as np
import jax
import jax.numpy as jnp
from jax import lax
from jax.experimental import pallas as pl
from jax.experimental.pallas import tpu as pltpu

D_MODEL = 1024
D_DN = 512
DN_HEADS = 4
DN_HEAD_DIM = 128
CONV_WIDTH = 4
D_ATT = 512
ATT_HEADS = 8
ATT_HEAD_DIM = 64
DILATED_PATTERNS = ((128, 1), (512, 4), (2048, 16))
N_BUCKETS = 32
MAX_DISTANCE = 2048
D_MIX = D_DN + D_ATT
EPS = 1e-6

LANES = 128
SUBLANES = 8
BLK = 128
ATT_GROUP = 8
SUPER = 128
DN_CHUNK = 128
DN_UNROLL = 1
DN_BATCH = 4
SUB = 8
NEG = -1e30
LOG2E = math.log2(math.e)
N_BIAS_VARIANTS = 3
IN_PROJ_ROWS = 512
DN_TILE = 256
ATT_MERGE_ROWS = 1024
OUT_PROJ_ROWS = 1024
V7X_VMEM_BYTES = 64 * 1024 * 1024
VMEM_LIMIT = V7X_VMEM_BYTES * 7 // 8

F32 = jnp.float32
BF16 = jnp.bfloat16


def _mm(a, b):
    return jnp.dot(a.astype(BF16), b.astype(BF16), preferred_element_type=F32)


def _mm_nt(a, b):
    return lax.dot_general(a.astype(BF16), b.astype(BF16), (((1,), (1,)), ((), ())),
                           preferred_element_type=F32)


def _sigmoid(x):
    return 0.5 * jnp.tanh(0.5 * x) + 0.5


def _in_proj_kernel(x_ref, nw_ref, w_ref, qw_ref, kw_ref, dnw_ref,
                    dnqkv_ref, dnz_ref, ba_ref, attg_ref, *att_refs_and_scratch):
    att_refs = att_refs_and_scratch[:len(DILATED_PATTERNS)]
    att_s, perm_s = att_refs_and_scratch[len(DILATED_PATTERNS):]
    tm = x_ref.shape[1]
    x = x_ref[0]
    ms = jnp.mean(x * x, axis=-1, keepdims=True)
    h = (x * lax.rsqrt(ms + EPS) * nw_ref[...]).astype(BF16)
    c0 = 3 * D_DN
    c1 = c0 + D_DN
    c2 = c1 + 3 * D_ATT
    c3 = c2 + D_ATT

    att = jnp.dot(h, w_ref[:, c1:c2], preferred_element_type=F32)
    head0 = lax.broadcasted_iota(jnp.int32, (tm, LANES), 1) < ATT_HEAD_DIM
    n_slab = 3 * D_ATT // LANES
    for j in range(n_slab):
        xs = att[:, j * LANES:(j + 1) * LANES]
        if j < 2 * D_ATT // LANES:
            is_q = j < D_ATT // LANES
            x2 = xs * xs
            s0 = jnp.sum(jnp.where(head0, x2, 0.0), axis=-1, keepdims=True)
            s1 = jnp.sum(jnp.where(head0, 0.0, x2), axis=-1, keepdims=True)
            ss = jnp.where(head0, s0, s1)
            wn = qw_ref[...] if is_q else kw_ref[...]
            xs = xs * lax.rsqrt(ss * (1.0 / ATT_HEAD_DIM) + EPS) * wn
        att_s[j] = xs
    for j in range(n_slab):
        cols = slice(j * LANES, (j + 1) * LANES)
        src_ref, r_prev = att_s.at[j], 1
        for level, (o_ref, (_, r)) in enumerate(zip(att_refs, DILATED_PATTERNS)):
            step = r // r_prev
            rows_per = tm // r
            keep = level + 1 < len(DILATED_PATTERNS) and r > 1
            for res in range(r):
                if r == 1:
                    piece = src_ref[...]
                else:
                    start = (res % r_prev) * (tm // r_prev) + res // r_prev
                    piece = src_ref[pl.ds(start, rows_per, stride=step), :]
                o_ref[0, res, :, cols] = piece.astype(BF16)
                if keep:
                    perm_s[j, pl.ds(res * rows_per, rows_per), :] = piece
            if keep:
                src_ref, r_prev = perm_s.at[j], r

    dnqkv_ref[0] = jnp.dot(h, w_ref[:, 0:c0], preferred_element_type=F32)
    z = jnp.dot(h, w_ref[:, c0:c1], preferred_element_type=F32)
    dnz_ref[0] = (z * _sigmoid(z) * dnw_ref[...]).astype(BF16)
    gate = jnp.dot(h, w_ref[:, c2:c3], preferred_element_type=F32)
    attg_ref[0] = (gate * _sigmoid(gate)).astype(BF16)
    ba_ref[0] = jnp.dot(h, w_ref[:, c3:c3 + LANES], preferred_element_type=F32)


def _in_proj(x, norm_w, w_all, qw, kw, dnw):
    bsz, seq, _ = x.shape
    tm = IN_PROJ_ROWS
    ncol = w_all.shape[1]
    row = lambda b, i: (b, i, 0)
    fixed = lambda b, i: (0, 0)
    att_specs = [pl.BlockSpec((1, r, tm // r, 3 * D_ATT), lambda b, i: (b, 0, i, 0))
                 for _, r in DILATED_PATTERNS]
    att_shapes = [jax.ShapeDtypeStruct((bsz, r, seq // r, 3 * D_ATT), BF16)
                  for _, r in DILATED_PATTERNS]
    return pl.pallas_call(
        _in_proj_kernel,
        grid=(bsz, seq // tm),
        in_specs=[pl.BlockSpec((1, tm, D_MODEL), row),
                  pl.BlockSpec((1, D_MODEL), fixed),
                  pl.BlockSpec((D_MODEL, ncol), fixed),
                  pl.BlockSpec((1, LANES), fixed),
                  pl.BlockSpec((1, LANES), fixed),
                  pl.BlockSpec((1, D_DN), fixed)],
        out_specs=[pl.BlockSpec((1, tm, 3 * D_DN), row),
                   pl.BlockSpec((1, tm, D_DN), row),
                   pl.BlockSpec((1, tm, LANES), row),
                   pl.BlockSpec((1, tm, D_ATT), row)] + att_specs,
        out_shape=[jax.ShapeDtypeStruct((bsz, seq, 3 * D_DN), F32),
                   jax.ShapeDtypeStruct((bsz, seq, D_DN), BF16),
                   jax.ShapeDtypeStruct((bsz, seq, LANES), F32),
                   jax.ShapeDtypeStruct((bsz, seq, D_ATT), BF16)] + att_shapes,
        scratch_shapes=[pltpu.VMEM((3 * D_ATT // LANES, tm, LANES), F32),
                        pltpu.VMEM((3 * D_ATT // LANES, tm, LANES), F32)],
        compiler_params=pltpu.CompilerParams(
            dimension_semantics=("arbitrary", "arbitrary"), vmem_limit_bytes=VMEM_LIMIT),
        name="in_proj",
    )(x, norm_w, w_all, qw, kw, dnw)


def _t5_bucket(dist):
    max_exact = N_BUCKETS // 2
    d = np.maximum(dist, 1).astype(np.float64)
    large = max_exact + (np.log(d / max_exact) / math.log(MAX_DISTANCE / max_exact)
                         * (N_BUCKETS - max_exact)).astype(np.int32)
    large = np.minimum(large, N_BUCKETS - 1)
    return np.where(dist < max_exact, dist, large).astype(np.int32)


def _bucket_maps():
    maps = []
    for window, r in DILATED_PATTERNS:
        w_steps = window // r
        assert w_steps == BLK
        qi = np.arange(BLK)[:, None]
        kj = np.arange(2 * BLK)[None, :]
        step = qi - kj + BLK
        band = (step >= 0) & (step <= w_steps)
        buckets = _t5_bucket(np.clip(step, 0, None) * r)
        maps.append(np.where(band, buckets, -1).astype(np.int32))
    return np.stack(maps)


def _bias_kernel(rel_ref, bkt_ref, o_ref):
    col = lax.broadcasted_iota(jnp.int32, (BLK, 2 * BLK), 1)

    def head_body(h, carry):
        bkt = bkt_ref[0]
        acc = jnp.full(bkt.shape, NEG, F32)
        for b in range(N_BUCKETS):
            acc = jnp.where(bkt == b, rel_ref[h, b] * LOG2E, acc)
        o_ref[0, h, 0] = acc
        o_ref[0, h, 1] = jnp.where(col < BLK, NEG, acc)
        o_ref[0, h, 2] = jnp.concatenate([acc[:, BLK:], jnp.full((BLK, BLK), NEG, F32)], axis=1)
        return carry
    lax.fori_loop(0, ATT_HEADS, head_body, 0)


def _bias_tables(rel_bias):
    bkt = jnp.asarray(_bucket_maps())
    n_pat = bkt.shape[0]
    return pl.pallas_call(
        _bias_kernel,
        grid=(n_pat,),
        in_specs=[pl.BlockSpec(memory_space=pltpu.SMEM),
                  pl.BlockSpec((1, BLK, 2 * BLK), lambda p: (p, 0, 0))],
        out_specs=pl.BlockSpec((1, ATT_HEADS, N_BIAS_VARIANTS, BLK, 2 * BLK),
                               lambda p: (p, 0, 0, 0, 0)),
        out_shape=jax.ShapeDtypeStruct((n_pat, ATT_HEADS, N_BIAS_VARIANTS, BLK, 2 * BLK), F32),
        name="bias_tables",
    )(rel_bias, bkt)


def _dn_kernel(*refs, tile):
    nh = DN_HEADS
    n_stream = 3 * nh
    x_refs = refs[:n_stream]
    (cw_ref, ba_ref, z_ref, alog_ref, dtb_ref, cf_ref, cb_ref,
     o_ref, state_s, xs) = refs[n_stream:]
    n_chunk = SUPER // DN_CHUNK
    heads = range(nh)
    lead = SUBLANES

    nb = DN_BATCH

    @pl.when(pl.program_id(1) == 0)
    def _():
        state_s[...] = jnp.zeros(state_s.shape, F32)
        xs[:, 0:lead, :] = jnp.zeros((nb * n_stream, lead, LANES), F32)

    def copy_body(i, carry):
        r0 = pl.multiple_of(i * SUPER, SUPER)
        for bb in range(nb):
            for j in range(n_stream):
                xs[bb * n_stream + j, pl.ds(r0 + lead, SUPER), :] = x_refs[j][bb, pl.ds(r0, SUPER), :]
        return carry
    lax.fori_loop(0, tile // SUPER, copy_body, 0)

    eye = lambda: cf_ref[0]
    neg_outside_incl = lambda: cf_ref[1]
    neg_in_strict = lambda: cf_ref[2]
    in_sub = lambda: cf_ref[3]
    cum_mat = lambda: cb_ref[0]
    chunk_cols = lambda c: cb_ref[1 + c]
    is_beta_row = lax.broadcasted_iota(jnp.int32, (SUBLANES, SUPER), 0) < nh

    a_coef = -LOG2E * jnp.exp(alog_ref[...])
    dt_b = dtb_ref[...]

    def conv_silu(bb, j, r0):
        def tap(s):
            w = 0.5 * cw_ref[CONV_WIDTH - 1 - s, j]
            return (xs[bb * n_stream + j, pl.ds(r0 + lead - s, SUPER), :]
                    * jnp.tile(w, (SUPER // SUBLANES, 1)))
        h = tap(0)
        for s in range(1, CONV_WIDTH):
            h = h + tap(s)
        return h * jnp.tanh(h) + h

    def l2n(x):
        return x * lax.rsqrt(jnp.sum(x * x, axis=-1, keepdims=True) + EPS)

    def lane_bcast(x, j):
        return jnp.broadcast_to(x[:, j:j + 1], x.shape)

    def row_bcast(x, i, n_rows):
        return jnp.broadcast_to(x[i:i + 1, :], (n_rows, x.shape[1]))

    def gate_rows(bb, r0):
        pre = ba_ref[bb, pl.ds(r0, SUPER), :].T[0:SUBLANES, :]
        xg = pre + dt_b
        g = a_coef * (jnp.maximum(xg, 0.0) + jnp.log(1.0 + jnp.exp(-jnp.abs(xg))))
        g_hi = g.astype(BF16)
        g_r = g - g_hi.astype(F32)
        g_mid = g_r.astype(BF16)
        g_lo = (g_r - g_mid.astype(F32)).astype(BF16)
        cs = jnp.dot(jnp.concatenate([g_hi, g_mid, g_lo], axis=0), cum_mat(),
                     preferred_element_type=F32)
        gc_r = cs[0:SUBLANES] + cs[SUBLANES:2 * SUBLANES] + cs[2 * SUBLANES:3 * SUBLANES]
        gates = jnp.where(is_beta_row, _sigmoid(pre), gc_r)
        gates_c = jnp.concatenate([gates, jnp.zeros((SUPER - SUBLANES, SUPER), F32)], axis=0).T
        return gc_r, gates_c

    def body(it, carry):
        r0s = [pl.multiple_of((it * DN_UNROLL + u) * SUPER, SUPER) for u in range(DN_UNROLL)]
        items = [(u, bb, h) for u in range(DN_UNROLL) for bb in range(nb) for h in heads]
        idx = range(len(items))
        gate = {(u, bb): gate_rows(bb, r0s[u]) for u in range(DN_UNROLL) for bb in range(nb)}

        q = [l2n(conv_silu(bb, h, r0s[u])) * (DN_HEAD_DIM ** -0.5) for u, bb, h in items]
        k = [l2n(conv_silu(bb, nh + h, r0s[u])) for u, bb, h in items]
        v = [conv_silu(bb, 2 * nh + h, r0s[u]) for u, bb, h in items]

        beta = [lane_bcast(gate[u, bb][1], h) for u, bb, h in items]
        gc = [lane_bcast(gate[u, bb][1], nh + h) for u, bb, h in items]
        decay = [jnp.exp2(gc[i] - row_bcast(gate[u, bb][0], nh + h, SUPER) + neg_outside_incl())
                 for i, (u, bb, h) in enumerate(items)]
        gc_last = [jnp.concatenate([row_bcast(gc[i], c * DN_CHUNK + DN_CHUNK - 1, DN_CHUNK)
                                    for c in range(n_chunk)], axis=0) for i in idx]
        e_gc = [jnp.exp2(gc[i]) for i in idx]
        kb = [k[i] * beta[i] for i in idx]

        kq = [_mm_nt(jnp.concatenate([kb[i], q[i]], axis=0), k[i]) for i in idx]
        neg_a = [kq[i][:SUPER] * decay[i] * neg_in_strict() for i in idx]
        attn = [(kq[i][SUPER:] * decay[i]).astype(BF16) for i in idx]
        rhs = [jnp.concatenate([v[i] * beta[i], kb[i] * e_gc[i]], axis=1).astype(BF16) for i in idx]

        x1 = [neg_a[i] * in_sub() for i in idx]
        neg_l = [(neg_a[i] - x1[i]).astype(BF16) for i in idx]
        x1b = [x1[i].astype(BF16) for i in idx]
        x2 = [_mm(x1b[i], x1b[i]) for i in idx]
        x2b = [x2[i].astype(BF16) for i in idx]
        x4 = [_mm(x2b[i], x2b[i]) for i in idx]
        p1 = [eye() + x1[i] + x2[i] + _mm(x1b[i], x2b[i]) for i in idx]
        t_d = [p1[i] + _mm(p1[i], x4[i]) for i in idx]
        t_db = [t_d[i].astype(BF16) for i in idx]
        y1 = [_mm(t_db[i], neg_l[i]) for i in idx]
        td_rhs = [_mm(t_db[i], rhs[i]).astype(BF16) for i in idx]
        y1b = [y1[i].astype(BF16) for i in idx]
        y2 = [_mm(y1b[i], y1b[i]) for i in idx]
        y2b = [y2[i].astype(BF16) for i in idx]
        y4 = [_mm(y2b[i], y2b[i]) for i in idx]
        q1 = [eye() + y1[i] + y2[i] + _mm(y1b[i], y2b[i]) for i in idx]
        q2 = [q1[i] + _mm(q1[i], y4[i]) for i in idx]
        y_pow, covered = y4, 8
        while covered < DN_CHUNK // SUB:
            y_pow_b = [y_pow[i].astype(BF16) for i in idx]
            y_pow = [_mm(y_pow_b[i], y_pow_b[i]) for i in idx]
            q2 = [q2[i] + _mm(q2[i], y_pow[i]) for i in idx]
            covered *= 2

        uw = [_mm(q2[i], td_rhs[i]).astype(BF16) for i in idx]
        aw = [_mm(attn[i], uw[i]) for i in idx]
        q_t = [(q[i] * e_gc[i] - aw[i][:, LANES:]).astype(BF16) for i in idx]
        kt_t = [(k[i] * jnp.exp2(gc_last[i] - gc[i])).T.astype(BF16) for i in idx]
        kw = [[_mm(kt_t[i] * chunk_cols(c) if n_chunk > 1 else kt_t[i], uw[i])
               for c in range(n_chunk)] for i in idx]

        chains = [(bb, h) for bb in range(nb) for h in heads]
        state = [state_s[bb * nh + h] for bb, h in chains]
        for u in range(DN_UNROLL):
            outs = [[] for _ in chains]
            for c in range(n_chunk):
                sl = slice(c * DN_CHUNK, (c + 1) * DN_CHUNK)
                s_bf = [st.astype(BF16) for st in state]
                for j in range(len(chains)):
                    i = u * len(chains) + j
                    outs[j].append(_mm(q_t[i][sl], s_bf[j]) + aw[i][sl, :LANES])
                for j in range(len(chains)):
                    i = u * len(chains) + j
                    g_last = jnp.exp2(row_bcast(gc[i], c * DN_CHUNK + DN_CHUNK - 1, LANES))
                    state[j] = (state[j] * g_last - _mm(kw[i][c][:, LANES:], s_bf[j])
                                + kw[i][c][:, :LANES])
            rows = pl.ds(r0s[u], SUPER)
            for j, (bb, h) in enumerate(chains):
                o = jnp.concatenate(outs[j], axis=0)
                ms = jnp.mean(o * o, axis=-1, keepdims=True)
                z = z_ref[bb, rows, h * LANES:(h + 1) * LANES].astype(F32)
                y = o * lax.rsqrt(ms + EPS) * z
                o_ref[bb, rows, h * LANES:(h + 1) * LANES] = y.astype(o_ref.dtype)
        for j, (bb, h) in enumerate(chains):
            state_s[bb * nh + h] = state[j]
        return carry

    lax.fori_loop(0, tile // (SUPER * DN_UNROLL), body, 0)
    xs[:, 0:lead, :] = xs[:, tile:tile + lead, :]


def _dn_constants():
    ri = np.arange(SUPER)[:, None]
    ci = np.arange(SUPER)[None, :]
    same_chunk = (ri // DN_CHUNK) == (ci // DN_CHUNK)
    cf = np.stack([
        (ri == ci).astype(np.float32),
        np.where(same_chunk & (ri >= ci), 0.0, NEG).astype(np.float32),
        -(same_chunk & (ri > ci)).astype(np.float32),
        ((ri // SUB) == (ci // SUB)).astype(np.float32)])
    cb = np.stack([(same_chunk & (ri <= ci)).astype(np.float32)]
                  + [np.broadcast_to((ci // DN_CHUNK) == c, (SUPER, SUPER)).astype(np.float32)
                     for c in range(SUPER // DN_CHUNK)])
    return jnp.asarray(cf), jnp.asarray(cb, dtype=BF16)


def _deltanet(dn_qkv, dn_z, ba, conv_w, a_log, dt_bias):
    bsz, seq, _ = dn_qkv.shape
    tile = DN_TILE
    assert bsz % DN_BATCH == 0
    n_stream = 3 * DN_HEADS
    seq_tile = lambda b, t: (b, t, 0)
    fixed = lambda b, t: (0, 0)
    cf, cb = _dn_constants()
    cw_tiles = jnp.broadcast_to(conv_w.astype(F32).reshape(CONV_WIDTH, n_stream, 1, LANES),
                                (CONV_WIDTH, n_stream, SUBLANES, LANES))

    def slab(j):
        return pl.BlockSpec((DN_BATCH, tile, LANES), lambda b, t: (b, t, j))

    def whole(a):
        return pl.BlockSpec(a.shape, lambda b, t: (0,) * a.ndim)

    return pl.pallas_call(
        functools.partial(_dn_kernel, tile=tile),
        grid=(bsz // DN_BATCH, seq // tile),
        in_specs=[slab(j) for j in range(n_stream)]
        + [whole(cw_tiles),
           pl.BlockSpec((DN_BATCH, tile, LANES), seq_tile),
           pl.BlockSpec((DN_BATCH, tile, D_DN), seq_tile),
           pl.BlockSpec((SUBLANES, LANES), fixed),
           pl.BlockSpec((SUBLANES, LANES), fixed),
           whole(cf),
           whole(cb)],
        out_specs=pl.BlockSpec((DN_BATCH, tile, D_DN), seq_tile),
        out_shape=jax.ShapeDtypeStruct((bsz, seq, D_DN), BF16),
        scratch_shapes=[pltpu.VMEM((DN_BATCH * DN_HEADS, DN_HEAD_DIM, DN_HEAD_DIM), F32),
                        pltpu.VMEM((DN_BATCH * n_stream, tile + SUBLANES, LANES), F32)],
        compiler_params=pltpu.CompilerParams(
            dimension_semantics=("arbitrary", "arbitrary"), vmem_limit_bytes=VMEM_LIMIT),
        name="deltanet",
    )(*([dn_qkv] * n_stream), cw_tiles, ba, dn_z, a_log, dt_bias, cf, cb)


def _part_pitch(seq, r):
    return seq // r + 1 if r >= 2 * SUBLANES else None


def _part_rows(seq, r):
    pitch = _part_pitch(seq, r)
    return seq if pitch is None else -(-(r * pitch) // SUBLANES) * SUBLANES


def _att_kernel(*refs, seq):
    n_pat = len(DILATED_PATTERNS)
    qkv_refs = [refs[3 * p:3 * p + 3] for p in range(n_pat)]
    g_ref, bias_ref, o_ref = refs[3 * n_pat:3 * n_pat + 3]
    part_s = refs[3 * n_pat + 3:]
    tile = ATT_MERGE_ROWS
    lane = lax.broadcasted_iota(jnp.int32, (BLK, LANES), 1)
    head0 = lane < ATT_HEAD_DIM
    ones_v = jnp.ones((2 * BLK, LANES), BF16)
    zero_q = jnp.zeros((BLK, LANES), BF16)

    def store_part(p, r, res, n, vals):
        pitch = _part_pitch(seq, r)
        if pitch is None:
            tok0 = res + n * (BLK * r)
            rows = pl.ds(tok0, BLK) if r == 1 else pl.ds(tok0, BLK, stride=r)
        else:
            rows = pl.ds(res * pitch + n * BLK, BLK)
        for a, val in enumerate(vals):
            part_s[p][a, rows, :] = val

    def load_part(p, r, a, t0):
        pitch = _part_pitch(seq, r)
        if pitch is None:
            return part_s[p][a, pl.ds(t0, tile), :]
        m0 = t0 // r
        return jnp.concatenate([part_s[p][a, pl.ds(m0 + jj, r, stride=pitch), :]
                                for jj in range(tile // r)], axis=0)

    def group(p, r, n_blk, bi0):
        q_ref, k_ref, v_ref = qkv_refs[p]
        idx = []
        for g in range(ATT_GROUP):
            bi = bi0 + g
            base = pl.multiple_of(bi * BLK, BLK)
            kbase = pl.multiple_of(jnp.maximum(base - BLK, 0), BLK)
            n = bi % n_blk
            variant = jnp.where(bi == 0, 2, jnp.where(n == 0, 1, 0))
            idx.append((base, kbase, variant, (bi // n_blk, n)))
        scores = []
        for base, kbase, variant, _ in idx:
            q = q_ref[0, pl.ds(base, BLK), :]
            q2 = jnp.concatenate([jnp.where(head0, q, zero_q), jnp.where(head0, zero_q, q)], axis=0)
            k = k_ref[0, pl.ds(kbase, 2 * BLK), :]
            bias = jnp.concatenate([bias_ref[p, 0, variant], bias_ref[p, 1, variant]], axis=0)
            scores.append(lax.dot_general(q2, k, (((1,), (1,)), ((), ())),
                                          preferred_element_type=F32) + bias)
        maxes, probs = [], []
        for s in scores:
            m = jnp.max(s, axis=-1, keepdims=True)
            maxes.append(m)
            probs.append(jnp.exp2(s - m).astype(BF16))
        for (base, kbase, variant, (res, n)), m, e in zip(idx, maxes, probs):
            v2 = jnp.concatenate([v_ref[0, pl.ds(kbase, 2 * BLK), :], ones_v], axis=1)
            pv = jnp.dot(e, v2, preferred_element_type=F32)
            store_part(p, r, res, n,
                       (jnp.where(head0, m[:BLK], m[BLK:]),
                        jnp.where(head0, pv[:BLK, LANES:], pv[BLK:, LANES:]),
                        jnp.where(head0, pv[:BLK, :LANES], pv[BLK:, :LANES])))

    for p, (window, r) in enumerate(DILATED_PATTERNS):
        n_blk = seq // (r * BLK)

        def group_body(i, carry, p=p, r=r, n_blk=n_blk):
            group(p, r, n_blk, i * ATT_GROUP)
            return carry
        lax.fori_loop(0, seq // (BLK * ATT_GROUP), group_body, 0)

    def out_body(i, carry):
        t0 = pl.multiple_of(i * tile, tile)
        rows = pl.ds(t0, tile)
        part = [[load_part(p, r, a, t0) for a in range(3)]
                for p, (_, r) in enumerate(DILATED_PATTERNS)]
        m = functools.reduce(jnp.maximum, [pt[0] for pt in part])
        w = [jnp.exp2(pt[0] - m) for pt in part]
        den = sum(wp * pt[1] for wp, pt in zip(w, part))
        num = sum(wp * pt[2] for wp, pt in zip(w, part))
        g = g_ref[0, rows, :].astype(F32)
        o_ref[0, rows, :] = (num / den * g).astype(o_ref.dtype)
        return carry
    lax.fori_loop(0, seq // tile, out_body, 0)


def _dilated_attn(att_perm, att_g, bias):
    bsz, seq, _ = att_g.shape
    n_pair = ATT_HEADS // 2
    n_pat = bias.shape[0]

    def col(off):
        return pl.BlockSpec((1, seq, LANES), lambda b, j: (b, 0, off + j))

    qkv_specs, qkv_args = [], []
    for a in att_perm:
        qkv_specs += [col(0), col(n_pair), col(2 * n_pair)]
        qkv_args += [a, a, a]
    return pl.pallas_call(
        functools.partial(_att_kernel, seq=seq),
        grid=(bsz, n_pair),
        in_specs=qkv_specs
        + [col(0),
           pl.BlockSpec((n_pat, 2, N_BIAS_VARIANTS, BLK, 2 * BLK), lambda b, j: (0, j, 0, 0, 0))],
        out_specs=col(0),
        out_shape=jax.ShapeDtypeStruct((bsz, seq, D_ATT), BF16),
        scratch_shapes=[pltpu.VMEM((3, _part_rows(seq, r), LANES), F32)
                        for _, r in DILATED_PATTERNS],
        compiler_params=pltpu.CompilerParams(
            dimension_semantics=("arbitrary", "arbitrary"), vmem_limit_bytes=VMEM_LIMIT),
        name="dilated_attn",
    )(*qkv_args, att_g, bias)


def _out_proj_kernel(x_ref, ydn_ref, yatt_ref, w_ref, o_ref):
    o_ref[...] = (x_ref[...]
                  + jnp.dot(ydn_ref[...], w_ref[0:D_DN, :], preferred_element_type=F32)
                  + jnp.dot(yatt_ref[...], w_ref[D_DN:D_MIX, :], preferred_element_type=F32))


def _out_proj(x2d, y_dn, y_att, w_out):
    n = x2d.shape[0]
    tm = OUT_PROJ_ROWS
    row = lambda i: (i, 0)
    return pl.pallas_call(
        _out_proj_kernel,
        grid=(n // tm,),
        in_specs=[pl.BlockSpec((tm, D_MODEL), row),
                  pl.BlockSpec((tm, D_DN), row),
                  pl.BlockSpec((tm, D_ATT), row),
                  pl.BlockSpec((D_MIX, D_MODEL), lambda i: (0, 0))],
        out_specs=pl.BlockSpec((tm, D_MODEL), row),
        out_shape=jax.ShapeDtypeStruct((n, D_MODEL), F32),
        compiler_params=pltpu.CompilerParams(
            dimension_semantics=("arbitrary",), vmem_limit_bytes=VMEM_LIMIT),
        name="out_proj",
    )(x2d, y_dn, y_att, w_out)


def _gate_rows(v):
    col = jnp.pad(v.reshape(-1, 1).astype(F32), ((DN_HEADS, SUBLANES - 2 * DN_HEADS), (0, 0)))
    return jnp.broadcast_to(col, (SUBLANES, LANES))


def _layer(x, norm_w, w_in, conv_w, a_log, dt_bias, dn_norm_w, q_norm_w, k_norm_w, bias, w_out):
    bsz, seq, _ = x.shape
    x2d = x.reshape(bsz * seq, D_MODEL)
    c_dn = 4 * D_DN
    c_ba = c_dn + 2 * DN_HEADS
    w_ba = jnp.pad(w_in[:, c_dn:c_ba], ((0, 0), (0, LANES - 2 * DN_HEADS)))
    w_all = jnp.concatenate([w_in[:, :c_dn], w_in[:, c_ba:], w_ba], axis=1).astype(BF16)

    qw = jnp.tile(q_norm_w.reshape(1, ATT_HEAD_DIM).astype(F32), (1, 2)) * (ATT_HEAD_DIM ** -0.5 * LOG2E)
    kw = jnp.tile(k_norm_w.reshape(1, ATT_HEAD_DIM).astype(F32), (1, 2))
    dnw = jnp.tile(dn_norm_w.reshape(1, DN_HEAD_DIM).astype(F32), (1, DN_HEADS))
    dn_qkv, dn_z, ba, att_g, *att_perm = _in_proj(x, norm_w.reshape(1, D_MODEL), w_all, qw, kw, dnw)
    att_perm = [a.reshape(bsz, seq, 3 * D_ATT) for a in att_perm]

    y_dn = _deltanet(dn_qkv, dn_z, ba, conv_w, _gate_rows(a_log), _gate_rows(dt_bias))
    y_att = _dilated_attn(att_perm, att_g, bias)
    out = _out_proj(x2d, y_dn.reshape(bsz * seq, D_DN), y_att.reshape(bsz * seq, D_ATT),
                    w_out.astype(BF16))
    return out.reshape(bsz, seq, D_MODEL)


def kernel(x, norm_w, w_in, conv_w, a_log, dt_bias, dn_norm_w, q_norm_w, k_norm_w, rel_bias, w_out):
    bias = _bias_tables(rel_bias.astype(F32))
    for layer in range(norm_w.shape[0]):
        x = _layer(x, norm_w[layer], w_in[layer], conv_w[layer], a_log[layer], dt_bias[layer],
                   dn_norm_w[layer], q_norm_w[layer], k_norm_w[layer], bias, w_out[layer])
    return x
```

```python
import functools
import math

import numpy as np
import jax
import jax.numpy as jnp
from jax import lax
from jax.experimental import pallas as pl
from jax.experimental.pallas import tpu as pltpu

D_MODEL = 1024
D_DN = 512
DN_HEADS = 4
DN_HEAD_DIM = 128
CONV_WIDTH = 4
D_ATT = 512
ATT_HEADS = 8
ATT_HEAD_DIM = 64
DILATED_PATTERNS = ((128, 1), (512, 4), (2048, 16))
N_BUCKETS = 32
MAX_DISTANCE = 2048
D_MIX = D_DN + D_ATT
EPS = 1e-6

LANES = 128
SUBLANES = 8
BLK = 128
ATT_GROUP = 8
SUPER = 128
DN_CHUNK = 128
DN_UNROLL = 1
DN_BATCH = 4
SUB = 8
NEG = -1e30
LOG2E = math.log2(math.e)
N_BIAS_VARIANTS = 3
IN_PROJ_ROWS = 512
DN_TILE = 256
ATT_MERGE_ROWS = 1024
OUT_PROJ_ROWS = 1024
V7X_VMEM_BYTES = 64 * 1024 * 1024
VMEM_LIMIT = V7X_VMEM_BYTES * 7 // 8

F32 = jnp.float32
BF16 = jnp.bfloat16


def _mm(a, b):
    return jnp.dot(a.astype(BF16), b.astype(BF16), preferred_element_type=F32)


def _mm_nt(a, b):
    return lax.dot_general(a.astype(BF16), b.astype(BF16), (((1,), (1,)), ((), ())),
                           preferred_element_type=F32)


def _sigmoid(x):
    return 0.5 * jnp.tanh(0.5 * x) + 0.5


def _in_proj_kernel(x_ref, nw_ref, w_ref, qw_ref, kw_ref, dnw_ref,
                    dnqkv_ref, dnz_ref, ba_ref, attg_ref, *att_refs_and_scratch):
    att_refs = att_refs_and_scratch[:len(DILATED_PATTERNS)]
    att_s, perm_s = att_refs_and_scratch[len(DILATED_PATTERNS):]
    tm = x_ref.shape[1]
    x = x_ref[0]
    ms = jnp.mean(x * x, axis=-1, keepdims=True)
    h = (x * lax.rsqrt(ms + EPS) * nw_ref[...]).astype(BF16)
    c0 = 3 * D_DN
    c1 = c0 + D_DN
    c2 = c1 + 3 * D_ATT
    c3 = c2 + D_ATT

    att = jnp.dot(h, w_ref[:, c1:c2], preferred_element_type=F32)
    head0 = lax.broadcasted_iota(jnp.int32, (tm, LANES), 1) < ATT_HEAD_DIM
    n_slab = 3 * D_ATT // LANES
    for j in range(n_slab):
        xs = att[:, j * LANES:(j + 1) * LANES]
        if j < 2 * D_ATT // LANES:
            is_q = j < D_ATT // LANES
            x2 = xs * xs
            s0 = jnp.sum(jnp.where(head0, x2, 0.0), axis=-1, keepdims=True)
            s1 = jnp.sum(jnp.where(head0, 0.0, x2), axis=-1, keepdims=True)
            ss = jnp.where(head0, s0, s1)
            wn = qw_ref[...] if is_q else kw_ref[...]
            xs = xs * lax.rsqrt(ss * (1.0 / ATT_HEAD_DIM) + EPS) * wn
        att_s[j] = xs
    for j in range(n_slab):
        cols = slice(j * LANES, (j + 1) * LANES)
        src_ref, r_prev = att_s.at[j], 1
        for level, (o_ref, (_, r)) in enumerate(zip(att_refs, DILATED_PATTERNS)):
            step = r // r_prev
            rows_per = tm // r
            keep = level + 1 < len(DILATED_PATTERNS) and r > 1
            for res in range(r):
                if r == 1:
                    piece = src_ref[...]
                else:
                    start = (res % r_prev) * (tm // r_prev) + res // r_prev
                    piece = src_ref[pl.ds(start, rows_per, stride=step), :]
                o_ref[0, res, :, cols] = piece.astype(BF16)
                if keep:
                    perm_s[j, pl.ds(res * rows_per, rows_per), :] = piece
            if keep:
                src_ref, r_prev = perm_s.at[j], r

    dnqkv_ref[0] = jnp.dot(h, w_ref[:, 0:c0], preferred_element_type=F32)
    z = jnp.dot(h, w_ref[:, c0:c1], preferred_element_type=F32)
    dnz_ref[0] = (z * _sigmoid(z) * dnw_ref[...]).astype(BF16)
    gate = jnp.dot(h, w_ref[:, c2:c3], preferred_element_type=F32)
    attg_ref[0] = (gate * _sigmoid(gate)).astype(BF16)
    ba_ref[0] = jnp.dot(h, w_ref[:, c3:c3 + LANES], preferred_element_type=F32)


def _in_proj(x, norm_w, w_all, qw, kw, dnw):
    bsz, seq, _ = x.shape
    tm = IN_PROJ_ROWS
    ncol = w_all.shape[1]
    row = lambda b, i: (b, i, 0)
    fixed = lambda b, i: (0, 0)
    att_specs = [pl.BlockSpec((1, r, tm // r, 3 * D_ATT), lambda b, i: (b, 0, i, 0))
                 for _, r in DILATED_PATTERNS]
    att_shapes = [jax.ShapeDtypeStruct((bsz, r, seq // r, 3 * D_ATT), BF16)
                  for _, r in DILATED_PATTERNS]
    return pl.pallas_call(
        _in_proj_kernel,
        grid=(bsz, seq // tm),
        in_specs=[pl.BlockSpec((1, tm, D_MODEL), row),
                  pl.BlockSpec((1, D_MODEL), fixed),
                  pl.BlockSpec((D_MODEL, ncol), fixed),
                  pl.BlockSpec((1, LANES), fixed),
                  pl.BlockSpec((1, LANES), fixed),
                  pl.BlockSpec((1, D_DN), fixed)],
        out_specs=[pl.BlockSpec((1, tm, 3 * D_DN), row),
                   pl.BlockSpec((1, tm, D_DN), row),
                   pl.BlockSpec((1, tm, LANES), row),
                   pl.BlockSpec((1, tm, D_ATT), row)] + att_specs,
        out_shape=[jax.ShapeDtypeStruct((bsz, seq, 3 * D_DN), F32),
                   jax.ShapeDtypeStruct((bsz, seq, D_DN), BF16),
                   jax.ShapeDtypeStruct((bsz, seq, LANES), F32),
                   jax.ShapeDtypeStruct((bsz, seq, D_ATT), BF16)] + att_shapes,
        scratch_shapes=[pltpu.VMEM((3 * D_ATT // LANES, tm, LANES), F32),
                        pltpu.VMEM((3 * D_ATT // LANES, tm, LANES), F32)],
        compiler_params=pltpu.CompilerParams(
            dimension_semantics=("arbitrary", "arbitrary"), vmem_limit_bytes=VMEM_LIMIT),
        name="in_proj",
    )(x, norm_w, w_all, qw, kw, dnw)


def _t5_bucket(dist):
    max_exact = N_BUCKETS // 2
    d = np.maximum(dist, 1).astype(np.float64)
    large = max_exact + (np.log(d / max_exact) / math.log(MAX_DISTANCE / max_exact)
                         * (N_BUCKETS - max_exact)).astype(np.int32)
    large = np.minimum(large, N_BUCKETS - 1)
    return np.where(dist < max_exact, dist, large).astype(np.int32)


def _bucket_maps():
    maps = []
    for window, r in DILATED_PATTERNS:
        w_steps = window // r
        assert w_steps == BLK
        qi = np.arange(BLK)[:, None]
        kj = np.arange(2 * BLK)[None, :]
        step = qi - kj + BLK
        band = (step >= 0) & (step <= w_steps)
        buckets = _t5_bucket(np.clip(step, 0, None) * r)
        maps.append(np.where(band, buckets, -1).astype(np.int32))
    return np.stack(maps)


def _bias_kernel(rel_ref, bkt_ref, o_ref):
    col = lax.broadcasted_iota(jnp.int32, (BLK, 2 * BLK), 1)

    def head_body(h, carry):
        bkt = bkt_ref[0]
        acc = jnp.full(bkt.shape, NEG, F32)
        for b in range(N_BUCKETS):
            acc = jnp.where(bkt == b, rel_ref[h, b] * LOG2E, acc)
        o_ref[0, h, 0] = acc
        o_ref[0, h, 1] = jnp.where(col < BLK, NEG, acc)
        o_ref[0, h, 2] = jnp.concatenate([acc[:, BLK:], jnp.full((BLK, BLK), NEG, F32)], axis=1)
        return carry
    lax.fori_loop(0, ATT_HEADS, head_body, 0)


def _bias_tables(rel_bias):
    bkt = jnp.asarray(_bucket_maps())
    n_pat = bkt.shape[0]
    return pl.pallas_call(
        _bias_kernel,
        grid=(n_pat,),
        in_specs=[pl.BlockSpec(memory_space=pltpu.SMEM),
                  pl.BlockSpec((1, BLK, 2 * BLK), lambda p: (p, 0, 0))],
        out_specs=pl.BlockSpec((1, ATT_HEADS, N_BIAS_VARIANTS, BLK, 2 * BLK),
                               lambda p: (p, 0, 0, 0, 0)),
        out_shape=jax.ShapeDtypeStruct((n_pat, ATT_HEADS, N_BIAS_VARIANTS, BLK, 2 * BLK), F32),
        name="bias_tables",
    )(rel_bias, bkt)


def _dn_kernel(*refs, tile):
    nh = DN_HEADS
    n_stream = 3 * nh
    x_refs = refs[:n_stream]
    (cw_ref, ba_ref, z_ref, alog_ref, dtb_ref, cf_ref, cb_ref,
     o_ref, state_s, xs) = refs[n_stream:]
    n_chunk = SUPER // DN_CHUNK
    heads = range(nh)
    lead = SUBLANES

    nb = DN_BATCH

    @pl.when(pl.program_id(1) == 0)
    def _():
        state_s[...] = jnp.zeros(state_s.shape, F32)
        xs[:, 0:lead, :] = jnp.zeros((nb * n_stream, lead, LANES), F32)

    for bb in range(nb):
        for j in range(n_stream):
            xs[bb * n_stream + j, lead:lead + SUPER, :] = x_refs[j][bb, 0:SUPER, :]

    eye = lambda: cf_ref[0]
    neg_outside_incl = lambda: cf_ref[1]
    neg_in_strict = lambda: cf_ref[2]
    in_sub = lambda: cf_ref[3]
    cum_mat = lambda: cb_ref[0]
    chunk_cols = lambda c: cb_ref[1 + c]
    is_beta_row = lax.broadcasted_iota(jnp.int32, (SUBLANES, SUPER), 0) < nh

    a_coef = -LOG2E * jnp.exp(alog_ref[...])
    dt_b = dtb_ref[...]

    def conv_silu(bb, j, r0, in_xs):
        def tap(s):
            w = 0.5 * cw_ref[CONV_WIDTH - 1 - s, j]
            if in_xs:
                rows = xs[bb * n_stream + j, lead - s:lead - s + SUPER, :]
            else:
                rows = x_refs[j][bb, pl.ds(r0 - s, SUPER), :]
            return rows * jnp.tile(w, (SUPER // SUBLANES, 1))
        h = tap(0)
        for s in range(1, CONV_WIDTH):
            h = h + tap(s)
        return h * jnp.tanh(h) + h

    def l2n(x):
        return x * lax.rsqrt(jnp.sum(x * x, axis=-1, keepdims=True) + EPS)

    def lane_bcast(x, j):
        return jnp.broadcast_to(x[:, j:j + 1], x.shape)

    def row_bcast(x, i, n_rows):
        return jnp.broadcast_to(x[i:i + 1, :], (n_rows, x.shape[1]))

    def gate_rows(bb, r0):
        pre = ba_ref[bb, pl.ds(r0, SUPER), :].T[0:SUBLANES, :]
        xg = pre + dt_b
        g = a_coef * (jnp.maximum(xg, 0.0) + jnp.log(1.0 + jnp.exp(-jnp.abs(xg))))
        g_hi = g.astype(BF16)
        g_r = g - g_hi.astype(F32)
        g_mid = g_r.astype(BF16)
        g_lo = (g_r - g_mid.astype(F32)).astype(BF16)
        cs = jnp.dot(jnp.concatenate([g_hi, g_mid, g_lo], axis=0), cum_mat(),
                     preferred_element_type=F32)
        gc_r = cs[0:SUBLANES] + cs[SUBLANES:2 * SUBLANES] + cs[2 * SUBLANES:3 * SUBLANES]
        gates = jnp.where(is_beta_row, _sigmoid(pre), gc_r)
        gates_c = jnp.concatenate([gates, jnp.zeros((SUPER - SUBLANES, SUPER), F32)], axis=0).T
        return gc_r, gates_c

    def body(it, carry, first=False):
        if first:
            r0s = [u * SUPER for u in range(DN_UNROLL)]
        else:
            r0s = [pl.multiple_of((it * DN_UNROLL + u) * SUPER, SUPER) for u in range(DN_UNROLL)]
        in_xs = [first and u == 0 for u in range(DN_UNROLL)]
        items = [(u, bb, h) for u in range(DN_UNROLL) for bb in range(nb) for h in heads]
        idx = range(len(items))
        gate = {(u, bb): gate_rows(bb, r0s[u]) for u in range(DN_UNROLL) for bb in range(nb)}

        q = [l2n(conv_silu(bb, h, r0s[u], in_xs[u])) * (DN_HEAD_DIM ** -0.5) for u, bb, h in items]
        k = [l2n(conv_silu(bb, nh + h, r0s[u], in_xs[u])) for u, bb, h in items]
        v = [conv_silu(bb, 2 * nh + h, r0s[u], in_xs[u]) for u, bb, h in items]

        beta = [lane_bcast(gate[u, bb][1], h) for u, bb, h in items]
        gc = [lane_bcast(gate[u, bb][1], nh + h) for u, bb, h in items]
        decay = [jnp.exp2(gc[i] - row_bcast(gate[u, bb][0], nh + h, SUPER) + neg_outside_incl())
                 for i, (u, bb, h) in enumerate(items)]
        gc_last = [jnp.concatenate([row_bcast(gc[i], c * DN_CHUNK + DN_CHUNK - 1, DN_CHUNK)
                                    for c in range(n_chunk)], axis=0) for i in idx]
        e_gc = [jnp.exp2(gc[i]) for i in idx]
        kb = [k[i] * beta[i] for i in idx]

        kq = [_mm_nt(jnp.concatenate([kb[i], q[i]], axis=0), k[i]) for i in idx]
        neg_a = [kq[i][:SUPER] * decay[i] * neg_in_strict() for i in idx]
        attn = [(kq[i][SUPER:] * decay[i]).astype(BF16) for i in idx]
        rhs = [jnp.concatenate([v[i] * beta[i], kb[i] * e_gc[i]], axis=1).astype(BF16) for i in idx]

        x1 = [neg_a[i] * in_sub() for i in idx]
        neg_l = [(neg_a[i] - x1[i]).astype(BF16) for i in idx]
        x1b = [x1[i].astype(BF16) for i in idx]
        x2 = [_mm(x1b[i], x1b[i]) for i in idx]
        x2b = [x2[i].astype(BF16) for i in idx]
        x4 = [_mm(x2b[i], x2b[i]) for i in idx]
        p1 = [eye() + x1[i] + x2[i] + _mm(x1b[i], x2b[i]) for i in idx]
        t_d = [p1[i] + _mm(p1[i], x4[i]) for i in idx]
        t_db = [t_d[i].astype(BF16) for i in idx]
        y1 = [_mm(t_db[i], neg_l[i]) for i in idx]
        td_rhs = [_mm(t_db[i], rhs[i]).astype(BF16) for i in idx]
        y1b = [y1[i].astype(BF16) for i in idx]
        y2 = [_mm(y1b[i], y1b[i]) for i in idx]
        y2b = [y2[i].astype(BF16) for i in idx]
        y4 = [_mm(y2b[i], y2b[i]) for i in idx]
        q1 = [eye() + y1[i] + y2[i] + _mm(y1b[i], y2b[i]) for i in idx]
        q2 = [q1[i] + _mm(q1[i], y4[i]) for i in idx]
        y_pow, covered = y4, 8
        while covered < DN_CHUNK // SUB:
            y_pow_b = [y_pow[i].astype(BF16) for i in idx]
            y_pow = [_mm(y_pow_b[i], y_pow_b[i]) for i in idx]
            q2 = [q2[i] + _mm(q2[i], y_pow[i]) for i in idx]
            covered *= 2

        uw = [_mm(q2[i], td_rhs[i]).astype(BF16) for i in idx]
        aw = [_mm(attn[i], uw[i]) for i in idx]
        q_t = [(q[i] * e_gc[i] - aw[i][:, LANES:]).astype(BF16) for i in idx]
        kt_t = [(k[i] * jnp.exp2(gc_last[i] - gc[i])).T.astype(BF16) for i in idx]
        kw = [[_mm(kt_t[i] * chunk_cols(c) if n_chunk > 1 else kt_t[i], uw[i])
               for c in range(n_chunk)] for i in idx]

        chains = [(bb, h) for bb in range(nb) for h in heads]
        state = [state_s[bb * nh + h] for bb, h in chains]
        for u in range(DN_UNROLL):
            outs = [[] for _ in chains]
            for c in range(n_chunk):
                sl = slice(c * DN_CHUNK, (c + 1) * DN_CHUNK)
                s_bf = [st.astype(BF16) for st in state]
                for j in range(len(chains)):
                    i = u * len(chains) + j
                    outs[j].append(_mm(q_t[i][sl], s_bf[j]) + aw[i][sl, :LANES])
                for j in range(len(chains)):
                    i = u * len(chains) + j
                    g_last = jnp.exp2(row_bcast(gc[i], c * DN_CHUNK + DN_CHUNK - 1, LANES))
                    state[j] = (state[j] * g_last - _mm(kw[i][c][:, LANES:], s_bf[j])
                                + kw[i][c][:, :LANES])
            rows = pl.ds(r0s[u], SUPER)
            for j, (bb, h) in enumerate(chains):
                o = jnp.concatenate(outs[j], axis=0)
                ms = jnp.mean(o * o, axis=-1, keepdims=True)
                z = z_ref[bb, rows, h * LANES:(h + 1) * LANES].astype(F32)
                y = o * lax.rsqrt(ms + EPS) * z
                o_ref[bb, rows, h * LANES:(h + 1) * LANES] = y.astype(o_ref.dtype)
        for j, (bb, h) in enumerate(chains):
            state_s[bb * nh + h] = state[j]
        return carry

    body(0, 0, first=True)
    lax.fori_loop(1, tile // (SUPER * DN_UNROLL), body, 0)
    for bb in range(nb):
        for j in range(n_stream):
            xs[bb * n_stream + j, 0:lead, :] = x_refs[j][bb, tile - lead:tile, :]


def _dn_constants():
    ri = np.arange(SUPER)[:, None]
    ci = np.arange(SUPER)[None, :]
    same_chunk = (ri // DN_CHUNK) == (ci // DN_CHUNK)
    cf = np.stack([
        (ri == ci).astype(np.float32),
        np.where(same_chunk & (ri >= ci), 0.0, NEG).astype(np.float32),
        -(same_chunk & (ri > ci)).astype(np.float32),
        ((ri // SUB) == (ci // SUB)).astype(np.float32)])
    cb = np.stack([(same_chunk & (ri <= ci)).astype(np.float32)]
                  + [np.broadcast_to((ci // DN_CHUNK) == c, (SUPER, SUPER)).astype(np.float32)
                     for c in range(SUPER // DN_CHUNK)])
    return jnp.asarray(cf), jnp.asarray(cb, dtype=BF16)


def _deltanet(dn_qkv, dn_z, ba, conv_w, a_log, dt_bias):
    bsz, seq, _ = dn_qkv.shape
    tile = DN_TILE
    assert bsz % DN_BATCH == 0
    n_stream = 3 * DN_HEADS
    seq_tile = lambda b, t: (b, t, 0)
    fixed = lambda b, t: (0, 0)
    cf, cb = _dn_constants()
    cw_tiles = jnp.broadcast_to(conv_w.astype(F32).reshape(CONV_WIDTH, n_stream, 1, LANES),
                                (CONV_WIDTH, n_stream, SUBLANES, LANES))

    def slab(j):
        return pl.BlockSpec((DN_BATCH, tile, LANES), lambda b, t: (b, t, j))

    def whole(a):
        return pl.BlockSpec(a.shape, lambda b, t: (0,) * a.ndim)

    return pl.pallas_call(
        functools.partial(_dn_kernel, tile=tile),
        grid=(bsz // DN_BATCH, seq // tile),
        in_specs=[slab(j) for j in range(n_stream)]
        + [whole(cw_tiles),
           pl.BlockSpec((DN_BATCH, tile, LANES), seq_tile),
           pl.BlockSpec((DN_BATCH, tile, D_DN), seq_tile),
           pl.BlockSpec((SUBLANES, LANES), fixed),
           pl.BlockSpec((SUBLANES, LANES), fixed),
           whole(cf),
           whole(cb)],
        out_specs=pl.BlockSpec((DN_BATCH, tile, D_DN), seq_tile),
        out_shape=jax.ShapeDtypeStruct((bsz, seq, D_DN), BF16),
        scratch_shapes=[pltpu.VMEM((DN_BATCH * DN_HEADS, DN_HEAD_DIM, DN_HEAD_DIM), F32),
                        pltpu.VMEM((DN_BATCH * n_stream, SUBLANES + SUPER, LANES), F32)],
        compiler_params=pltpu.CompilerParams(
            dimension_semantics=("arbitrary", "arbitrary"), vmem_limit_bytes=VMEM_LIMIT),
        name="deltanet",
    )(*([dn_qkv] * n_stream), cw_tiles, ba, dn_z, a_log, dt_bias, cf, cb)


def _part_pitch(seq, r):
    return seq // r + 1 if r >= 2 * SUBLANES else None


def _part_rows(seq, r):
    pitch = _part_pitch(seq, r)
    return seq if pitch is None else -(-(r * pitch) // SUBLANES) * SUBLANES


def _att_kernel(*refs, seq):
    n_pat = len(DILATED_PATTERNS)
    qkv_refs = [refs[3 * p:3 * p + 3] for p in range(n_pat)]
    g_ref, bias_ref, o_ref = refs[3 * n_pat:3 * n_pat + 3]
    part_s = refs[3 * n_pat + 3:]
    tile = ATT_MERGE_ROWS
    lane = lax.broadcasted_iota(jnp.int32, (BLK, LANES), 1)
    head0 = lane < ATT_HEAD_DIM
    ones_v = jnp.ones((2 * BLK, LANES), BF16)
    zero_q = jnp.zeros((BLK, LANES), BF16)

    def store_part(p, r, res, n, vals):
        pitch = _part_pitch(seq, r)
        if pitch is None:
            tok0 = res + n * (BLK * r)
            rows = pl.ds(tok0, BLK) if r == 1 else pl.ds(tok0, BLK, stride=r)
        else:
            rows = pl.ds(res * pitch + n * BLK, BLK)
        for a, val in enumerate(vals):
            part_s[p][a, rows, :] = val

    def load_part(p, r, a, t0):
        pitch = _part_pitch(seq, r)
        if pitch is None:
            return part_s[p][a, pl.ds(t0, tile), :]
        m0 = t0 // r
        return jnp.concatenate([part_s[p][a, pl.ds(m0 + jj, r, stride=pitch), :]
                                for jj in range(tile // r)], axis=0)

    def group(p, r, n_blk, bi0):
        q_ref, k_ref, v_ref = qkv_refs[p]
        idx = []
        for g in range(ATT_GROUP):
            bi = bi0 + g
            base = pl.multiple_of(bi * BLK, BLK)
            kbase = pl.multiple_of(jnp.maximum(base - BLK, 0), BLK)
            n = bi % n_blk
            variant = jnp.where(bi == 0, 2, jnp.where(n == 0, 1, 0))
            idx.append((base, kbase, variant, (bi // n_blk, n)))
        scores = []
        for base, kbase, variant, _ in idx:
            q = q_ref[0, pl.ds(base, BLK), :]
            q2 = jnp.concatenate([jnp.where(head0, q, zero_q), jnp.where(head0, zero_q, q)], axis=0)
            k = k_ref[0, pl.ds(kbase, 2 * BLK), :]
            bias = jnp.concatenate([bias_ref[p, 0, variant], bias_ref[p, 1, variant]], axis=0)
            scores.append(lax.dot_general(q2, k, (((1,), (1,)), ((), ())),
                                          preferred_element_type=F32) + bias)
        maxes, probs = [], []
        for s in scores:
            m = jnp.max(s, axis=-1, keepdims=True)
            maxes.append(m)
            probs.append(jnp.exp2(s - m).astype(BF16))
        for (base, kbase, variant, (res, n)), m, e in zip(idx, maxes, probs):
            v2 = jnp.concatenate([v_ref[0, pl.ds(kbase, 2 * BLK), :], ones_v], axis=1)
            pv = jnp.dot(e, v2, preferred_element_type=F32)
            store_part(p, r, res, n,
                       (jnp.where(head0, m[:BLK], m[BLK:]),
                        jnp.where(head0, pv[:BLK, LANES:], pv[BLK:, LANES:]),
                        jnp.where(head0, pv[:BLK, :LANES], pv[BLK:, :LANES])))

    for p, (window, r) in enumerate(DILATED_PATTERNS):
        n_blk = seq // (r * BLK)

        def group_body(i, carry, p=p, r=r, n_blk=n_blk):
            group(p, r, n_blk, i * ATT_GROUP)
            return carry
        lax.fori_loop(0, seq // (BLK * ATT_GROUP), group_body, 0)

    def out_body(i, carry):
        t0 = pl.multiple_of(i * tile, tile)
        rows = pl.ds(t0, tile)
        part = [[load_part(p, r, a, t0) for a in range(3)]
                for p, (_, r) in enumerate(DILATED_PATTERNS)]
        m = functools.reduce(jnp.maximum, [pt[0] for pt in part])
        w = [jnp.exp2(pt[0] - m) for pt in part]
        den = sum(wp * pt[1] for wp, pt in zip(w, part))
        num = sum(wp * pt[2] for wp, pt in zip(w, part))
        g = g_ref[0, rows, :].astype(F32)
        o_ref[0, rows, :] = (num / den * g).astype(o_ref.dtype)
        return carry
    lax.fori_loop(0, seq // tile, out_body, 0)


def _dilated_attn(att_perm, att_g, bias):
    bsz, seq, _ = att_g.shape
    n_pair = ATT_HEADS // 2
    n_pat = bias.shape[0]

    def col(off):
        return pl.BlockSpec((1, seq, LANES), lambda b, j: (b, 0, off + j))

    qkv_specs, qkv_args = [], []
    for a in att_perm:
        qkv_specs += [col(0), col(n_pair), col(2 * n_pair)]
        qkv_args += [a, a, a]
    return pl.pallas_call(
        functools.partial(_att_kernel, seq=seq),
        grid=(bsz, n_pair),
        in_specs=qkv_specs
        + [col(0),
           pl.BlockSpec((n_pat, 2, N_BIAS_VARIANTS, BLK, 2 * BLK), lambda b, j: (0, j, 0, 0, 0))],
        out_specs=col(0),
        out_shape=jax.ShapeDtypeStruct((bsz, seq, D_ATT), BF16),
        scratch_shapes=[pltpu.VMEM((3, _part_rows(seq, r), LANES), F32)
                        for _, r in DILATED_PATTERNS],
        compiler_params=pltpu.CompilerParams(
            dimension_semantics=("arbitrary", "arbitrary"), vmem_limit_bytes=VMEM_LIMIT),
        name="dilated_attn",
    )(*qkv_args, att_g, bias)


def _out_proj_kernel(x_ref, ydn_ref, yatt_ref, w_ref, o_ref):
    o_ref[...] = (x_ref[...]
                  + jnp.dot(ydn_ref[...], w_ref[0:D_DN, :], preferred_element_type=F32)
                  + jnp.dot(yatt_ref[...], w_ref[D_DN:D_MIX, :], preferred_element_type=F32))


def _out_proj(x2d, y_dn, y_att, w_out):
    n = x2d.shape[0]
    tm = OUT_PROJ_ROWS
    row = lambda i: (i, 0)
    return pl.pallas_call(
        _out_proj_kernel,
        grid=(n // tm,),
        in_specs=[pl.BlockSpec((tm, D_MODEL), row),
                  pl.BlockSpec((tm, D_DN), row),
                  pl.BlockSpec((tm, D_ATT), row),
                  pl.BlockSpec((D_MIX, D_MODEL), lambda i: (0, 0))],
        out_specs=pl.BlockSpec((tm, D_MODEL), row),
        out_shape=jax.ShapeDtypeStruct((n, D_MODEL), F32),
        compiler_params=pltpu.CompilerParams(
            dimension_semantics=("arbitrary",), vmem_limit_bytes=VMEM_LIMIT),
        name="out_proj",
    )(x2d, y_dn, y_att, w_out)


def _gate_rows(v):
    col = jnp.pad(v.reshape(-1, 1).astype(F32), ((DN_HEADS, SUBLANES - 2 * DN_HEADS), (0, 0)))
    return jnp.broadcast_to(col, (SUBLANES, LANES))


def _layer(x, norm_w, w_in, conv_w, a_log, dt_bias, dn_norm_w, q_norm_w, k_norm_w, bias, w_out):
    bsz, seq, _ = x.shape
    x2d = x.reshape(bsz * seq, D_MODEL)
    c_dn = 4 * D_DN
    c_ba = c_dn + 2 * DN_HEADS
    w_ba = jnp.pad(w_in[:, c_dn:c_ba], ((0, 0), (0, LANES - 2 * DN_HEADS)))
    w_all = jnp.concatenate([w_in[:, :c_dn], w_in[:, c_ba:], w_ba], axis=1).astype(BF16)

    qw = jnp.tile(q_norm_w.reshape(1, ATT_HEAD_DIM).astype(F32), (1, 2)) * (ATT_HEAD_DIM ** -0.5 * LOG2E)
    kw = jnp.tile(k_norm_w.reshape(1, ATT_HEAD_DIM).astype(F32), (1, 2))
    dnw = jnp.tile(dn_norm_w.reshape(1, DN_HEAD_DIM).astype(F32), (1, DN_HEADS))
    dn_qkv, dn_z, ba, att_g, *att_perm = _in_proj(x, norm_w.reshape(1, D_MODEL), w_all, qw, kw, dnw)
    att_perm = [a.reshape(bsz, seq, 3 * D_ATT) for a in att_perm]

    y_dn = _deltanet(dn_qkv, dn_z, ba, conv_w, _gate_rows(a_log), _gate_rows(dt_bias))
    y_att = _dilated_attn(att_perm, att_g, bias)
    out = _out_proj(x2d, y_dn.reshape(bsz * seq, D_DN), y_att.reshape(bsz * seq, D_ATT),
                    w_out.astype(BF16))
    return out.reshape(bsz, seq, D_MODEL)


def kernel(x, norm_w, w_in, conv_w, a_log, dt_bias, dn_norm_w, q_norm_w, k_norm_w, rel_bias, w_out):
    bias = _bias_tables(rel_bias.astype(F32))
    for layer in range(norm_w.shape[0]):
        x = _layer(x, norm_w[layer], w_in[layer], conv_w[layer], a_log[layer], dt_bias[layer],
                   dn_norm_w[layer], q_norm_w[layer], k_norm_w[layer], bias, w_out[layer])
    return x
```

```python
import functools
import math

import numpy as np
import jax
import jax.numpy as jnp
from jax import lax
from jax.experimental import pallas as pl
from jax.experimental.pallas import tpu as pltpu

D_MODEL = 1024
D_DN = 512
DN_HEADS = 4
DN_HEAD_DIM = 128
CONV_WIDTH = 4
D_ATT = 512
ATT_HEADS = 8
ATT_HEAD_DIM = 64
DILATED_PATTERNS = ((128, 1), (512, 4), (2048, 16))
N_BUCKETS = 32
MAX_DISTANCE = 2048
D_MIX = D_DN + D_ATT
EPS = 1e-6

LANES = 128
SUBLANES = 8
BLK = 128
ATT_GROUP = 8
SUPER = 128
DN_CHUNK = 128
DN_UNROLL = 1
DN_BATCH = 4
SUB = 8
NEG = -1e30
LOG2E = math.log2(math.e)
N_BIAS_VARIANTS = 3
IN_PROJ_ROWS = 512
DN_TILE = 256
ATT_MERGE_ROWS = 1024
OUT_PROJ_ROWS = 1024
V7X_VMEM_BYTES = 64 * 1024 * 1024
VMEM_LIMIT = V7X_VMEM_BYTES * 7 // 8

F32 = jnp.float32
BF16 = jnp.bfloat16


def _mm(a, b):
    return jnp.dot(a.astype(BF16), b.astype(BF16), preferred_element_type=F32)


def _mm_nt(a, b):
    return lax.dot_general(a.astype(BF16), b.astype(BF16), (((1,), (1,)), ((), ())),
                           preferred_element_type=F32)


def _sigmoid(x):
    return 0.5 * jnp.tanh(0.5 * x) + 0.5


def _in_proj_kernel(x_ref, nw_ref, w_ref, qw_ref, kw_ref, dnw_ref,
                    dnqkv_ref, dnz_ref, ba_ref, attg_ref, *att_refs_and_scratch):
    att_refs = att_refs_and_scratch[:len(DILATED_PATTERNS)]
    att_s, perm_s = att_refs_and_scratch[len(DILATED_PATTERNS):]
    tm = x_ref.shape[1]
    x = x_ref[0]
    ms = jnp.mean(x * x, axis=-1, keepdims=True)
    h = (x * lax.rsqrt(ms + EPS) * nw_ref[...]).astype(BF16)
    c0 = 3 * D_DN
    c1 = c0 + D_DN
    c2 = c1 + 3 * D_ATT
    c3 = c2 + D_ATT

    att = jnp.dot(h, w_ref[:, c1:c2], preferred_element_type=F32)
    head0 = lax.broadcasted_iota(jnp.int32, (tm, LANES), 1) < ATT_HEAD_DIM
    n_slab = 3 * D_ATT // LANES
    for j in range(n_slab):
        xs = att[:, j * LANES:(j + 1) * LANES]
        if j < 2 * D_ATT // LANES:
            is_q = j < D_ATT // LANES
            x2 = xs * xs
            s0 = jnp.sum(jnp.where(head0, x2, 0.0), axis=-1, keepdims=True)
            s1 = jnp.sum(jnp.where(head0, 0.0, x2), axis=-1, keepdims=True)
            ss = jnp.where(head0, s0, s1)
            wn = qw_ref[...] if is_q else kw_ref[...]
            xs = xs * lax.rsqrt(ss * (1.0 / ATT_HEAD_DIM) + EPS) * wn
        att_s[j] = xs
    for j in range(n_slab):
        cols = slice(j * LANES, (j + 1) * LANES)
        src_ref, r_prev = att_s.at[j], 1
        for level, (o_ref, (_, r)) in enumerate(zip(att_refs, DILATED_PATTERNS)):
            step = r // r_prev
            rows_per = tm // r
            keep = level + 1 < len(DILATED_PATTERNS) and r > 1
            for res in range(r):
                if r == 1:
                    piece = src_ref[...]
                else:
                    start = (res % r_prev) * (tm // r_prev) + res // r_prev
                    piece = src_ref[pl.ds(start, rows_per, stride=step), :]
                o_ref[0, res, :, cols] = piece.astype(BF16)
                if keep:
                    perm_s[j, pl.ds(res * rows_per, rows_per), :] = piece
            if keep:
                src_ref, r_prev = perm_s.at[j], r

    dnqkv_ref[0] = jnp.dot(h, w_ref[:, 0:c0], preferred_element_type=F32)
    z = jnp.dot(h, w_ref[:, c0:c1], preferred_element_type=F32)
    dnz_ref[0] = (z * _sigmoid(z) * dnw_ref[...]).astype(BF16)
    gate = jnp.dot(h, w_ref[:, c2:c3], preferred_element_type=F32)
    attg_ref[0] = (gate * _sigmoid(gate)).astype(BF16)
    ba_ref[0] = jnp.dot(h, w_ref[:, c3:c3 + LANES], preferred_element_type=F32)


def _in_proj(x, norm_w, w_all, qw, kw, dnw):
    bsz, seq, _ = x.shape
    tm = IN_PROJ_ROWS
    ncol = w_all.shape[1]
    row = lambda b, i: (b, i, 0)
    fixed = lambda b, i: (0, 0)
    att_specs = [pl.BlockSpec((1, r, tm // r, 3 * D_ATT), lambda b, i: (b, 0, i, 0))
                 for _, r in DILATED_PATTERNS]
    att_shapes = [jax.ShapeDtypeStruct((bsz, r, seq // r, 3 * D_ATT), BF16)
                  for _, r in DILATED_PATTERNS]
    return pl.pallas_call(
        _in_proj_kernel,
        grid=(bsz, seq // tm),
        in_specs=[pl.BlockSpec((1, tm, D_MODEL), row),
                  pl.BlockSpec((1, D_MODEL), fixed),
                  pl.BlockSpec((D_MODEL, ncol), fixed),
                  pl.BlockSpec((1, LANES), fixed),
                  pl.BlockSpec((1, LANES), fixed),
                  pl.BlockSpec((1, D_DN), fixed)],
        out_specs=[pl.BlockSpec((1, tm, 3 * D_DN), row),
                   pl.BlockSpec((1, tm, D_DN), row),
                   pl.BlockSpec((1, tm, LANES), row),
                   pl.BlockSpec((1, tm, D_ATT), row)] + att_specs,
        out_shape=[jax.ShapeDtypeStruct((bsz, seq, 3 * D_DN), F32),
                   jax.ShapeDtypeStruct((bsz, seq, D_DN), BF16),
                   jax.ShapeDtypeStruct((bsz, seq, LANES), F32),
                   jax.ShapeDtypeStruct((bsz, seq, D_ATT), BF16)] + att_shapes,
        scratch_shapes=[pltpu.VMEM((3 * D_ATT // LANES, tm, LANES), F32),
                        pltpu.VMEM((3 * D_ATT // LANES, tm, LANES), F32)],
        compiler_params=pltpu.CompilerParams(
            dimension_semantics=("arbitrary", "arbitrary"), vmem_limit_bytes=VMEM_LIMIT),
        name="in_proj",
    )(x, norm_w, w_all, qw, kw, dnw)


def _t5_bucket(dist):
    max_exact = N_BUCKETS // 2
    d = np.maximum(dist, 1).astype(np.float64)
    large = max_exact + (np.log(d / max_exact) / math.log(MAX_DISTANCE / max_exact)
                         * (N_BUCKETS - max_exact)).astype(np.int32)
    large = np.minimum(large, N_BUCKETS - 1)
    return np.where(dist < max_exact, dist, large).astype(np.int32)


def _bucket_maps():
    maps = []
    for window, r in DILATED_PATTERNS:
        w_steps = window // r
        assert w_steps == BLK
        qi = np.arange(BLK)[:, None]
        kj = np.arange(2 * BLK)[None, :]
        step = qi - kj + BLK
        band = (step >= 0) & (step <= w_steps)
        buckets = _t5_bucket(np.clip(step, 0, None) * r)
        maps.append(np.where(band, buckets, -1).astype(np.int32))
    return np.stack(maps)


def _bias_kernel(rel_ref, bkt_ref, o_ref):
    col = lax.broadcasted_iota(jnp.int32, (BLK, 2 * BLK), 1)

    def head_body(h, carry):
        bkt = bkt_ref[0]
        acc = jnp.full(bkt.shape, NEG, F32)
        for b in range(N_BUCKETS):
            acc = jnp.where(bkt == b, rel_ref[h, b] * LOG2E, acc)
        o_ref[0, h, 0] = acc
        o_ref[0, h, 1] = jnp.where(col < BLK, NEG, acc)
        o_ref[0, h, 2] = jnp.concatenate([acc[:, BLK:], jnp.full((BLK, BLK), NEG, F32)], axis=1)
        return carry
    lax.fori_loop(0, ATT_HEADS, head_body, 0)


def _bias_tables(rel_bias):
    bkt = jnp.asarray(_bucket_maps())
    n_pat = bkt.shape[0]
    return pl.pallas_call(
        _bias_kernel,
        grid=(n_pat,),
        in_specs=[pl.BlockSpec(memory_space=pltpu.SMEM),
                  pl.BlockSpec((1, BLK, 2 * BLK), lambda p: (p, 0, 0))],
        out_specs=pl.BlockSpec((1, ATT_HEADS, N_BIAS_VARIANTS, BLK, 2 * BLK),
                               lambda p: (p, 0, 0, 0, 0)),
        out_shape=jax.ShapeDtypeStruct((n_pat, ATT_HEADS, N_BIAS_VARIANTS, BLK, 2 * BLK), F32),
        name="bias_tables",
    )(rel_bias, bkt)


def _dn_kernel(*refs, tile):
    nh = DN_HEADS
    n_stream = 3 * nh
    x_refs = refs[:n_stream]
    (cw_ref, ba_ref, z_ref, alog_ref, dtb_ref, cf_ref, cb_ref,
     o_ref, state_s, xs) = refs[n_stream:]
    n_chunk = SUPER // DN_CHUNK
    heads = range(nh)
    lead = SUBLANES

    nb = DN_BATCH

    @pl.when(pl.program_id(1) == 0)
    def _():
        state_s[...] = jnp.zeros(state_s.shape, F32)
        xs[:, 0:lead, :] = jnp.zeros((nb * n_stream, lead, LANES), F32)

    for bb in range(nb):
        for j in range(n_stream):
            xs[bb * n_stream + j, lead:lead + SUPER, :] = x_refs[j][bb, 0:SUPER, :]

    eye = lambda: cf_ref[0]
    neg_outside_incl = lambda: cf_ref[1]
    neg_in_strict = lambda: cf_ref[2]
    in_sub = lambda: cf_ref[3]
    cum_mat = lambda: cb_ref[0]
    chunk_cols = lambda c: cb_ref[1 + c]
    is_beta_row = lax.broadcasted_iota(jnp.int32, (SUBLANES, SUPER), 0) < nh

    a_coef = -LOG2E * jnp.exp(alog_ref[...])
    dt_b = dtb_ref[...]

    def conv_silu(bb, j, r0, in_xs):
        def tap(s):
            w = 0.5 * cw_ref[CONV_WIDTH - 1 - s, j]
            if in_xs:
                rows = xs[bb * n_stream + j, lead - s:lead - s + SUPER, :]
            else:
                rows = x_refs[j][bb, pl.ds(r0 - s, SUPER), :]
            return rows * jnp.tile(w, (SUPER // SUBLANES, 1))
        h = tap(0)
        for s in range(1, CONV_WIDTH):
            h = h + tap(s)
        return h * jnp.tanh(h) + h

    def l2n(x):
        return x * lax.rsqrt(jnp.sum(x * x, axis=-1, keepdims=True) + EPS)

    def lane_bcast(x, j):
        return jnp.broadcast_to(x[:, j:j + 1], x.shape)

    def row_bcast(x, i, n_rows):
        return jnp.broadcast_to(x[i:i + 1, :], (n_rows, x.shape[1]))

    def gate_rows(bb, r0):
        pre = ba_ref[bb, pl.ds(r0, SUPER), :].T[0:SUBLANES, :]
        xg = pre + dt_b
        g = a_coef * (jnp.maximum(xg, 0.0) + jnp.log(1.0 + jnp.exp(-jnp.abs(xg))))
        g_hi = g.astype(BF16)
        g_r = g - g_hi.astype(F32)
        g_mid = g_r.astype(BF16)
        g_lo = (g_r - g_mid.astype(F32)).astype(BF16)
        cs = jnp.dot(jnp.concatenate([g_hi, g_mid, g_lo], axis=0), cum_mat(),
                     preferred_element_type=F32)
        gc_r = cs[0:SUBLANES] + cs[SUBLANES:2 * SUBLANES] + cs[2 * SUBLANES:3 * SUBLANES]
        gates = jnp.where(is_beta_row, _sigmoid(pre), gc_r)
        gates_c = jnp.concatenate([gates, jnp.zeros((SUPER - SUBLANES, SUPER), F32)], axis=0).T
        return gc_r, gates_c

    def body(it, carry, first=False):
        if first:
            r0s = [u * SUPER for u in range(DN_UNROLL)]
        else:
            r0s = [pl.multiple_of((it * DN_UNROLL + u) * SUPER, SUPER) for u in range(DN_UNROLL)]
        in_xs = [first and u == 0 for u in range(DN_UNROLL)]
        items = [(u, bb, h) for u in range(DN_UNROLL) for bb in range(nb) for h in heads]
        idx = range(len(items))
        gate = {(u, bb): gate_rows(bb, r0s[u]) for u in range(DN_UNROLL) for bb in range(nb)}

        q = [l2n(conv_silu(bb, h, r0s[u], in_xs[u])) * (DN_HEAD_DIM ** -0.5) for u, bb, h in items]
        k = [l2n(conv_silu(bb, nh + h, r0s[u], in_xs[u])) for u, bb, h in items]
        v = [conv_silu(bb, 2 * nh + h, r0s[u], in_xs[u]) for u, bb, h in items]

        beta = [lane_bcast(gate[u, bb][1], h) for u, bb, h in items]
        gc = [lane_bcast(gate[u, bb][1], nh + h) for u, bb, h in items]
        decay = [jnp.exp2(gc[i] - row_bcast(gate[u, bb][0], nh + h, SUPER) + neg_outside_incl())
                 for i, (u, bb, h) in enumerate(items)]
        gc_last = [jnp.concatenate([row_bcast(gc[i], c * DN_CHUNK + DN_CHUNK - 1, DN_CHUNK)
                                    for c in range(n_chunk)], axis=0) for i in idx]
        e_gc = [jnp.exp2(gc[i]) for i in idx]
        kb = [k[i] * beta[i] for i in idx]

        kq = [_mm_nt(jnp.concatenate([kb[i], q[i]], axis=0), k[i]) for i in idx]
        neg_a = [kq[i][:SUPER] * decay[i] * neg_in_strict() for i in idx]
        attn = [(kq[i][SUPER:] * decay[i]).astype(BF16) for i in idx]
        rhs = [jnp.concatenate([v[i] * beta[i], kb[i] * e_gc[i]], axis=1).astype(BF16) for i in idx]

        x1 = [neg_a[i] * in_sub() for i in idx]
        neg_l = [(neg_a[i] - x1[i]).astype(BF16) for i in idx]
        x1b = [x1[i].astype(BF16) for i in idx]
        x2 = [_mm(x1b[i], x1b[i]) for i in idx]
        x2b = [x2[i].astype(BF16) for i in idx]
        x4 = [_mm(x2b[i], x2b[i]) for i in idx]
        p1 = [eye() + x1[i] + x2[i] + _mm(x1b[i], x2b[i]) for i in idx]
        t_d = [p1[i] + _mm(p1[i], x4[i]) for i in idx]
        t_db = [t_d[i].astype(BF16) for i in idx]
        y1 = [_mm(t_db[i], neg_l[i]) for i in idx]
        td_rhs = [_mm(t_db[i], rhs[i]).astype(BF16) for i in idx]
        y1b = [y1[i].astype(BF16) for i in idx]
        y2 = [_mm(y1b[i], y1b[i]) for i in idx]
        y2b = [y2[i].astype(BF16) for i in idx]
        y4 = [_mm(y2b[i], y2b[i]) for i in idx]
        q1 = [eye() + y1[i] + y2[i] + _mm(y1b[i], y2b[i]) for i in idx]
        q2 = [q1[i] + _mm(q1[i], y4[i]) for i in idx]
        y_pow, covered = y4, 8
        while covered < DN_CHUNK // SUB:
            y_pow_b = [y_pow[i].astype(BF16) for i in idx]
            y_pow = [_mm(y_pow_b[i], y_pow_b[i]) for i in idx]
            q2 = [q2[i] + _mm(q2[i], y_pow[i]) for i in idx]
            covered *= 2

        uw = [_mm(q2[i], td_rhs[i]).astype(BF16) for i in idx]
        aw = [_mm(attn[i], uw[i]) for i in idx]
        q_t = [(q[i] * e_gc[i] - aw[i][:, LANES:]).astype(BF16) for i in idx]
        kt_t = [(k[i] * jnp.exp2(gc_last[i] - gc[i])).T.astype(BF16) for i in idx]
        kw = [[_mm(kt_t[i] * chunk_cols(c) if n_chunk > 1 else kt_t[i], uw[i])
               for c in range(n_chunk)] for i in idx]

        chains = [(bb, h) for bb in range(nb) for h in heads]
        state = [state_s[bb * nh + h] for bb, h in chains]
        for u in range(DN_UNROLL):
            outs = [[] for _ in chains]
            for c in range(n_chunk):
                sl = slice(c * DN_CHUNK, (c + 1) * DN_CHUNK)
                s_bf = [st.astype(BF16) for st in state]
                for j in range(len(chains)):
                    i = u * len(chains) + j
                    outs[j].append(_mm(q_t[i][sl], s_bf[j]) + aw[i][sl, :LANES])
                for j in range(len(chains)):
                    i = u * len(chains) + j
                    g_last = jnp.exp2(row_bcast(gc[i], c * DN_CHUNK + DN_CHUNK - 1, LANES))
                    state[j] = (state[j] * g_last - _mm(kw[i][c][:, LANES:], s_bf[j])
                                + kw[i][c][:, :LANES])
            rows = pl.ds(r0s[u], SUPER)
            for j, (bb, h) in enumerate(chains):
                o = jnp.concatenate(outs[j], axis=0)
                ms = jnp.mean(o * o, axis=-1, keepdims=True)
                z = z_ref[bb, rows, h * LANES:(h + 1) * LANES].astype(F32)
                y = o * lax.rsqrt(ms + EPS) * z
                o_ref[bb, rows, h * LANES:(h + 1) * LANES] = y.astype(o_ref.dtype)
        for j, (bb, h) in enumerate(chains):
            state_s[bb * nh + h] = state[j]
        return carry

    body(0, 0, first=True)
    lax.fori_loop(1, tile // (SUPER * DN_UNROLL), body, 0)
    for bb in range(nb):
        for j in range(n_stream):
            xs[bb * n_stream + j, 0:lead, :] = x_refs[j][bb, tile - lead:tile, :]


def _dn_constants():
    ri = np.arange(SUPER)[:, None]
    ci = np.arange(SUPER)[None, :]
    same_chunk = (ri // DN_CHUNK) == (ci // DN_CHUNK)
    cf = np.stack([
        (ri == ci).astype(np.float32),
        np.where(same_chunk & (ri >= ci), 0.0, NEG).astype(np.float32),
        -(same_chunk & (ri > ci)).astype(np.float32),
        ((ri // SUB) == (ci // SUB)).astype(np.float32)])
    cb = np.stack([(same_chunk & (ri <= ci)).astype(np.float32)]
                  + [np.broadcast_to((ci // DN_CHUNK) == c, (SUPER, SUPER)).astype(np.float32)
                     for c in range(SUPER // DN_CHUNK)])
    return jnp.asarray(cf), jnp.asarray(cb, dtype=BF16)


def _deltanet(dn_qkv, dn_z, ba, conv_w, a_log, dt_bias):
    bsz, seq, _ = dn_qkv.shape
    tile = DN_TILE
    assert bsz % DN_BATCH == 0
    n_stream = 3 * DN_HEADS
    seq_tile = lambda b, t: (b, t, 0)
    fixed = lambda b, t: (0, 0)
    cf, cb = _dn_constants()
    cw_tiles = jnp.broadcast_to(conv_w.astype(F32).reshape(CONV_WIDTH, n_stream, 1, LANES),
                                (CONV_WIDTH, n_stream, SUBLANES, LANES))

    def slab(j):
        return pl.BlockSpec((DN_BATCH, tile, LANES), lambda b, t: (b, t, j))

    def whole(a):
        return pl.BlockSpec(a.shape, lambda b, t: (0,) * a.ndim)

    return pl.pallas_call(
        functools.partial(_dn_kernel, tile=tile),
        grid=(bsz // DN_BATCH, seq // tile),
        in_specs=[slab(j) for j in range(n_stream)]
        + [whole(cw_tiles),
           pl.BlockSpec((DN_BATCH, tile, LANES), seq_tile),
           pl.BlockSpec((DN_BATCH, tile, D_DN), seq_tile),
           pl.BlockSpec((SUBLANES, LANES), fixed),
           pl.BlockSpec((SUBLANES, LANES), fixed),
           whole(cf),
           whole(cb)],
        out_specs=pl.BlockSpec((DN_BATCH, tile, D_DN), seq_tile),
        out_shape=jax.ShapeDtypeStruct((bsz, seq, D_DN), BF16),
        scratch_shapes=[pltpu.VMEM((DN_BATCH * DN_HEADS, DN_HEAD_DIM, DN_HEAD_DIM), F32),
                        pltpu.VMEM((DN_BATCH * n_stream, SUBLANES + SUPER, LANES), F32)],
        compiler_params=pltpu.CompilerParams(
            dimension_semantics=("arbitrary", "arbitrary"), vmem_limit_bytes=VMEM_LIMIT),
        name="deltanet",
    )(*([dn_qkv] * n_stream), cw_tiles, ba, dn_z, a_log, dt_bias, cf, cb)


def _part_pitch(seq, r):
    return seq // r + 1 if r >= 2 * SUBLANES else None


def _part_rows(seq, r):
    pitch = _part_pitch(seq, r)
    return seq if pitch is None else -(-(r * pitch) // SUBLANES) * SUBLANES


def _att_kernel(*refs, seq):
    n_pat = len(DILATED_PATTERNS)
    qkv_refs = [refs[3 * p:3 * p + 3] for p in range(n_pat)]
    g_ref, bias_ref, o_ref = refs[3 * n_pat:3 * n_pat + 3]
    part_s = refs[3 * n_pat + 3:]
    tile = ATT_MERGE_ROWS
    lane = lax.broadcasted_iota(jnp.int32, (BLK, LANES), 1)
    head0 = lane < ATT_HEAD_DIM
    ones_v = jnp.ones((2 * BLK, LANES), BF16)
    zero_q = jnp.zeros((BLK, LANES), BF16)

    def store_part(p, r, res, n, vals):
        pitch = _part_pitch(seq, r)
        if pitch is None:
            tok0 = res + n * (BLK * r)
            rows = pl.ds(tok0, BLK) if r == 1 else pl.ds(tok0, BLK, stride=r)
        else:
            rows = pl.ds(res * pitch + n * BLK, BLK)
        for a, val in enumerate(vals):
            part_s[p][a, rows, :] = val

    def load_part(p, r, a, t0):
        pitch = _part_pitch(seq, r)
        if pitch is None:
            return part_s[p][a, pl.ds(t0, tile), :]
        m0 = t0 // r
        return jnp.concatenate([part_s[p][a, pl.ds(m0 + jj, r, stride=pitch), :]
                                for jj in range(tile // r)], axis=0)

    def group(p, r, n_blk, bi0):
        q_ref, k_ref, v_ref = qkv_refs[p]
        idx = []
        for g in range(ATT_GROUP):
            bi = bi0 + g
            base = pl.multiple_of(bi * BLK, BLK)
            kbase = pl.multiple_of(jnp.maximum(base - BLK, 0), BLK)
            n = bi % n_blk
            variant = jnp.where(bi == 0, 2, jnp.where(n == 0, 1, 0))
            idx.append((base, kbase, variant, (bi // n_blk, n)))
        scores = []
        for base, kbase, variant, _ in idx:
            q = q_ref[0, pl.ds(base, BLK), :]
            q2 = jnp.concatenate([jnp.where(head0, q, zero_q), jnp.where(head0, zero_q, q)], axis=0)
            k = k_ref[0, pl.ds(kbase, 2 * BLK), :]
            bias = jnp.concatenate([bias_ref[p, 0, variant], bias_ref[p, 1, variant]], axis=0)
            scores.append(lax.dot_general(q2, k, (((1,), (1,)), ((), ())),
                                          preferred_element_type=F32) + bias)
        maxes, probs = [], []
        for s in scores:
            m = jnp.max(s, axis=-1, keepdims=True)
            maxes.append(m)
            probs.append(jnp.exp2((s - m).astype(BF16)))
        for (base, kbase, variant, (res, n)), m, e in zip(idx, maxes, probs):
            v2 = jnp.concatenate([v_ref[0, pl.ds(kbase, 2 * BLK), :], ones_v], axis=1)
            pv = jnp.dot(e, v2, preferred_element_type=F32)
            store_part(p, r, res, n,
                       (jnp.where(head0, m[:BLK], m[BLK:]),
                        jnp.where(head0, pv[:BLK, LANES:], pv[BLK:, LANES:]),
                        jnp.where(head0, pv[:BLK, :LANES], pv[BLK:, :LANES])))

    for p, (window, r) in enumerate(DILATED_PATTERNS):
        n_blk = seq // (r * BLK)

        def group_body(i, carry, p=p, r=r, n_blk=n_blk):
            group(p, r, n_blk, i * ATT_GROUP)
            return carry
        lax.fori_loop(0, seq // (BLK * ATT_GROUP), group_body, 0)

    def out_body(i, carry):
        t0 = pl.multiple_of(i * tile, tile)
        rows = pl.ds(t0, tile)
        part = [[load_part(p, r, a, t0) for a in range(3)]
                for p, (_, r) in enumerate(DILATED_PATTERNS)]
        m = functools.reduce(jnp.maximum, [pt[0] for pt in part])
        w = [jnp.exp2(pt[0] - m) for pt in part]
        den = sum(wp * pt[1] for wp, pt in zip(w, part))
        num = sum(wp * pt[2] for wp, pt in zip(w, part))
        g = g_ref[0, rows, :].astype(F32)
        o_ref[0, rows, :] = (num / den * g).astype(o_ref.dtype)
        return carry
    lax.fori_loop(0, seq // tile, out_body, 0)


def _dilated_attn(att_perm, att_g, bias):
    bsz, seq, _ = att_g.shape
    n_pair = ATT_HEADS // 2
    n_pat = bias.shape[0]

    def col(off):
        return pl.BlockSpec((1, seq, LANES), lambda b, j: (b, 0, off + j))

    qkv_specs, qkv_args = [], []
    for a in att_perm:
        qkv_specs += [col(0), col(n_pair), col(2 * n_pair)]
        qkv_args += [a, a, a]
    return pl.pallas_call(
        functools.partial(_att_kernel, seq=seq),
        grid=(bsz, n_pair),
        in_specs=qkv_specs
        + [col(0),
           pl.BlockSpec((n_pat, 2, N_BIAS_VARIANTS, BLK, 2 * BLK), lambda b, j: (0, j, 0, 0, 0))],
        out_specs=col(0),
        out_shape=jax.ShapeDtypeStruct((bsz, seq, D_ATT), BF16),
        scratch_shapes=[pltpu.VMEM((3, _part_rows(seq, r), LANES), F32)
                        for _, r in DILATED_PATTERNS],
        compiler_params=pltpu.CompilerParams(
            dimension_semantics=("arbitrary", "arbitrary"), vmem_limit_bytes=VMEM_LIMIT),
        name="dilated_attn",
    )(*qkv_args, att_g, bias)


def _out_proj_kernel(x_ref, ydn_ref, yatt_ref, w_ref, o_ref):
    o_ref[...] = (x_ref[...]
                  + jnp.dot(ydn_ref[...], w_ref[0:D_DN, :], preferred_element_type=F32)
                  + jnp.dot(yatt_ref[...], w_ref[D_DN:D_MIX, :], preferred_element_type=F32))


def _out_proj(x2d, y_dn, y_att, w_out):
    n = x2d.shape[0]
    tm = OUT_PROJ_ROWS
    row = lambda i: (i, 0)
    return pl.pallas_call(
        _out_proj_kernel,
        grid=(n // tm,),
        in_specs=[pl.BlockSpec((tm, D_MODEL), row),
                  pl.BlockSpec((tm, D_DN), row),
                  pl.BlockSpec((tm, D_ATT), row),
                  pl.BlockSpec((D_MIX, D_MODEL), lambda i: (0, 0))],
        out_specs=pl.BlockSpec((tm, D_MODEL), row),
        out_shape=jax.ShapeDtypeStruct((n, D_MODEL), F32),
        compiler_params=pltpu.CompilerParams(
            dimension_semantics=("arbitrary",), vmem_limit_bytes=VMEM_LIMIT),
        name="out_proj",
    )(x2d, y_dn, y_att, w_out)


def _gate_rows(v):
    col = jnp.pad(v.reshape(-1, 1).astype(F32), ((DN_HEADS, SUBLANES - 2 * DN_HEADS), (0, 0)))
    return jnp.broadcast_to(col, (SUBLANES, LANES))


def _layer(x, norm_w, w_in, conv_w, a_log, dt_bias, dn_norm_w, q_norm_w, k_norm_w, bias, w_out):
    bsz, seq, _ = x.shape
    x2d = x.reshape(bsz * seq, D_MODEL)
    c_dn = 4 * D_DN
    c_ba = c_dn + 2 * DN_HEADS
    w_ba = jnp.pad(w_in[:, c_dn:c_ba], ((0, 0), (0, LANES - 2 * DN_HEADS)))
    w_all = jnp.concatenate([w_in[:, :c_dn], w_in[:, c_ba:], w_ba], axis=1).astype(BF16)

    qw = jnp.tile(q_norm_w.reshape(1, ATT_HEAD_DIM).astype(F32), (1, 2)) * (ATT_HEAD_DIM ** -0.5 * LOG2E)
    kw = jnp.tile(k_norm_w.reshape(1, ATT_HEAD_DIM).astype(F32), (1, 2))
    dnw = jnp.tile(dn_norm_w.reshape(1, DN_HEAD_DIM).astype(F32), (1, DN_HEADS))
    dn_qkv, dn_z, ba, att_g, *att_perm = _in_proj(x, norm_w.reshape(1, D_MODEL), w_all, qw, kw, dnw)
    att_perm = [a.reshape(bsz, seq, 3 * D_ATT) for a in att_perm]

    y_dn = _deltanet(dn_qkv, dn_z, ba, conv_w, _gate_rows(a_log), _gate_rows(dt_bias))
    y_att = _dilated_attn(att_perm, att_g, bias)
    out = _out_proj(x2d, y_dn.reshape(bsz * seq, D_DN), y_att.reshape(bsz * seq, D_ATT),
                    w_out.astype(BF16))
    return out.reshape(bsz, seq, D_MODEL)


def kernel(x, norm_w, w_in, conv_w, a_log, dt_bias, dn_norm_w, q_norm_w, k_norm_w, rel_bias, w_out):
    bias = _bias_tables(rel_bias.astype(F32))
    for layer in range(norm_w.shape[0]):
        x = _layer(x, norm_w[layer], w_in[layer], conv_w[layer], a_log[layer], dt_bias[layer],
                   dn_norm_w[layer], q_norm_w[layer], k_norm_w[layer], bias, w_out[layer])
    return x
```

```python
import functools
import math

import numpy as np
import jax
import jax.numpy as jnp
from jax import lax
from jax.experimental import pallas as pl
from jax.experimental.pallas import tpu as pltpu

D_MODEL = 1024
D_DN = 512
DN_HEADS = 4
DN_HEAD_DIM = 128
CONV_WIDTH = 4
D_ATT = 512
ATT_HEADS = 8
ATT_HEAD_DIM = 64
DILATED_PATTERNS = ((128, 1), (512, 4), (2048, 16))
N_BUCKETS = 32
MAX_DISTANCE = 2048
D_MIX = D_DN + D_ATT
EPS = 1e-6

LANES = 128
SUBLANES = 8
BLK = 128
ATT_GROUP = 8
SUPER = 128
DN_CHUNK = 128
DN_UNROLL = 1
DN_BATCH = 4
SUB = 8
NEG = -1e30
LOG2E = math.log2(math.e)
N_BIAS_VARIANTS = 3
IN_PROJ_ROWS = 512
DN_TILE = 256
ATT_MERGE_ROWS = 1024
OUT_PROJ_ROWS = 1024
V7X_VMEM_BYTES = 64 * 1024 * 1024
VMEM_LIMIT = V7X_VMEM_BYTES * 7 // 8

F32 = jnp.float32
BF16 = jnp.bfloat16


def _mm(a, b):
    return jnp.dot(a.astype(BF16), b.astype(BF16), preferred_element_type=F32)


def _mm_nt(a, b):
    return lax.dot_general(a.astype(BF16), b.astype(BF16), (((1,), (1,)), ((), ())),
                           preferred_element_type=F32)


def _sigmoid(x):
    return 0.5 * jnp.tanh(0.5 * x) + 0.5


def _in_proj_kernel(x_ref, nw_ref, w_ref, qw_ref, kw_ref, dnw_ref,
                    dnqkv_ref, dnz_ref, ba_ref, attg_ref, *att_refs_and_scratch):
    att_refs = att_refs_and_scratch[:len(DILATED_PATTERNS)]
    att_s, perm_s = att_refs_and_scratch[len(DILATED_PATTERNS):]
    tm = x_ref.shape[1]
    x = x_ref[0]
    ms = jnp.mean(x * x, axis=-1, keepdims=True)
    h = (x * lax.rsqrt(ms + EPS) * nw_ref[...]).astype(BF16)
    c0 = 3 * D_DN
    c1 = c0 + D_DN
    c2 = c1 + 3 * D_ATT
    c3 = c2 + D_ATT

    att = jnp.dot(h, w_ref[:, c1:c2], preferred_element_type=F32)
    head0 = lax.broadcasted_iota(jnp.int32, (tm, LANES), 1) < ATT_HEAD_DIM
    n_slab = 3 * D_ATT // LANES
    for j in range(n_slab):
        xs = att[:, j * LANES:(j + 1) * LANES]
        if j < 2 * D_ATT // LANES:
            is_q = j < D_ATT // LANES
            x2 = xs * xs
            s0 = jnp.sum(jnp.where(head0, x2, 0.0), axis=-1, keepdims=True)
            s1 = jnp.sum(jnp.where(head0, 0.0, x2), axis=-1, keepdims=True)
            ss = jnp.where(head0, s0, s1)
            wn = qw_ref[...] if is_q else kw_ref[...]
            xs = xs * lax.rsqrt(ss * (1.0 / ATT_HEAD_DIM) + EPS) * wn
        att_s[j] = xs
    for j in range(n_slab):
        cols = slice(j * LANES, (j + 1) * LANES)
        src_ref, r_prev = att_s.at[j], 1
        for level, (o_ref, (_, r)) in enumerate(zip(att_refs, DILATED_PATTERNS)):
            step = r // r_prev
            rows_per = tm // r
            keep = level + 1 < len(DILATED_PATTERNS) and r > 1
            for res in range(r):
                if r == 1:
                    piece = src_ref[...]
                else:
                    start = (res % r_prev) * (tm // r_prev) + res // r_prev
                    piece = src_ref[pl.ds(start, rows_per, stride=step), :]
                o_ref[0, res, :, cols] = piece.astype(BF16)
                if keep:
                    perm_s[j, pl.ds(res * rows_per, rows_per), :] = piece
            if keep:
                src_ref, r_prev = perm_s.at[j], r

    dnqkv_ref[0] = jnp.dot(h, w_ref[:, 0:c0], preferred_element_type=F32)
    z = jnp.dot(h, w_ref[:, c0:c1], preferred_element_type=F32)
    dnz_ref[0] = (z * _sigmoid(z) * dnw_ref[...]).astype(BF16)
    gate = jnp.dot(h, w_ref[:, c2:c3], preferred_element_type=F32)
    attg_ref[0] = (gate * _sigmoid(gate)).astype(BF16)
    ba_ref[0] = jnp.dot(h, w_ref[:, c3:c3 + LANES], preferred_element_type=F32)


def _in_proj(x, norm_w, w_all, qw, kw, dnw):
    bsz, seq, _ = x.shape
    tm = IN_PROJ_ROWS
    ncol = w_all.shape[1]
    row = lambda b, i: (b, i, 0)
    fixed = lambda b, i: (0, 0)
    att_specs = [pl.BlockSpec((1, r, tm // r, 3 * D_ATT), lambda b, i: (b, 0, i, 0))
                 for _, r in DILATED_PATTERNS]
    att_shapes = [jax.ShapeDtypeStruct((bsz, r, seq // r, 3 * D_ATT), BF16)
                  for _, r in DILATED_PATTERNS]
    return pl.pallas_call(
        _in_proj_kernel,
        grid=(bsz, seq // tm),
        in_specs=[pl.BlockSpec((1, tm, D_MODEL), row),
                  pl.BlockSpec((1, D_MODEL), fixed),
                  pl.BlockSpec((D_MODEL, ncol), fixed),
                  pl.BlockSpec((1, LANES), fixed),
                  pl.BlockSpec((1, LANES), fixed),
                  pl.BlockSpec((1, D_DN), fixed)],
        out_specs=[pl.BlockSpec((1, tm, 3 * D_DN), row),
                   pl.BlockSpec((1, tm, D_DN), row),
                   pl.BlockSpec((1, tm, LANES), row),
                   pl.BlockSpec((1, tm, D_ATT), row)] + att_specs,
        out_shape=[jax.ShapeDtypeStruct((bsz, seq, 3 * D_DN), F32),
                   jax.ShapeDtypeStruct((bsz, seq, D_DN), BF16),
                   jax.ShapeDtypeStruct((bsz, seq, LANES), F32),
                   jax.ShapeDtypeStruct((bsz, seq, D_ATT), BF16)] + att_shapes,
        scratch_shapes=[pltpu.VMEM((3 * D_ATT // LANES, tm, LANES), F32),
                        pltpu.VMEM((3 * D_ATT // LANES, tm, LANES), F32)],
        compiler_params=pltpu.CompilerParams(
            dimension_semantics=("arbitrary", "arbitrary"), vmem_limit_bytes=VMEM_LIMIT),
        name="in_proj",
    )(x, norm_w, w_all, qw, kw, dnw)


def _t5_bucket(dist):
    max_exact = N_BUCKETS // 2
    d = np.maximum(dist, 1).astype(np.float64)
    large = max_exact + (np.log(d / max_exact) / math.log(MAX_DISTANCE / max_exact)
                         * (N_BUCKETS - max_exact)).astype(np.int32)
    large = np.minimum(large, N_BUCKETS - 1)
    return np.where(dist < max_exact, dist, large).astype(np.int32)


def _bucket_maps():
    maps = []
    for window, r in DILATED_PATTERNS:
        w_steps = window // r
        assert w_steps == BLK
        qi = np.arange(BLK)[:, None]
        kj = np.arange(2 * BLK)[None, :]
        step = qi - kj + BLK
        band = (step >= 0) & (step <= w_steps)
        buckets = _t5_bucket(np.clip(step, 0, None) * r)
        maps.append(np.where(band, buckets, -1).astype(np.int32))
    return np.stack(maps)


def _fill_bias_tables(rel_ref, bkt_ref, bias_s, pair):
    col = lax.broadcasted_iota(jnp.int32, (BLK, 2 * BLK), 1)
    for p in range(len(DILATED_PATTERNS)):
        bkt = bkt_ref[p]
        for hh in range(2):
            h = 2 * pair + hh
            acc = jnp.full(bkt.shape, NEG, F32)
            for b in range(N_BUCKETS):
                acc = jnp.where(bkt == b, rel_ref[h, b] * LOG2E, acc)
            bias_s[p, hh, 0] = acc
            bias_s[p, hh, 1] = jnp.where(col < BLK, NEG, acc)
            bias_s[p, hh, 2] = jnp.concatenate([acc[:, BLK:], jnp.full((BLK, BLK), NEG, F32)], axis=1)


def _dn_kernel(*refs, tile):
    nh = DN_HEADS
    n_stream = 3 * nh
    x_refs = refs[:n_stream]
    (cw_ref, ba_ref, z_ref, alog_ref, dtb_ref, cf_ref, cb_ref,
     o_ref, state_s, xs) = refs[n_stream:]
    n_chunk = SUPER // DN_CHUNK
    heads = range(nh)
    lead = SUBLANES

    nb = DN_BATCH

    @pl.when(pl.program_id(1) == 0)
    def _():
        state_s[...] = jnp.zeros(state_s.shape, F32)
        xs[:, 0:lead, :] = jnp.zeros((nb * n_stream, lead, LANES), F32)

    for bb in range(nb):
        for j in range(n_stream):
            xs[bb * n_stream + j, lead:lead + SUPER, :] = x_refs[j][bb, 0:SUPER, :]

    eye = lambda: cf_ref[0]
    neg_outside_incl = lambda: cf_ref[1]
    neg_in_strict = lambda: cf_ref[2]
    in_sub = lambda: cf_ref[3]
    cum_mat = lambda: cb_ref[0]
    chunk_cols = lambda c: cb_ref[1 + c]
    is_beta_row = lax.broadcasted_iota(jnp.int32, (SUBLANES, SUPER), 0) < nh

    a_coef = -LOG2E * jnp.exp(alog_ref[...])
    dt_b = dtb_ref[...]

    def conv_silu(bb, j, r0, in_xs):
        def tap(s):
            w = 0.5 * cw_ref[CONV_WIDTH - 1 - s, j]
            if in_xs:
                rows = xs[bb * n_stream + j, lead - s:lead - s + SUPER, :]
            else:
                rows = x_refs[j][bb, pl.ds(r0 - s, SUPER), :]
            return rows * jnp.tile(w, (SUPER // SUBLANES, 1))
        h = tap(0)
        for s in range(1, CONV_WIDTH):
            h = h + tap(s)
        return h * jnp.tanh(h) + h

    def l2n(x):
        return x * lax.rsqrt(jnp.sum(x * x, axis=-1, keepdims=True) + EPS)

    def lane_bcast(x, j):
        return jnp.broadcast_to(x[:, j:j + 1], x.shape)

    def row_bcast(x, i, n_rows):
        return jnp.broadcast_to(x[i:i + 1, :], (n_rows, x.shape[1]))

    def gate_rows(bb, r0):
        pre = ba_ref[bb, pl.ds(r0, SUPER), :].T[0:SUBLANES, :]
        xg = pre + dt_b
        g = a_coef * (jnp.maximum(xg, 0.0) + jnp.log(1.0 + jnp.exp(-jnp.abs(xg))))
        g_hi = g.astype(BF16)
        g_r = g - g_hi.astype(F32)
        g_mid = g_r.astype(BF16)
        g_lo = (g_r - g_mid.astype(F32)).astype(BF16)
        cs = jnp.dot(jnp.concatenate([g_hi, g_mid, g_lo], axis=0), cum_mat(),
                     preferred_element_type=F32)
        gc_r = cs[0:SUBLANES] + cs[SUBLANES:2 * SUBLANES] + cs[2 * SUBLANES:3 * SUBLANES]
        gates = jnp.where(is_beta_row, _sigmoid(pre), gc_r)
        gates_c = jnp.concatenate([gates, jnp.zeros((SUPER - SUBLANES, SUPER), F32)], axis=0).T
        return gc_r, gates_c

    def body(it, carry, first=False):
        if first:
            r0s = [u * SUPER for u in range(DN_UNROLL)]
        else:
            r0s = [pl.multiple_of((it * DN_UNROLL + u) * SUPER, SUPER) for u in range(DN_UNROLL)]
        in_xs = [first and u == 0 for u in range(DN_UNROLL)]
        items = [(u, bb, h) for u in range(DN_UNROLL) for bb in range(nb) for h in heads]
        idx = range(len(items))
        gate = {(u, bb): gate_rows(bb, r0s[u]) for u in range(DN_UNROLL) for bb in range(nb)}

        q = [l2n(conv_silu(bb, h, r0s[u], in_xs[u])) * (DN_HEAD_DIM ** -0.5) for u, bb, h in items]
        k = [l2n(conv_silu(bb, nh + h, r0s[u], in_xs[u])) for u, bb, h in items]
        v = [conv_silu(bb, 2 * nh + h, r0s[u], in_xs[u]) for u, bb, h in items]

        beta = [lane_bcast(gate[u, bb][1], h) for u, bb, h in items]
        gc = [lane_bcast(gate[u, bb][1], nh + h) for u, bb, h in items]
        decay = [jnp.exp2(gc[i] - row_bcast(gate[u, bb][0], nh + h, SUPER) + neg_outside_incl())
                 for i, (u, bb, h) in enumerate(items)]
        gc_last = [jnp.concatenate([row_bcast(gc[i], c * DN_CHUNK + DN_CHUNK - 1, DN_CHUNK)
                                    for c in range(n_chunk)], axis=0) for i in idx]
        e_gc = [jnp.exp2(gc[i]) for i in idx]
        kb = [k[i] * beta[i] for i in idx]

        kq = [_mm_nt(jnp.concatenate([kb[i], q[i]], axis=0), k[i]) for i in idx]
        neg_a = [kq[i][:SUPER] * decay[i] * neg_in_strict() for i in idx]
        attn = [(kq[i][SUPER:] * decay[i]).astype(BF16) for i in idx]
        rhs = [jnp.concatenate([v[i] * beta[i], kb[i] * e_gc[i]], axis=1).astype(BF16) for i in idx]

        x1 = [neg_a[i] * in_sub() for i in idx]
        neg_l = [(neg_a[i] - x1[i]).astype(BF16) for i in idx]
        x1b = [x1[i].astype(BF16) for i in idx]
        x2 = [_mm(x1b[i], x1b[i]) for i in idx]
        x2b = [x2[i].astype(BF16) for i in idx]
        x4 = [_mm(x2b[i], x2b[i]) for i in idx]
        p1 = [eye() + x1[i] + x2[i] + _mm(x1b[i], x2b[i]) for i in idx]
        t_d = [p1[i] + _mm(p1[i], x4[i]) for i in idx]
        t_db = [t_d[i].astype(BF16) for i in idx]
        y1 = [_mm(t_db[i], neg_l[i]) for i in idx]
        td_rhs = [_mm(t_db[i], rhs[i]).astype(BF16) for i in idx]
        y1b = [y1[i].astype(BF16) for i in idx]
        y2 = [_mm(y1b[i], y1b[i]) for i in idx]
        y2b = [y2[i].astype(BF16) for i in idx]
        y4 = [_mm(y2b[i], y2b[i]) for i in idx]
        q1 = [eye() + y1[i] + y2[i] + _mm(y1b[i], y2b[i]) for i in idx]
        q2 = [q1[i] + _mm(q1[i], y4[i]) for i in idx]
        y_pow, covered = y4, 8
        while covered < DN_CHUNK // SUB:
            y_pow_b = [y_pow[i].astype(BF16) for i in idx]
            y_pow = [_mm(y_pow_b[i], y_pow_b[i]) for i in idx]
            q2 = [q2[i] + _mm(q2[i], y_pow[i]) for i in idx]
            covered *= 2

        uw = [_mm(q2[i], td_rhs[i]).astype(BF16) for i in idx]
        aw = [_mm(attn[i], uw[i]) for i in idx]
        q_t = [(q[i] * e_gc[i] - aw[i][:, LANES:]).astype(BF16) for i in idx]
        kt_t = [(k[i] * jnp.exp2(gc_last[i] - gc[i])).T.astype(BF16) for i in idx]
        kw = [[_mm(kt_t[i] * chunk_cols(c) if n_chunk > 1 else kt_t[i], uw[i])
               for c in range(n_chunk)] for i in idx]

        chains = [(bb, h) for bb in range(nb) for h in heads]
        state = [state_s[bb * nh + h] for bb, h in chains]
        for u in range(DN_UNROLL):
            outs = [[] for _ in chains]
            for c in range(n_chunk):
                sl = slice(c * DN_CHUNK, (c + 1) * DN_CHUNK)
                s_bf = [st.astype(BF16) for st in state]
                for j in range(len(chains)):
                    i = u * len(chains) + j
                    outs[j].append(_mm(q_t[i][sl], s_bf[j]) + aw[i][sl, :LANES])
                for j in range(len(chains)):
                    i = u * len(chains) + j
                    g_last = jnp.exp2(row_bcast(gc[i], c * DN_CHUNK + DN_CHUNK - 1, LANES))
                    state[j] = (state[j] * g_last - _mm(kw[i][c][:, LANES:], s_bf[j])
                                + kw[i][c][:, :LANES])
            rows = pl.ds(r0s[u], SUPER)
            for j, (bb, h) in enumerate(chains):
                o = jnp.concatenate(outs[j], axis=0)
                ms = jnp.mean(o * o, axis=-1, keepdims=True)
                z = z_ref[bb, rows, h * LANES:(h + 1) * LANES].astype(F32)
                y = o * lax.rsqrt(ms + EPS) * z
                o_ref[bb, rows, h * LANES:(h + 1) * LANES] = y.astype(o_ref.dtype)
        for j, (bb, h) in enumerate(chains):
            state_s[bb * nh + h] = state[j]
        return carry

    body(0, 0, first=True)
    lax.fori_loop(1, tile // (SUPER * DN_UNROLL), body, 0)
    for bb in range(nb):
        for j in range(n_stream):
            xs[bb * n_stream + j, 0:lead, :] = x_refs[j][bb, tile - lead:tile, :]


def _dn_constants():
    ri = np.arange(SUPER)[:, None]
    ci = np.arange(SUPER)[None, :]
    same_chunk = (ri // DN_CHUNK) == (ci // DN_CHUNK)
    cf = np.stack([
        (ri == ci).astype(np.float32),
        np.where(same_chunk & (ri >= ci), 0.0, NEG).astype(np.float32),
        -(same_chunk & (ri > ci)).astype(np.float32),
        ((ri // SUB) == (ci // SUB)).astype(np.float32)])
    cb = np.stack([(same_chunk & (ri <= ci)).astype(np.float32)]
                  + [np.broadcast_to((ci // DN_CHUNK) == c, (SUPER, SUPER)).astype(np.float32)
                     for c in range(SUPER // DN_CHUNK)])
    return jnp.asarray(cf), jnp.asarray(cb, dtype=BF16)


def _deltanet(dn_qkv, dn_z, ba, conv_w, a_log, dt_bias):
    bsz, seq, _ = dn_qkv.shape
    tile = DN_TILE
    assert bsz % DN_BATCH == 0
    n_stream = 3 * DN_HEADS
    seq_tile = lambda b, t: (b, t, 0)
    fixed = lambda b, t: (0, 0)
    cf, cb = _dn_constants()
    cw_tiles = jnp.broadcast_to(conv_w.astype(F32).reshape(CONV_WIDTH, n_stream, 1, LANES),
                                (CONV_WIDTH, n_stream, SUBLANES, LANES))

    def slab(j):
        return pl.BlockSpec((DN_BATCH, tile, LANES), lambda b, t: (b, t, j))

    def whole(a):
        return pl.BlockSpec(a.shape, lambda b, t: (0,) * a.ndim)

    return pl.pallas_call(
        functools.partial(_dn_kernel, tile=tile),
        grid=(bsz // DN_BATCH, seq // tile),
        in_specs=[slab(j) for j in range(n_stream)]
        + [whole(cw_tiles),
           pl.BlockSpec((DN_BATCH, tile, LANES), seq_tile),
           pl.BlockSpec((DN_BATCH, tile, D_DN), seq_tile),
           pl.BlockSpec((SUBLANES, LANES), fixed),
           pl.BlockSpec((SUBLANES, LANES), fixed),
           whole(cf),
           whole(cb)],
        out_specs=pl.BlockSpec((DN_BATCH, tile, D_DN), seq_tile),
        out_shape=jax.ShapeDtypeStruct((bsz, seq, D_DN), BF16),
        scratch_shapes=[pltpu.VMEM((DN_BATCH * DN_HEADS, DN_HEAD_DIM, DN_HEAD_DIM), F32),
                        pltpu.VMEM((DN_BATCH * n_stream, SUBLANES + SUPER, LANES), F32)],
        compiler_params=pltpu.CompilerParams(
            dimension_semantics=("arbitrary", "arbitrary"), vmem_limit_bytes=VMEM_LIMIT),
        name="deltanet",
    )(*([dn_qkv] * n_stream), cw_tiles, ba, dn_z, a_log, dt_bias, cf, cb)


def _part_pitch(seq, r):
    return seq // r + 1 if r >= 2 * SUBLANES else None


def _part_rows(seq, r):
    pitch = _part_pitch(seq, r)
    return seq if pitch is None else -(-(r * pitch) // SUBLANES) * SUBLANES


def _att_kernel(*refs, seq):
    n_pat = len(DILATED_PATTERNS)
    qkv_refs = [refs[3 * p:3 * p + 3] for p in range(n_pat)]
    g_ref, rel_ref, bkt_ref, o_ref = refs[3 * n_pat:3 * n_pat + 4]
    part_s = refs[3 * n_pat + 4:3 * n_pat + 4 + n_pat]
    bias_ref = refs[3 * n_pat + 4 + n_pat]

    @pl.when(pl.program_id(1) == 0)
    def _():
        _fill_bias_tables(rel_ref, bkt_ref, bias_ref, pl.program_id(0))

    tile = ATT_MERGE_ROWS
    lane = lax.broadcasted_iota(jnp.int32, (BLK, LANES), 1)
    head0 = lane < ATT_HEAD_DIM
    ones_v = jnp.ones((2 * BLK, LANES), BF16)
    zero_q = jnp.zeros((BLK, LANES), BF16)

    def store_part(p, r, res, n, vals):
        pitch = _part_pitch(seq, r)
        if pitch is None:
            tok0 = res + n * (BLK * r)
            rows = pl.ds(tok0, BLK) if r == 1 else pl.ds(tok0, BLK, stride=r)
        else:
            rows = pl.ds(res * pitch + n * BLK, BLK)
        for a, val in enumerate(vals):
            part_s[p][a, rows, :] = val

    def load_part(p, r, a, t0):
        pitch = _part_pitch(seq, r)
        if pitch is None:
            return part_s[p][a, pl.ds(t0, tile), :]
        m0 = t0 // r
        return jnp.concatenate([part_s[p][a, pl.ds(m0 + jj, r, stride=pitch), :]
                                for jj in range(tile // r)], axis=0)

    def group(p, r, n_blk, bi0):
        q_ref, k_ref, v_ref = qkv_refs[p]
        idx = []
        for g in range(ATT_GROUP):
            bi = bi0 + g
            base = pl.multiple_of(bi * BLK, BLK)
            kbase = pl.multiple_of(jnp.maximum(base - BLK, 0), BLK)
            n = bi % n_blk
            variant = jnp.where(bi == 0, 2, jnp.where(n == 0, 1, 0))
            idx.append((base, kbase, variant, (bi // n_blk, n)))
        scores = []
        for base, kbase, variant, _ in idx:
            q = q_ref[0, pl.ds(base, BLK), :]
            q2 = jnp.concatenate([jnp.where(head0, q, zero_q), jnp.where(head0, zero_q, q)], axis=0)
            k = k_ref[0, pl.ds(kbase, 2 * BLK), :]
            bias = jnp.concatenate([bias_ref[p, 0, variant], bias_ref[p, 1, variant]], axis=0)
            scores.append(lax.dot_general(q2, k, (((1,), (1,)), ((), ())),
                                          preferred_element_type=F32) + bias)
        maxes, probs = [], []
        for s in scores:
            m = jnp.max(s, axis=-1, keepdims=True)
            maxes.append(m)
            probs.append(jnp.exp2(s - m).astype(BF16))
        for (base, kbase, variant, (res, n)), m, e in zip(idx, maxes, probs):
            v2 = jnp.concatenate([v_ref[0, pl.ds(kbase, 2 * BLK), :], ones_v], axis=1)
            pv = jnp.dot(e, v2, preferred_element_type=F32)
            store_part(p, r, res, n,
                       (jnp.where(head0, m[:BLK], m[BLK:]),
                        jnp.where(head0, pv[:BLK, LANES:], pv[BLK:, LANES:]),
                        jnp.where(head0, pv[:BLK, :LANES], pv[BLK:, :LANES])))

    for p, (window, r) in enumerate(DILATED_PATTERNS):
        n_blk = seq // (r * BLK)

        def group_body(i, carry, p=p, r=r, n_blk=n_blk):
            group(p, r, n_blk, i * ATT_GROUP)
            return carry
        lax.fori_loop(0, seq // (BLK * ATT_GROUP), group_body, 0)

    def out_body(i, carry):
        t0 = pl.multiple_of(i * tile, tile)
        rows = pl.ds(t0, tile)
        part = [[load_part(p, r, a, t0) for a in range(3)]
                for p, (_, r) in enumerate(DILATED_PATTERNS)]
        m = functools.reduce(jnp.maximum, [pt[0] for pt in part])
        w = [jnp.exp2(pt[0] - m) for pt in part]
        den = sum(wp * pt[1] for wp, pt in zip(w, part))
        num = sum(wp * pt[2] for wp, pt in zip(w, part))
        g = g_ref[0, rows, :].astype(F32)
        o_ref[0, rows, :] = (num / den * g).astype(o_ref.dtype)
        return carry
    lax.fori_loop(0, seq // tile, out_body, 0)


def _dilated_attn(att_perm, att_g, rel_bias):
    bsz, seq, _ = att_g.shape
    n_pair = ATT_HEADS // 2
    n_pat = len(DILATED_PATTERNS)
    bkt = jnp.asarray(_bucket_maps())

    def col(off):
        return pl.BlockSpec((1, seq, LANES), lambda j, b: (b, 0, off + j))

    qkv_specs, qkv_args = [], []
    for a in att_perm:
        qkv_specs += [col(0), col(n_pair), col(2 * n_pair)]
        qkv_args += [a, a, a]
    return pl.pallas_call(
        functools.partial(_att_kernel, seq=seq),
        grid=(n_pair, bsz),
        in_specs=qkv_specs
        + [col(0),
           pl.BlockSpec(memory_space=pltpu.SMEM),
           pl.BlockSpec(bkt.shape, lambda j, b: (0, 0, 0))],
        out_specs=col(0),
        out_shape=jax.ShapeDtypeStruct((bsz, seq, D_ATT), BF16),
        scratch_shapes=[pltpu.VMEM((3, _part_rows(seq, r), LANES), F32)
                        for _, r in DILATED_PATTERNS]
        + [pltpu.VMEM((n_pat, 2, N_BIAS_VARIANTS, BLK, 2 * BLK), F32)],
        compiler_params=pltpu.CompilerParams(
            dimension_semantics=("arbitrary", "arbitrary"), vmem_limit_bytes=VMEM_LIMIT),
        name="dilated_attn",
    )(*qkv_args, att_g, rel_bias, bkt)


def _out_proj_kernel(x_ref, ydn_ref, yatt_ref, w_ref, o_ref):
    o_ref[...] = (x_ref[...]
                  + jnp.dot(ydn_ref[...], w_ref[0:D_DN, :], preferred_element_type=F32)
                  + jnp.dot(yatt_ref[...], w_ref[D_DN:D_MIX, :], preferred_element_type=F32))


def _out_proj(x2d, y_dn, y_att, w_out):
    n = x2d.shape[0]
    tm = OUT_PROJ_ROWS
    row = lambda i: (i, 0)
    return pl.pallas_call(
        _out_proj_kernel,
        grid=(n // tm,),
        in_specs=[pl.BlockSpec((tm, D_MODEL), row),
                  pl.BlockSpec((tm, D_DN), row),
                  pl.BlockSpec((tm, D_ATT), row),
                  pl.BlockSpec((D_MIX, D_MODEL), lambda i: (0, 0))],
        out_specs=pl.BlockSpec((tm, D_MODEL), row),
        out_shape=jax.ShapeDtypeStruct((n, D_MODEL), F32),
        compiler_params=pltpu.CompilerParams(
            dimension_semantics=("arbitrary",), vmem_limit_bytes=VMEM_LIMIT),
        name="out_proj",
    )(x2d, y_dn, y_att, w_out)


def _gate_rows(v):
    col = jnp.pad(v.reshape(-1, 1).astype(F32), ((DN_HEADS, SUBLANES - 2 * DN_HEADS), (0, 0)))
    return jnp.broadcast_to(col, (SUBLANES, LANES))


def _layer(x, norm_w, w_in, conv_w, a_log, dt_bias, dn_norm_w, q_norm_w, k_norm_w, rel_bias, w_out):
    bsz, seq, _ = x.shape
    x2d = x.reshape(bsz * seq, D_MODEL)
    c_dn = 4 * D_DN
    c_ba = c_dn + 2 * DN_HEADS
    w_ba = jnp.pad(w_in[:, c_dn:c_ba], ((0, 0), (0, LANES - 2 * DN_HEADS)))
    w_all = jnp.concatenate([w_in[:, :c_dn], w_in[:, c_ba:], w_ba], axis=1).astype(BF16)

    qw = jnp.tile(q_norm_w.reshape(1, ATT_HEAD_DIM).astype(F32), (1, 2)) * (ATT_HEAD_DIM ** -0.5 * LOG2E)
    kw = jnp.tile(k_norm_w.reshape(1, ATT_HEAD_DIM).astype(F32), (1, 2))
    dnw = jnp.tile(dn_norm_w.reshape(1, DN_HEAD_DIM).astype(F32), (1, DN_HEADS))
    dn_qkv, dn_z, ba, att_g, *att_perm = _in_proj(x, norm_w.reshape(1, D_MODEL), w_all, qw, kw, dnw)
    att_perm = [a.reshape(bsz, seq, 3 * D_ATT) for a in att_perm]

    y_dn = _deltanet(dn_qkv, dn_z, ba, conv_w, _gate_rows(a_log), _gate_rows(dt_bias))
    y_att = _dilated_attn(att_perm, att_g, rel_bias)
    out = _out_proj(x2d, y_dn.reshape(bsz * seq, D_DN), y_att.reshape(bsz * seq, D_ATT),
                    w_out.astype(BF16))
    return out.reshape(bsz, seq, D_MODEL)


def kernel(x, norm_w, w_in, conv_w, a_log, dt_bias, dn_norm_w, q_norm_w, k_norm_w, rel_bias, w_out):
    rel_bias = rel_bias.astype(F32)
    for layer in range(norm_w.shape[0]):
        x = _layer(x, norm_w[layer], w_in[layer], conv_w[layer], a_log[layer], dt_bias[layer],
                   dn_norm_w[layer], q_norm_w[layer], k_norm_w[layer], rel_bias, w_out[layer])
    return x
```

```python
import functools
import math

import numpy as np
import jax
import jax.numpy as jnp
from jax import lax
from jax.experimental import pallas as pl
from jax.experimental.pallas import tpu as pltpu

D_MODEL = 1024
D_DN = 512
DN_HEADS = 4
DN_HEAD_DIM = 128
CONV_WIDTH = 4
D_ATT = 512
ATT_HEADS = 8
ATT_HEAD_DIM = 64
DILATED_PATTERNS = ((128, 1), (512, 4), (2048, 16))
N_BUCKETS = 32
MAX_DISTANCE = 2048
D_MIX = D_DN + D_ATT
EPS = 1e-6

LANES = 128
SUBLANES = 8
BLK = 128
ATT_GROUP = 8
SUPER = 128
DN_CHUNK = 128
DN_UNROLL = 1
DN_BATCH = 4
SUB = 8
NEG = -1e30
LOG2E = math.log2(math.e)
N_BIAS_VARIANTS = 3
IN_PROJ_ROWS = 512
DN_TILE = 256
ATT_MERGE_ROWS = 1024
OUT_PROJ_ROWS = 1024
V7X_VMEM_BYTES = 64 * 1024 * 1024
VMEM_LIMIT = V7X_VMEM_BYTES * 7 // 8

F32 = jnp.float32
BF16 = jnp.bfloat16


def _mm(a, b):
    return jnp.dot(a.astype(BF16), b.astype(BF16), preferred_element_type=F32)


def _mm_nt(a, b):
    return lax.dot_general(a.astype(BF16), b.astype(BF16), (((1,), (1,)), ((), ())),
                           preferred_element_type=F32)


def _sigmoid(x):
    return 0.5 * jnp.tanh(0.5 * x) + 0.5


def _in_proj_kernel(x_ref, nw_ref, wdn_ref, watt_ref, wba_ref, qw_ref, kw_ref, dnw_ref,
                    dnqkv_ref, dnz_ref, ba_ref, attg_ref, *att_refs_and_scratch):
    att_refs = att_refs_and_scratch[:len(DILATED_PATTERNS)]
    att_s, perm_s = att_refs_and_scratch[len(DILATED_PATTERNS):]
    tm = x_ref.shape[1]
    x = x_ref[0]
    ms = jnp.mean(x * x, axis=-1, keepdims=True)
    h = (x * lax.rsqrt(ms + EPS) * nw_ref[...]).astype(BF16)
    c_dn = 3 * D_DN
    c_att = 3 * D_ATT

    att = jnp.dot(h, watt_ref[:, 0:c_att], preferred_element_type=F32)
    head0 = lax.broadcasted_iota(jnp.int32, (tm, LANES), 1) < ATT_HEAD_DIM
    n_slab = 3 * D_ATT // LANES
    for j in range(n_slab):
        xs = att[:, j * LANES:(j + 1) * LANES]
        if j < 2 * D_ATT // LANES:
            is_q = j < D_ATT // LANES
            x2 = xs * xs
            s0 = jnp.sum(jnp.where(head0, x2, 0.0), axis=-1, keepdims=True)
            s1 = jnp.sum(jnp.where(head0, 0.0, x2), axis=-1, keepdims=True)
            ss = jnp.where(head0, s0, s1)
            wn = qw_ref[...] if is_q else kw_ref[...]
            xs = xs * lax.rsqrt(ss * (1.0 / ATT_HEAD_DIM) + EPS) * wn
        att_s[j] = xs
    for j in range(n_slab):
        cols = slice(j * LANES, (j + 1) * LANES)
        src_ref, r_prev = att_s.at[j], 1
        for level, (o_ref, (_, r)) in enumerate(zip(att_refs, DILATED_PATTERNS)):
            step = r // r_prev
            rows_per = tm // r
            keep = level + 1 < len(DILATED_PATTERNS) and r > 1
            for res in range(r):
                if r == 1:
                    piece = src_ref[...]
                else:
                    start = (res % r_prev) * (tm // r_prev) + res // r_prev
                    piece = src_ref[pl.ds(start, rows_per, stride=step), :]
                o_ref[0, res, :, cols] = piece.astype(BF16)
                if keep:
                    perm_s[j, pl.ds(res * rows_per, rows_per), :] = piece
            if keep:
                src_ref, r_prev = perm_s.at[j], r

    dnqkv_ref[0] = jnp.dot(h, wdn_ref[:, 0:c_dn], preferred_element_type=F32)
    z = jnp.dot(h, wdn_ref[:, c_dn:c_dn + D_DN], preferred_element_type=F32)
    dnz_ref[0] = (z * _sigmoid(z) * dnw_ref[...]).astype(BF16)
    gate = jnp.dot(h, watt_ref[:, c_att:c_att + D_ATT], preferred_element_type=F32)
    attg_ref[0] = (gate * _sigmoid(gate)).astype(BF16)
    ba_ref[0] = jnp.dot(h, wba_ref[...], preferred_element_type=F32)


def _in_proj(x, norm_w, w_dn, w_att, w_ba, qw, kw, dnw):
    bsz, seq, _ = x.shape
    tm = IN_PROJ_ROWS
    row = lambda b, i: (b, i, 0)
    fixed = lambda b, i: (0, 0)
    att_specs = [pl.BlockSpec((1, r, tm // r, 3 * D_ATT), lambda b, i: (b, 0, i, 0))
                 for _, r in DILATED_PATTERNS]
    att_shapes = [jax.ShapeDtypeStruct((bsz, r, seq // r, 3 * D_ATT), BF16)
                  for _, r in DILATED_PATTERNS]
    return pl.pallas_call(
        _in_proj_kernel,
        grid=(bsz, seq // tm),
        in_specs=[pl.BlockSpec((1, tm, D_MODEL), row),
                  pl.BlockSpec((1, D_MODEL), fixed),
                  pl.BlockSpec(w_dn.shape, fixed),
                  pl.BlockSpec(w_att.shape, fixed),
                  pl.BlockSpec(w_ba.shape, fixed),
                  pl.BlockSpec((1, LANES), fixed),
                  pl.BlockSpec((1, LANES), fixed),
                  pl.BlockSpec((1, D_DN), fixed)],
        out_specs=[pl.BlockSpec((1, tm, 3 * D_DN), row),
                   pl.BlockSpec((1, tm, D_DN), row),
                   pl.BlockSpec((1, tm, LANES), row),
                   pl.BlockSpec((1, tm, D_ATT), row)] + att_specs,
        out_shape=[jax.ShapeDtypeStruct((bsz, seq, 3 * D_DN), F32),
                   jax.ShapeDtypeStruct((bsz, seq, D_DN), BF16),
                   jax.ShapeDtypeStruct((bsz, seq, LANES), F32),
                   jax.ShapeDtypeStruct((bsz, seq, D_ATT), BF16)] + att_shapes,
        scratch_shapes=[pltpu.VMEM((3 * D_ATT // LANES, tm, LANES), F32),
                        pltpu.VMEM((3 * D_ATT // LANES, tm, LANES), F32)],
        compiler_params=pltpu.CompilerParams(
            dimension_semantics=("arbitrary", "arbitrary"), vmem_limit_bytes=VMEM_LIMIT),
        name="in_proj",
    )(x, norm_w, w_dn, w_att, w_ba, qw, kw, dnw)


def _t5_bucket(dist):
    max_exact = N_BUCKETS // 2
    d = np.maximum(dist, 1).astype(np.float64)
    large = max_exact + (np.log(d / max_exact) / math.log(MAX_DISTANCE / max_exact)
                         * (N_BUCKETS - max_exact)).astype(np.int32)
    large = np.minimum(large, N_BUCKETS - 1)
    return np.where(dist < max_exact, dist, large).astype(np.int32)


def _bucket_maps():
    maps = []
    for window, r in DILATED_PATTERNS:
        w_steps = window // r
        assert w_steps == BLK
        qi = np.arange(BLK)[:, None]
        kj = np.arange(2 * BLK)[None, :]
        step = qi - kj + BLK
        band = (step >= 0) & (step <= w_steps)
        buckets = _t5_bucket(np.clip(step, 0, None) * r)
        maps.append(np.where(band, buckets, -1).astype(np.int32))
    return np.stack(maps)


def _fill_bias_tables(rel_ref, bkt_ref, bias_s, pair):
    col = lax.broadcasted_iota(jnp.int32, (BLK, 2 * BLK), 1)
    for p in range(len(DILATED_PATTERNS)):
        bkt = bkt_ref[p]
        for hh in range(2):
            h = 2 * pair + hh
            acc = jnp.full(bkt.shape, NEG, F32)
            for b in range(N_BUCKETS):
                acc = jnp.where(bkt == b, rel_ref[h, b] * LOG2E, acc)
            bias_s[p, hh, 0] = acc
            bias_s[p, hh, 1] = jnp.where(col < BLK, NEG, acc)
            bias_s[p, hh, 2] = jnp.concatenate([acc[:, BLK:], jnp.full((BLK, BLK), NEG, F32)], axis=1)


def _dn_kernel(*refs, tile):
    nh = DN_HEADS
    n_stream = 3 * nh
    x_refs = refs[:n_stream]
    (cw_ref, ba_ref, z_ref, alog_ref, dtb_ref, cf_ref, cb_ref,
     o_ref, state_s, xs) = refs[n_stream:]
    n_chunk = SUPER // DN_CHUNK
    heads = range(nh)
    lead = SUBLANES

    nb = DN_BATCH

    @pl.when(pl.program_id(1) == 0)
    def _():
        state_s[...] = jnp.zeros(state_s.shape, F32)
        xs[:, 0:lead, :] = jnp.zeros((nb * n_stream, lead, LANES), F32)

    for bb in range(nb):
        for j in range(n_stream):
            xs[bb * n_stream + j, lead:lead + SUPER, :] = x_refs[j][bb, 0:SUPER, :]

    eye = lambda: cf_ref[0]
    neg_outside_incl = lambda: cf_ref[1]
    neg_in_strict = lambda: cf_ref[2]
    in_sub = lambda: cf_ref[3]
    cum_mat = lambda: cb_ref[0]
    chunk_cols = lambda c: cb_ref[1 + c]
    is_beta_row = lax.broadcasted_iota(jnp.int32, (SUBLANES, SUPER), 0) < nh

    a_coef = -LOG2E * jnp.exp(alog_ref[...])
    dt_b = dtb_ref[...]

    def conv_silu(bb, j, r0, in_xs):
        def tap(s):
            w = 0.5 * cw_ref[CONV_WIDTH - 1 - s, j]
            if in_xs:
                rows = xs[bb * n_stream + j, lead - s:lead - s + SUPER, :]
            else:
                rows = x_refs[j][bb, pl.ds(r0 - s, SUPER), :]
            return rows * jnp.tile(w, (SUPER // SUBLANES, 1))
        h = tap(0)
        for s in range(1, CONV_WIDTH):
            h = h + tap(s)
        return h * jnp.tanh(h) + h

    def l2n(x):
        return x * lax.rsqrt(jnp.sum(x * x, axis=-1, keepdims=True) + EPS)

    def lane_bcast(x, j):
        return jnp.broadcast_to(x[:, j:j + 1], x.shape)

    def row_bcast(x, i, n_rows):
        return jnp.broadcast_to(x[i:i + 1, :], (n_rows, x.shape[1]))

    def gate_rows(bb, r0):
        pre = ba_ref[bb, pl.ds(r0, SUPER), :].T[0:SUBLANES, :]
        xg = pre + dt_b
        g = a_coef * (jnp.maximum(xg, 0.0) + jnp.log(1.0 + jnp.exp(-jnp.abs(xg))))
        g_hi = g.astype(BF16)
        g_r = g - g_hi.astype(F32)
        g_mid = g_r.astype(BF16)
        g_lo = (g_r - g_mid.astype(F32)).astype(BF16)
        cs = jnp.dot(jnp.concatenate([g_hi, g_mid, g_lo], axis=0), cum_mat(),
                     preferred_element_type=F32)
        gc_r = cs[0:SUBLANES] + cs[SUBLANES:2 * SUBLANES] + cs[2 * SUBLANES:3 * SUBLANES]
        gates = jnp.where(is_beta_row, _sigmoid(pre), gc_r)
        gates_c = jnp.concatenate([gates, jnp.zeros((SUPER - SUBLANES, SUPER), F32)], axis=0).T
        return gc_r, gates_c

    def body(it, carry, first=False):
        if first:
            r0s = [u * SUPER for u in range(DN_UNROLL)]
        else:
            r0s = [pl.multiple_of((it * DN_UNROLL + u) * SUPER, SUPER) for u in range(DN_UNROLL)]
        in_xs = [first and u == 0 for u in range(DN_UNROLL)]
        items = [(u, bb, h) for u in range(DN_UNROLL) for bb in range(nb) for h in heads]
        idx = range(len(items))
        gate = {(u, bb): gate_rows(bb, r0s[u]) for u in range(DN_UNROLL) for bb in range(nb)}

        q = [l2n(conv_silu(bb, h, r0s[u], in_xs[u])) * (DN_HEAD_DIM ** -0.5) for u, bb, h in items]
        k = [l2n(conv_silu(bb, nh + h, r0s[u], in_xs[u])) for u, bb, h in items]
        v = [conv_silu(bb, 2 * nh + h, r0s[u], in_xs[u]) for u, bb, h in items]

        beta = [lane_bcast(gate[u, bb][1], h) for u, bb, h in items]
        gc = [lane_bcast(gate[u, bb][1], nh + h) for u, bb, h in items]
        decay = [jnp.exp2(gc[i] - row_bcast(gate[u, bb][0], nh + h, SUPER) + neg_outside_incl())
                 for i, (u, bb, h) in enumerate(items)]
        gc_last = [jnp.concatenate([row_bcast(gc[i], c * DN_CHUNK + DN_CHUNK - 1, DN_CHUNK)
                                    for c in range(n_chunk)], axis=0) for i in idx]
        e_gc = [jnp.exp2(gc[i]) for i in idx]
        kb = [k[i] * beta[i] for i in idx]

        kq = [_mm_nt(jnp.concatenate([kb[i], q[i]], axis=0), k[i]) for i in idx]
        neg_a = [kq[i][:SUPER] * decay[i] * neg_in_strict() for i in idx]
        attn = [(kq[i][SUPER:] * decay[i]).astype(BF16) for i in idx]
        rhs = [jnp.concatenate([v[i] * beta[i], kb[i] * e_gc[i]], axis=1).astype(BF16) for i in idx]

        x1 = [neg_a[i] * in_sub() for i in idx]
        neg_l = [(neg_a[i] - x1[i]).astype(BF16) for i in idx]
        x1b = [x1[i].astype(BF16) for i in idx]
        x2 = [_mm(x1b[i], x1b[i]) for i in idx]
        x2b = [x2[i].astype(BF16) for i in idx]
        x4 = [_mm(x2b[i], x2b[i]) for i in idx]
        p1 = [eye() + x1[i] + x2[i] + _mm(x1b[i], x2b[i]) for i in idx]
        t_d = [p1[i] + _mm(p1[i], x4[i]) for i in idx]
        t_db = [t_d[i].astype(BF16) for i in idx]
        y1 = [_mm(t_db[i], neg_l[i]) for i in idx]
        td_rhs = [_mm(t_db[i], rhs[i]).astype(BF16) for i in idx]
        y1b = [y1[i].astype(BF16) for i in idx]
        y2 = [_mm(y1b[i], y1b[i]) for i in idx]
        y2b = [y2[i].astype(BF16) for i in idx]
        y4 = [_mm(y2b[i], y2b[i]) for i in idx]
        q1 = [eye() + y1[i] + y2[i] + _mm(y1b[i], y2b[i]) for i in idx]
        q2 = [q1[i] + _mm(q1[i], y4[i]) for i in idx]
        y_pow, covered = y4, 8
        while covered < DN_CHUNK // SUB:
            y_pow_b = [y_pow[i].astype(BF16) for i in idx]
            y_pow = [_mm(y_pow_b[i], y_pow_b[i]) for i in idx]
            q2 = [q2[i] + _mm(q2[i], y_pow[i]) for i in idx]
            covered *= 2

        uw = [_mm(q2[i], td_rhs[i]).astype(BF16) for i in idx]
        aw = [_mm(attn[i], uw[i]) for i in idx]
        q_t = [(q[i] * e_gc[i] - aw[i][:, LANES:]).astype(BF16) for i in idx]
        kt_t = [(k[i] * jnp.exp2(gc_last[i] - gc[i])).T.astype(BF16) for i in idx]
        kw = [[_mm(kt_t[i] * chunk_cols(c) if n_chunk > 1 else kt_t[i], uw[i])
               for c in range(n_chunk)] for i in idx]

        chains = [(bb, h) for bb in range(nb) for h in heads]
        state = [state_s[bb * nh + h] for bb, h in chains]
        for u in range(DN_UNROLL):
            outs = [[] for _ in chains]
            for c in range(n_chunk):
                sl = slice(c * DN_CHUNK, (c + 1) * DN_CHUNK)
                s_bf = [st.astype(BF16) for st in state]
                for j in range(len(chains)):
                    i = u * len(chains) + j
                    outs[j].append(_mm(q_t[i][sl], s_bf[j]) + aw[i][sl, :LANES])
                for j in range(len(chains)):
                    i = u * len(chains) + j
                    g_last = jnp.exp2(row_bcast(gc[i], c * DN_CHUNK + DN_CHUNK - 1, LANES))
                    state[j] = (state[j] * g_last - _mm(kw[i][c][:, LANES:], s_bf[j])
                                + kw[i][c][:, :LANES])
            rows = pl.ds(r0s[u], SUPER)
            for j, (bb, h) in enumerate(chains):
                o = jnp.concatenate(outs[j], axis=0)
                ms = jnp.mean(o * o, axis=-1, keepdims=True)
                z = z_ref[bb, rows, h * LANES:(h + 1) * LANES].astype(F32)
                y = o * lax.rsqrt(ms + EPS) * z
                o_ref[bb, rows, h * LANES:(h + 1) * LANES] = y.astype(o_ref.dtype)
        for j, (bb, h) in enumerate(chains):
            state_s[bb * nh + h] = state[j]
        return carry

    body(0, 0, first=True)
    lax.fori_loop(1, tile // (SUPER * DN_UNROLL), body, 0)
    for bb in range(nb):
        for j in range(n_stream):
            xs[bb * n_stream + j, 0:lead, :] = x_refs[j][bb, tile - lead:tile, :]


def _dn_constants():
    ri = np.arange(SUPER)[:, None]
    ci = np.arange(SUPER)[None, :]
    same_chunk = (ri // DN_CHUNK) == (ci // DN_CHUNK)
    cf = np.stack([
        (ri == ci).astype(np.float32),
        np.where(same_chunk & (ri >= ci), 0.0, NEG).astype(np.float32),
        -(same_chunk & (ri > ci)).astype(np.float32),
        ((ri // SUB) == (ci // SUB)).astype(np.float32)])
    cb = np.stack([(same_chunk & (ri <= ci)).astype(np.float32)]
                  + [np.broadcast_to((ci // DN_CHUNK) == c, (SUPER, SUPER)).astype(np.float32)
                     for c in range(SUPER // DN_CHUNK)])
    return jnp.asarray(cf), jnp.asarray(cb, dtype=BF16)


def _deltanet(dn_qkv, dn_z, ba, conv_w, a_log, dt_bias):
    bsz, seq, _ = dn_qkv.shape
    tile = DN_TILE
    assert bsz % DN_BATCH == 0
    n_stream = 3 * DN_HEADS
    seq_tile = lambda b, t: (b, t, 0)
    fixed = lambda b, t: (0, 0)
    cf, cb = _dn_constants()
    cw_tiles = jnp.broadcast_to(conv_w.astype(F32).reshape(CONV_WIDTH, n_stream, 1, LANES),
                                (CONV_WIDTH, n_stream, SUBLANES, LANES))

    def slab(j):
        return pl.BlockSpec((DN_BATCH, tile, LANES), lambda b, t: (b, t, j))

    def whole(a):
        return pl.BlockSpec(a.shape, lambda b, t: (0,) * a.ndim)

    return pl.pallas_call(
        functools.partial(_dn_kernel, tile=tile),
        grid=(bsz // DN_BATCH, seq // tile),
        in_specs=[slab(j) for j in range(n_stream)]
        + [whole(cw_tiles),
           pl.BlockSpec((DN_BATCH, tile, LANES), seq_tile),
           pl.BlockSpec((DN_BATCH, tile, D_DN), seq_tile),
           pl.BlockSpec((SUBLANES, LANES), fixed),
           pl.BlockSpec((SUBLANES, LANES), fixed),
           whole(cf),
           whole(cb)],
        out_specs=pl.BlockSpec((DN_BATCH, tile, D_DN), seq_tile),
        out_shape=jax.ShapeDtypeStruct((bsz, seq, D_DN), BF16),
        scratch_shapes=[pltpu.VMEM((DN_BATCH * DN_HEADS, DN_HEAD_DIM, DN_HEAD_DIM), F32),
                        pltpu.VMEM((DN_BATCH * n_stream, SUBLANES + SUPER, LANES), F32)],
        compiler_params=pltpu.CompilerParams(
            dimension_semantics=("arbitrary", "arbitrary"), vmem_limit_bytes=VMEM_LIMIT),
        name="deltanet",
    )(*([dn_qkv] * n_stream), cw_tiles, ba, dn_z, a_log, dt_bias, cf, cb)


def _part_pitch(seq, r):
    return seq // r + 1 if r >= 2 * SUBLANES else None


def _part_rows(seq, r):
    pitch = _part_pitch(seq, r)
    return seq if pitch is None else -(-(r * pitch) // SUBLANES) * SUBLANES


def _att_kernel(*refs, seq):
    n_pat = len(DILATED_PATTERNS)
    qkv_refs = [refs[3 * p:3 * p + 3] for p in range(n_pat)]
    g_ref, rel_ref, bkt_ref, o_ref = refs[3 * n_pat:3 * n_pat + 4]
    part_s = refs[3 * n_pat + 4:3 * n_pat + 4 + n_pat]
    bias_ref = refs[3 * n_pat + 4 + n_pat]

    @pl.when(pl.program_id(1) == 0)
    def _():
        _fill_bias_tables(rel_ref, bkt_ref, bias_ref, pl.program_id(0))

    tile = ATT_MERGE_ROWS
    lane = lax.broadcasted_iota(jnp.int32, (BLK, LANES), 1)
    head0 = lane < ATT_HEAD_DIM
    ones_v = jnp.ones((2 * BLK, LANES), BF16)
    zero_q = jnp.zeros((BLK, LANES), BF16)

    def store_part(p, r, res, n, vals):
        pitch = _part_pitch(seq, r)
        if pitch is None:
            tok0 = res + n * (BLK * r)
            rows = pl.ds(tok0, BLK) if r == 1 else pl.ds(tok0, BLK, stride=r)
        else:
            rows = pl.ds(res * pitch + n * BLK, BLK)
        for a, val in enumerate(vals):
            part_s[p][a, rows, :] = val

    def load_part(p, r, a, t0):
        pitch = _part_pitch(seq, r)
        if pitch is None:
            return part_s[p][a, pl.ds(t0, tile), :]
        m0 = t0 // r
        return jnp.concatenate([part_s[p][a, pl.ds(m0 + jj, r, stride=pitch), :]
                                for jj in range(tile // r)], axis=0)

    def group(p, r, n_blk, bi0):
        q_ref, k_ref, v_ref = qkv_refs[p]
        idx = []
        for g in range(ATT_GROUP):
            bi = bi0 + g
            base = pl.multiple_of(bi * BLK, BLK)
            kbase = pl.multiple_of(jnp.maximum(base - BLK, 0), BLK)
            n = bi % n_blk
            variant = jnp.where(bi == 0, 2, jnp.where(n == 0, 1, 0))
            idx.append((base, kbase, variant, (bi // n_blk, n)))
        scores = []
        for base, kbase, variant, _ in idx:
            q = q_ref[0, pl.ds(base, BLK), :]
            q2 = jnp.concatenate([jnp.where(head0, q, zero_q), jnp.where(head0, zero_q, q)], axis=0)
            k = k_ref[0, pl.ds(kbase, 2 * BLK), :]
            bias = jnp.concatenate([bias_ref[p, 0, variant], bias_ref[p, 1, variant]], axis=0)
            scores.append(lax.dot_general(q2, k, (((1,), (1,)), ((), ())),
                                          preferred_element_type=F32) + bias)
        maxes, probs = [], []
        for s in scores:
            m = jnp.max(s, axis=-1, keepdims=True)
            maxes.append(m)
            probs.append(jnp.exp2(s - m).astype(BF16))
        for (base, kbase, variant, (res, n)), m, e in zip(idx, maxes, probs):
            v2 = jnp.concatenate([v_ref[0, pl.ds(kbase, 2 * BLK), :], ones_v], axis=1)
            pv = jnp.dot(e, v2, preferred_element_type=F32)
            store_part(p, r, res, n,
                       (jnp.where(head0, m[:BLK], m[BLK:]),
                        jnp.where(head0, pv[:BLK, LANES:], pv[BLK:, LANES:]),
                        jnp.where(head0, pv[:BLK, :LANES], pv[BLK:, :LANES])))

    for p, (window, r) in enumerate(DILATED_PATTERNS):
        n_blk = seq // (r * BLK)

        def group_body(i, carry, p=p, r=r, n_blk=n_blk):
            group(p, r, n_blk, i * ATT_GROUP)
            return carry
        lax.fori_loop(0, seq // (BLK * ATT_GROUP), group_body, 0)

    def out_body(i, carry):
        t0 = pl.multiple_of(i * tile, tile)
        rows = pl.ds(t0, tile)
        part = [[load_part(p, r, a, t0) for a in range(3)]
                for p, (_, r) in enumerate(DILATED_PATTERNS)]
        m = functools.reduce(jnp.maximum, [pt[0] for pt in part])
        w = [jnp.exp2(pt[0] - m) for pt in part]
        den = sum(wp * pt[1] for wp, pt in zip(w, part))
        num = sum(wp * pt[2] for wp, pt in zip(w, part))
        g = g_ref[0, rows, :].astype(F32)
        o_ref[0, rows, :] = (num / den * g).astype(o_ref.dtype)
        return carry
    lax.fori_loop(0, seq // tile, out_body, 0)


def _dilated_attn(att_perm, att_g, rel_bias):
    bsz, seq, _ = att_g.shape
    n_pair = ATT_HEADS // 2
    n_pat = len(DILATED_PATTERNS)
    bkt = jnp.asarray(_bucket_maps())

    def col(off):
        return pl.BlockSpec((1, seq, LANES), lambda j, b: (b, 0, off + j))

    qkv_specs, qkv_args = [], []
    for a in att_perm:
        qkv_specs += [col(0), col(n_pair), col(2 * n_pair)]
        qkv_args += [a, a, a]
    return pl.pallas_call(
        functools.partial(_att_kernel, seq=seq),
        grid=(n_pair, bsz),
        in_specs=qkv_specs
        + [col(0),
           pl.BlockSpec(memory_space=pltpu.SMEM),
           pl.BlockSpec(bkt.shape, lambda j, b: (0, 0, 0))],
        out_specs=col(0),
        out_shape=jax.ShapeDtypeStruct((bsz, seq, D_ATT), BF16),
        scratch_shapes=[pltpu.VMEM((3, _part_rows(seq, r), LANES), F32)
                        for _, r in DILATED_PATTERNS]
        + [pltpu.VMEM((n_pat, 2, N_BIAS_VARIANTS, BLK, 2 * BLK), F32)],
        compiler_params=pltpu.CompilerParams(
            dimension_semantics=("arbitrary", "arbitrary"), vmem_limit_bytes=VMEM_LIMIT),
        name="dilated_attn",
    )(*qkv_args, att_g, rel_bias, bkt)


def _out_proj_kernel(x_ref, ydn_ref, yatt_ref, w_ref, o_ref):
    o_ref[...] = (x_ref[...]
                  + jnp.dot(ydn_ref[...], w_ref[0:D_DN, :], preferred_element_type=F32)
                  + jnp.dot(yatt_ref[...], w_ref[D_DN:D_MIX, :], preferred_element_type=F32))


def _out_proj(x2d, y_dn, y_att, w_out):
    n = x2d.shape[0]
    tm = OUT_PROJ_ROWS
    row = lambda i: (i, 0)
    return pl.pallas_call(
        _out_proj_kernel,
        grid=(n // tm,),
        in_specs=[pl.BlockSpec((tm, D_MODEL), row),
                  pl.BlockSpec((tm, D_DN), row),
                  pl.BlockSpec((tm, D_ATT), row),
                  pl.BlockSpec((D_MIX, D_MODEL), lambda i: (0, 0))],
        out_specs=pl.BlockSpec((tm, D_MODEL), row),
        out_shape=jax.ShapeDtypeStruct((n, D_MODEL), F32),
        compiler_params=pltpu.CompilerParams(
            dimension_semantics=("arbitrary",), vmem_limit_bytes=VMEM_LIMIT),
        name="out_proj",
    )(x2d, y_dn, y_att, w_out)


def _gate_rows(v):
    col = jnp.pad(v.reshape(-1, 1).astype(F32), ((DN_HEADS, SUBLANES - 2 * DN_HEADS), (0, 0)))
    return jnp.broadcast_to(col, (SUBLANES, LANES))


def _layer(x, norm_w, w_in, conv_w, a_log, dt_bias, dn_norm_w, q_norm_w, k_norm_w, rel_bias, w_out):
    bsz, seq, _ = x.shape
    x2d = x.reshape(bsz * seq, D_MODEL)
    c_dn = 4 * D_DN
    c_ba = c_dn + 2 * DN_HEADS
    w_dn = w_in[:, :c_dn].astype(BF16)
    w_att = w_in[:, c_ba:].astype(BF16)
    w_ba = jnp.pad(w_in[:, c_dn:c_ba], ((0, 0), (0, LANES - 2 * DN_HEADS))).astype(BF16)

    qw = jnp.tile(q_norm_w.reshape(1, ATT_HEAD_DIM).astype(F32), (1, 2)) * (ATT_HEAD_DIM ** -0.5 * LOG2E)
    kw = jnp.tile(k_norm_w.reshape(1, ATT_HEAD_DIM).astype(F32), (1, 2))
    dnw = jnp.tile(dn_norm_w.reshape(1, DN_HEAD_DIM).astype(F32), (1, DN_HEADS))
    dn_qkv, dn_z, ba, att_g, *att_perm = _in_proj(x, norm_w.reshape(1, D_MODEL), w_dn, w_att, w_ba, qw, kw, dnw)
    att_perm = [a.reshape(bsz, seq, 3 * D_ATT) for a in att_perm]

    y_dn = _deltanet(dn_qkv, dn_z, ba, conv_w, _gate_rows(a_log), _gate_rows(dt_bias))
    y_att = _dilated_attn(att_perm, att_g, rel_bias)
    out = _out_proj(x2d, y_dn.reshape(bsz * seq, D_DN), y_att.reshape(bsz * seq, D_ATT),
                    w_out.astype(BF16))
    return out.reshape(bsz, seq, D_MODEL)


def kernel(x, norm_w, w_in, conv_w, a_log, dt_bias, dn_norm_w, q_norm_w, k_norm_w, rel_bias, w_out):
    rel_bias = rel_bias.astype(F32)
    for layer in range(norm_w.shape[0]):
        x = _layer(x, norm_w[layer], w_in[layer], conv_w[layer], a_log[layer], dt_bias[layer],
                   dn_norm_w[layer], q_norm_w[layer], k_norm_w[layer], rel_bias, w_out[layer])
    return x
```

```python
import functools
import math

import numpy as np
import jax
import jax.numpy as jnp
from jax import lax
from jax.experimental import pallas as pl
from jax.experimental.pallas import tpu as pltpu

D_MODEL = 1024
D_DN = 512
DN_HEADS = 4
DN_HEAD_DIM = 128
CONV_WIDTH = 4
D_ATT = 512
ATT_HEADS = 8
ATT_HEAD_DIM = 64
DILATED_PATTERNS = ((128, 1), (512, 4), (2048, 16))
N_BUCKETS = 32
MAX_DISTANCE = 2048
D_MIX = D_DN + D_ATT
EPS = 1e-6

LANES = 128
SUBLANES = 8
BLK = 128
ATT_GROUP = 8
SUPER = 128
DN_CHUNK = 128
DN_UNROLL = 1
DN_BATCH = 4
SUB = 8
NEG = -1e30
LOG2E = math.log2(math.e)
N_BIAS_VARIANTS = 3
IN_PROJ_ROWS = 512
DN_TILE = 256
ATT_MERGE_ROWS = 1024
OUT_PROJ_ROWS = 1024
V7X_VMEM_BYTES = 64 * 1024 * 1024
VMEM_LIMIT = V7X_VMEM_BYTES * 7 // 8

F32 = jnp.float32
BF16 = jnp.bfloat16


def _mm(a, b):
    return jnp.dot(a.astype(BF16), b.astype(BF16), preferred_element_type=F32)


def _mm_nt(a, b):
    return lax.dot_general(a.astype(BF16), b.astype(BF16), (((1,), (1,)), ((), ())),
                           preferred_element_type=F32)


def _sigmoid(x):
    return 0.5 * jnp.tanh(0.5 * x) + 0.5


def _in_proj_kernel(x_ref, nw_ref, w_ref, qw_ref, kw_ref, dnw_ref,
                    dnqkv_ref, dnz_ref, ba_ref, attg_ref, *att_refs_and_scratch):
    att_refs = att_refs_and_scratch[:len(DILATED_PATTERNS)]
    att_s, perm_s = att_refs_and_scratch[len(DILATED_PATTERNS):]
    tm = x_ref.shape[1]
    x = x_ref[0]
    ms = jnp.mean(x * x, axis=-1, keepdims=True)
    h = (x * lax.rsqrt(ms + EPS) * nw_ref[...]).astype(BF16)
    c0 = 3 * D_DN
    c1 = c0 + D_DN
    c2 = c1 + 3 * D_ATT
    c3 = c2 + D_ATT

    att = jnp.dot(h, w_ref[:, c1:c2], preferred_element_type=F32)
    head0 = lax.broadcasted_iota(jnp.int32, (tm, LANES), 1) < ATT_HEAD_DIM
    n_slab = 3 * D_ATT // LANES
    for j in range(n_slab):
        xs = att[:, j * LANES:(j + 1) * LANES]
        if j < 2 * D_ATT // LANES:
            is_q = j < D_ATT // LANES
            x2 = xs * xs
            s0 = jnp.sum(jnp.where(head0, x2, 0.0), axis=-1, keepdims=True)
            s1 = jnp.sum(jnp.where(head0, 0.0, x2), axis=-1, keepdims=True)
            ss = jnp.where(head0, s0, s1)
            wn = qw_ref[...] if is_q else kw_ref[...]
            xs = xs * lax.rsqrt(ss * (1.0 / ATT_HEAD_DIM) + EPS) * wn
        att_s[j] = xs
    for j in range(n_slab):
        cols = slice(j * LANES, (j + 1) * LANES)
        src_ref, r_prev = att_s.at[j], 1
        for level, (o_ref, (_, r)) in enumerate(zip(att_refs, DILATED_PATTERNS)):
            step = r // r_prev
            rows_per = tm // r
            keep = level + 1 < len(DILATED_PATTERNS) and r > 1
            for res in range(r):
                if r == 1:
                    piece = src_ref[...]
                else:
                    start = (res % r_prev) * (tm // r_prev) + res // r_prev
                    piece = src_ref[pl.ds(start, rows_per, stride=step), :]
                o_ref[0, res, :, cols] = piece.astype(BF16)
                if keep:
                    perm_s[j, pl.ds(res * rows_per, rows_per), :] = piece
            if keep:
                src_ref, r_prev = perm_s.at[j], r

    dnqkv_ref[0] = jnp.dot(h, w_ref[:, 0:c0], preferred_element_type=F32)
    z = jnp.dot(h, w_ref[:, c0:c1], preferred_element_type=F32)
    dnz_ref[0] = (z * _sigmoid(z) * dnw_ref[...]).astype(BF16)
    gate = jnp.dot(h, w_ref[:, c2:c3], preferred_element_type=F32)
    attg_ref[0] = (gate * _sigmoid(gate)).astype(BF16)
    ba_ref[0] = jnp.dot(h, w_ref[:, c3:c3 + LANES], preferred_element_type=F32)


def _in_proj(x, norm_w, w_all, qw, kw, dnw):
    bsz, seq, _ = x.shape
    tm = IN_PROJ_ROWS
    ncol = w_all.shape[1]
    row = lambda b, i: (b, i, 0)
    fixed = lambda b, i: (0, 0)
    att_specs = [pl.BlockSpec((1, r, tm // r, 3 * D_ATT), lambda b, i: (b, 0, i, 0))
                 for _, r in DILATED_PATTERNS]
    att_shapes = [jax.ShapeDtypeStruct((bsz, r, seq // r, 3 * D_ATT), BF16)
                  for _, r in DILATED_PATTERNS]
    return pl.pallas_call(
        _in_proj_kernel,
        grid=(bsz, seq // tm),
        in_specs=[pl.BlockSpec((1, tm, D_MODEL), row),
                  pl.BlockSpec((1, D_MODEL), fixed),
                  pl.BlockSpec((D_MODEL, ncol), fixed),
                  pl.BlockSpec((1, LANES), fixed),
                  pl.BlockSpec((1, LANES), fixed),
                  pl.BlockSpec((1, D_DN), fixed)],
        out_specs=[pl.BlockSpec((1, tm, 3 * D_DN), row),
                   pl.BlockSpec((1, tm, D_DN), row),
                   pl.BlockSpec((1, tm, LANES), row),
                   pl.BlockSpec((1, tm, D_ATT), row)] + att_specs,
        out_shape=[jax.ShapeDtypeStruct((bsz, seq, 3 * D_DN), F32),
                   jax.ShapeDtypeStruct((bsz, seq, D_DN), BF16),
                   jax.ShapeDtypeStruct((bsz, seq, LANES), F32),
                   jax.ShapeDtypeStruct((bsz, seq, D_ATT), BF16)] + att_shapes,
        scratch_shapes=[pltpu.VMEM((3 * D_ATT // LANES, tm, LANES), F32),
                        pltpu.VMEM((3 * D_ATT // LANES, tm, LANES), F32)],
        compiler_params=pltpu.CompilerParams(
            dimension_semantics=("arbitrary", "arbitrary"), vmem_limit_bytes=VMEM_LIMIT),
        name="in_proj",
    )(x, norm_w, w_all, qw, kw, dnw)


def _t5_bucket(dist):
    max_exact = N_BUCKETS // 2
    d = np.maximum(dist, 1).astype(np.float64)
    large = max_exact + (np.log(d / max_exact) / math.log(MAX_DISTANCE / max_exact)
                         * (N_BUCKETS - max_exact)).astype(np.int32)
    large = np.minimum(large, N_BUCKETS - 1)
    return np.where(dist < max_exact, dist, large).astype(np.int32)


def _bucket_maps():
    maps = []
    for window, r in DILATED_PATTERNS:
        w_steps = window // r
        assert w_steps == BLK
        qi = np.arange(BLK)[:, None]
        kj = np.arange(2 * BLK)[None, :]
        step = qi - kj + BLK
        band = (step >= 0) & (step <= w_steps)
        buckets = _t5_bucket(np.clip(step, 0, None) * r)
        maps.append(np.where(band, buckets, -1).astype(np.int32))
    return np.stack(maps)


def _fill_bias_tables(rel_ref, bkt_ref, bias_s, pair):
    col = lax.broadcasted_iota(jnp.int32, (BLK, 2 * BLK), 1)
    for p in range(len(DILATED_PATTERNS)):
        bkt = bkt_ref[p]
        for hh in range(2):
            h = 2 * pair + hh
            acc = jnp.full(bkt.shape, NEG, F32)
            for b in range(N_BUCKETS):
                acc = jnp.where(bkt == b, rel_ref[h, b] * LOG2E, acc)
            bias_s[p, hh, 0] = acc
            bias_s[p, hh, 1] = jnp.where(col < BLK, NEG, acc)
            bias_s[p, hh, 2] = jnp.concatenate([acc[:, BLK:], jnp.full((BLK, BLK), NEG, F32)], axis=1)


def _dn_kernel(*refs, tile):
    nh = DN_HEADS
    n_stream = 3 * nh
    x_refs = refs[:n_stream]
    (cw_ref, ba_ref, z_ref, alog_ref, dtb_ref, cf_ref, cb_ref,
     o_ref, state_s, xs) = refs[n_stream:]
    n_chunk = SUPER // DN_CHUNK
    heads = range(nh)
    lead = SUBLANES

    nb = DN_BATCH

    @pl.when(pl.program_id(1) == 0)
    def _():
        state_s[...] = jnp.zeros(state_s.shape, F32)
        xs[:, 0:lead, :] = jnp.zeros((nb * n_stream, lead, LANES), F32)

    for bb in range(nb):
        for j in range(n_stream):
            xs[bb * n_stream + j, lead:lead + SUPER, :] = x_refs[j][bb, 0:SUPER, :]

    eye = lambda: cf_ref[0]
    neg_outside_incl = lambda: cf_ref[1]
    neg_in_strict = lambda: cf_ref[2]
    in_sub = lambda: cf_ref[3]
    cum_mat = lambda: cb_ref[0]
    chunk_cols = lambda c: cb_ref[1 + c]
    is_beta_row = lax.broadcasted_iota(jnp.int32, (SUBLANES, SUPER), 0) < nh

    a_coef = -LOG2E * jnp.exp(alog_ref[...])
    dt_b = dtb_ref[...]

    def conv_silu(bb, j, r0, in_xs):
        def tap(s):
            w = 0.5 * cw_ref[CONV_WIDTH - 1 - s, j]
            if in_xs:
                rows = xs[bb * n_stream + j, lead - s:lead - s + SUPER, :]
            else:
                rows = x_refs[j][bb, pl.ds(r0 - s, SUPER), :]
            return rows * jnp.tile(w, (SUPER // SUBLANES, 1))
        h = tap(0)
        for s in range(1, CONV_WIDTH):
            h = h + tap(s)
        return h * jnp.tanh(h) + h

    def l2n(x):
        return x * lax.rsqrt(jnp.sum(x * x, axis=-1, keepdims=True) + EPS)

    def lane_bcast(x, j):
        return jnp.broadcast_to(x[:, j:j + 1], x.shape)

    def row_bcast(x, i, n_rows):
        return jnp.broadcast_to(x[i:i + 1, :], (n_rows, x.shape[1]))

    def gate_rows(bb, r0):
        pre = ba_ref[bb, pl.ds(r0, SUPER), :].T[0:SUBLANES, :]
        xg = pre + dt_b
        g = a_coef * (jnp.maximum(xg, 0.0) + jnp.log(1.0 + jnp.exp(-jnp.abs(xg))))
        g_hi = g.astype(BF16)
        g_r = g - g_hi.astype(F32)
        g_mid = g_r.astype(BF16)
        g_lo = (g_r - g_mid.astype(F32)).astype(BF16)
        cs = jnp.dot(jnp.concatenate([g_hi, g_mid, g_lo], axis=0), cum_mat(),
                     preferred_element_type=F32)
        gc_r = cs[0:SUBLANES] + cs[SUBLANES:2 * SUBLANES] + cs[2 * SUBLANES:3 * SUBLANES]
        gates = jnp.where(is_beta_row, _sigmoid(pre), gc_r)
        gates_c = jnp.concatenate([gates, jnp.zeros((SUPER - SUBLANES, SUPER), F32)], axis=0).T
        return gc_r, gates_c

    def body(it, carry, first=False):
        if first:
            r0s = [u * SUPER for u in range(DN_UNROLL)]
        else:
            r0s = [pl.multiple_of((it * DN_UNROLL + u) * SUPER, SUPER) for u in range(DN_UNROLL)]
        in_xs = [first and u == 0 for u in range(DN_UNROLL)]
        items = [(u, bb, h) for u in range(DN_UNROLL) for bb in range(nb) for h in heads]
        idx = range(len(items))
        gate = {(u, bb): gate_rows(bb, r0s[u]) for u in range(DN_UNROLL) for bb in range(nb)}

        q = [l2n(conv_silu(bb, h, r0s[u], in_xs[u])) * (DN_HEAD_DIM ** -0.5) for u, bb, h in items]
        k = [l2n(conv_silu(bb, nh + h, r0s[u], in_xs[u])) for u, bb, h in items]
        v = [conv_silu(bb, 2 * nh + h, r0s[u], in_xs[u]) for u, bb, h in items]

        beta = [lane_bcast(gate[u, bb][1], h) for u, bb, h in items]
        gc = [lane_bcast(gate[u, bb][1], nh + h) for u, bb, h in items]
        decay = [jnp.exp2(gc[i] - row_bcast(gate[u, bb][0], nh + h, SUPER) + neg_outside_incl())
                 for i, (u, bb, h) in enumerate(items)]
        gc_last = [jnp.concatenate([row_bcast(gc[i], c * DN_CHUNK + DN_CHUNK - 1, DN_CHUNK)
                                    for c in range(n_chunk)], axis=0) for i in idx]
        e_gc = [jnp.exp2(gc[i]) for i in idx]
        kb = [k[i] * beta[i] for i in idx]

        kq = [_mm_nt(jnp.concatenate([kb[i], q[i]], axis=0), k[i]) for i in idx]
        neg_a = [kq[i][:SUPER] * decay[i] * neg_in_strict() for i in idx]
        attn = [(kq[i][SUPER:] * decay[i]).astype(BF16) for i in idx]
        rhs = [jnp.concatenate([v[i] * beta[i], kb[i] * e_gc[i]], axis=1).astype(BF16) for i in idx]

        x1 = [neg_a[i] * in_sub() for i in idx]
        neg_l = [(neg_a[i] - x1[i]).astype(BF16) for i in idx]
        x1b = [x1[i].astype(BF16) for i in idx]
        x2 = [_mm(x1b[i], x1b[i]) for i in idx]
        x2b = [x2[i].astype(BF16) for i in idx]
        x4 = [_mm(x2b[i], x2b[i]) for i in idx]
        p1 = [eye() + x1[i] + x2[i] + _mm(x1b[i], x2b[i]) for i in idx]
        t_d = [p1[i] + _mm(p1[i], x4[i]) for i in idx]
        t_db = [t_d[i].astype(BF16) for i in idx]
        y1 = [_mm(t_db[i], neg_l[i]) for i in idx]
        td_rhs = [_mm(t_db[i], rhs[i]).astype(BF16) for i in idx]
        y1b = [y1[i].astype(BF16) for i in idx]
        y2 = [_mm(y1b[i], y1b[i]) for i in idx]
        y2b = [y2[i].astype(BF16) for i in idx]
        y4 = [_mm(y2b[i], y2b[i]) for i in idx]
        q1 = [eye() + y1[i] + y2[i] + _mm(y1b[i], y2b[i]) for i in idx]
        q2 = [q1[i] + _mm(q1[i], y4[i]) for i in idx]
        y_pow, covered = y4, 8
        while covered < DN_CHUNK // SUB:
            y_pow_b = [y_pow[i].astype(BF16) for i in idx]
            y_pow = [_mm(y_pow_b[i], y_pow_b[i]) for i in idx]
            q2 = [q2[i] + _mm(q2[i], y_pow[i]) for i in idx]
            covered *= 2

        uw = [_mm(q2[i], td_rhs[i]).astype(BF16) for i in idx]
        aw = [_mm(attn[i], uw[i]) for i in idx]
        q_t = [(q[i] * e_gc[i] - aw[i][:, LANES:]).astype(BF16) for i in idx]
        kt_t = [(k[i] * jnp.exp2(gc_last[i] - gc[i])).T.astype(BF16) for i in idx]
        kw = [[_mm(kt_t[i] * chunk_cols(c) if n_chunk > 1 else kt_t[i], uw[i])
               for c in range(n_chunk)] for i in idx]

        chains = [(bb, h) for bb in range(nb) for h in heads]
        state = [state_s[bb * nh + h] for bb, h in chains]
        for u in range(DN_UNROLL):
            outs = [[] for _ in chains]
            for c in range(n_chunk):
                sl = slice(c * DN_CHUNK, (c + 1) * DN_CHUNK)
                s_bf = [st.astype(BF16) for st in state]
                for j in range(len(chains)):
                    i = u * len(chains) + j
                    outs[j].append(_mm(q_t[i][sl], s_bf[j]) + aw[i][sl, :LANES])
                for j in range(len(chains)):
                    i = u * len(chains) + j
                    g_last = jnp.exp2(row_bcast(gc[i], c * DN_CHUNK + DN_CHUNK - 1, LANES))
                    state[j] = (state[j] * g_last - _mm(kw[i][c][:, LANES:], s_bf[j])
                                + kw[i][c][:, :LANES])
            rows = pl.ds(r0s[u], SUPER)
            for j, (bb, h) in enumerate(chains):
                o = jnp.concatenate(outs[j], axis=0)
                ms = jnp.mean(o * o, axis=-1, keepdims=True)
                z = z_ref[bb, rows, h * LANES:(h + 1) * LANES].astype(F32)
                y = o * lax.rsqrt(ms + EPS) * z
                o_ref[bb, rows, h * LANES:(h + 1) * LANES] = y.astype(o_ref.dtype)
        for j, (bb, h) in enumerate(chains):
            state_s[bb * nh + h] = state[j]
        return carry

    body(0, 0, first=True)
    lax.fori_loop(1, tile // (SUPER * DN_UNROLL), body, 0)
    for bb in range(nb):
        for j in range(n_stream):
            xs[bb * n_stream + j, 0:lead, :] = x_refs[j][bb, tile - lead:tile, :]


def _dn_constants():
    ri = np.arange(SUPER)[:, None]
    ci = np.arange(SUPER)[None, :]
    same_chunk = (ri // DN_CHUNK) == (ci // DN_CHUNK)
    cf = np.stack([
        (ri == ci).astype(np.float32),
        np.where(same_chunk & (ri >= ci), 0.0, NEG).astype(np.float32),
        -(same_chunk & (ri > ci)).astype(np.float32),
        ((ri // SUB) == (ci // SUB)).astype(np.float32)])
    cb = np.stack([(same_chunk & (ri <= ci)).astype(np.float32)]
                  + [np.broadcast_to((ci // DN_CHUNK) == c, (SUPER, SUPER)).astype(np.float32)
                     for c in range(SUPER // DN_CHUNK)])
    return jnp.asarray(cf), jnp.asarray(cb, dtype=BF16)


def _deltanet(dn_qkv, dn_z, ba, conv_w, a_log, dt_bias):
    bsz, seq, _ = dn_qkv.shape
    tile = DN_TILE
    assert bsz % DN_BATCH == 0
    n_stream = 3 * DN_HEADS
    seq_tile = lambda b, t: (b, t, 0)
    fixed = lambda b, t: (0, 0)
    cf, cb = _dn_constants()
    cw_tiles = jnp.broadcast_to(conv_w.astype(F32).reshape(CONV_WIDTH, n_stream, 1, LANES),
                                (CONV_WIDTH, n_stream, SUBLANES, LANES))

    def slab(j):
        return pl.BlockSpec((DN_BATCH, tile, LANES), lambda b, t: (b, t, j))

    def whole(a):
        return pl.BlockSpec(a.shape, lambda b, t: (0,) * a.ndim)

    return pl.pallas_call(
        functools.partial(_dn_kernel, tile=tile),
        grid=(bsz // DN_BATCH, seq // tile),
        in_specs=[slab(j) for j in range(n_stream)]
        + [whole(cw_tiles),
           pl.BlockSpec((DN_BATCH, tile, LANES), seq_tile),
           pl.BlockSpec((DN_BATCH, tile, D_DN), seq_tile),
           pl.BlockSpec((SUBLANES, LANES), fixed),
           pl.BlockSpec((SUBLANES, LANES), fixed),
           whole(cf),
           whole(cb)],
        out_specs=pl.BlockSpec((DN_BATCH, tile, D_DN), seq_tile),
        out_shape=jax.ShapeDtypeStruct((bsz, seq, D_DN), BF16),
        scratch_shapes=[pltpu.VMEM((DN_BATCH * DN_HEADS, DN_HEAD_DIM, DN_HEAD_DIM), F32),
                        pltpu.VMEM((DN_BATCH * n_stream, SUBLANES + SUPER, LANES), F32)],
        compiler_params=pltpu.CompilerParams(
            dimension_semantics=("arbitrary", "arbitrary"), vmem_limit_bytes=VMEM_LIMIT),
        name="deltanet",
    )(*([dn_qkv] * n_stream), cw_tiles, ba, dn_z, a_log, dt_bias, cf, cb)


def _part_pitch(seq, r):
    return seq // r + 1 if r >= 2 * SUBLANES else None


def _part_rows(seq, r):
    pitch = _part_pitch(seq, r)
    return seq if pitch is None else -(-(r * pitch) // SUBLANES) * SUBLANES


def _att_kernel(*refs, seq):
    n_pat = len(DILATED_PATTERNS)
    qkv_refs = [refs[3 * p:3 * p + 3] for p in range(n_pat)]
    g_ref, rel_ref, bkt_ref, o_ref = refs[3 * n_pat:3 * n_pat + 4]
    part_s = refs[3 * n_pat + 4:3 * n_pat + 4 + n_pat]
    bias_ref = refs[3 * n_pat + 4 + n_pat]

    @pl.when(pl.program_id(1) == 0)
    def _():
        _fill_bias_tables(rel_ref, bkt_ref, bias_ref, pl.program_id(0))

    tile = ATT_MERGE_ROWS
    lane = lax.broadcasted_iota(jnp.int32, (BLK, LANES), 1)
    head0 = lane < ATT_HEAD_DIM
    ones_v = jnp.ones((2 * BLK, LANES), BF16)
    zero_q = jnp.zeros((BLK, LANES), BF16)

    def store_part(p, r, res, n, vals):
        pitch = _part_pitch(seq, r)
        if pitch is None:
            tok0 = res + n * (BLK * r)
            rows = pl.ds(tok0, BLK) if r == 1 else pl.ds(tok0, BLK, stride=r)
        else:
            rows = pl.ds(res * pitch + n * BLK, BLK)
        for a, val in enumerate(vals):
            part_s[p][a, rows, :] = val

    def load_part(p, r, a, t0):
        pitch = _part_pitch(seq, r)
        if pitch is None:
            return part_s[p][a, pl.ds(t0, tile), :]
        m0 = t0 // r
        return jnp.concatenate([part_s[p][a, pl.ds(m0 + jj, r, stride=pitch), :]
                                for jj in range(tile // r)], axis=0)

    def group(p, r, n_blk, bi0):
        q_ref, k_ref, v_ref = qkv_refs[p]
        idx = []
        for g in range(ATT_GROUP):
            bi = bi0 + g
            base = pl.multiple_of(bi * BLK, BLK)
            kbase = pl.multiple_of(jnp.maximum(base - BLK, 0), BLK)
            n = bi % n_blk
            variant = jnp.where(bi == 0, 2, jnp.where(n == 0, 1, 0))
            idx.append((base, kbase, variant, (bi // n_blk, n)))
        scores = []
        for base, kbase, variant, _ in idx:
            q = q_ref[0, pl.ds(base, BLK), :]
            q2 = jnp.concatenate([jnp.where(head0, q, zero_q), jnp.where(head0, zero_q, q)], axis=0)
            k = k_ref[0, pl.ds(kbase, 2 * BLK), :]
            bias = jnp.concatenate([bias_ref[p, 0, variant], bias_ref[p, 1, variant]], axis=0)
            scores.append(lax.dot_general(q2, k, (((1,), (1,)), ((), ())),
                                          preferred_element_type=F32) + bias)
        maxes, probs = [], []
        for s in scores:
            m = jnp.max(s, axis=-1, keepdims=True)
            maxes.append(m)
            probs.append(jnp.exp2(s - m).astype(BF16))
        for (base, kbase, variant, (res, n)), m, e in zip(idx, maxes, probs):
            v2 = jnp.concatenate([v_ref[0, pl.ds(kbase, 2 * BLK), :], ones_v], axis=1)
            pv = jnp.dot(e, v2, preferred_element_type=F32)
            store_part(p, r, res, n,
                       (jnp.where(head0, m[:BLK], m[BLK:]),
                        jnp.where(head0, pv[:BLK, LANES:], pv[BLK:, LANES:]),
                        jnp.where(head0, pv[:BLK, :LANES], pv[BLK:, :LANES])))

    for p, (window, r) in enumerate(DILATED_PATTERNS):
        n_blk = seq // (r * BLK)

        def group_body(i, carry, p=p, r=r, n_blk=n_blk):
            group(p, r, n_blk, i * ATT_GROUP)
            return carry
        lax.fori_loop(0, seq // (BLK * ATT_GROUP), group_body, 0)

    def out_body(i, carry):
        t0 = pl.multiple_of(i * tile, tile)
        rows = pl.ds(t0, tile)
        part = [[load_part(p, r, a, t0) for a in range(3)]
                for p, (_, r) in enumerate(DILATED_PATTERNS)]
        m = functools.reduce(jnp.maximum, [pt[0] for pt in part])
        w = [jnp.exp2(pt[0] - m) for pt in part]
        den = sum(wp * pt[1] for wp, pt in zip(w, part))
        num = sum(wp * pt[2] for wp, pt in zip(w, part))
        g = g_ref[0, rows, :].astype(F32)
        o_ref[0, rows, :] = (num / den * g).astype(o_ref.dtype)
        return carry
    lax.fori_loop(0, seq // tile, out_body, 0)


def _dilated_attn(att_perm, att_g, rel_bias):
    bsz, seq, _ = att_g.shape
    n_pair = ATT_HEADS // 2
    n_pat = len(DILATED_PATTERNS)
    bkt = jnp.asarray(_bucket_maps())

    def col(off):
        return pl.BlockSpec((1, seq, LANES), lambda j, b: (b, 0, off + j))

    qkv_specs, qkv_args = [], []
    for a in att_perm:
        qkv_specs += [col(0), col(n_pair), col(2 * n_pair)]
        qkv_args += [a, a, a]
    return pl.pallas_call(
        functools.partial(_att_kernel, seq=seq),
        grid=(n_pair, bsz),
        in_specs=qkv_specs
        + [col(0),
           pl.BlockSpec(memory_space=pltpu.SMEM),
           pl.BlockSpec(bkt.shape, lambda j, b: (0, 0, 0))],
        out_specs=col(0),
        out_shape=jax.ShapeDtypeStruct((bsz, seq, D_ATT), BF16),
        scratch_shapes=[pltpu.VMEM((3, _part_rows(seq, r), LANES), F32)
                        for _, r in DILATED_PATTERNS]
        + [pltpu.VMEM((n_pat, 2, N_BIAS_VARIANTS, BLK, 2 * BLK), F32)],
        compiler_params=pltpu.CompilerParams(
            dimension_semantics=("arbitrary", "arbitrary"), vmem_limit_bytes=VMEM_LIMIT),
        name="dilated_attn",
    )(*qkv_args, att_g, rel_bias, bkt)


def _out_proj_kernel(x_ref, ydn_ref, yatt_ref, w_ref, o_ref, wb_s):
    @pl.when(pl.program_id(0) == 0)
    def _():
        wb_s[...] = w_ref[...].astype(BF16)

    o_ref[...] = (x_ref[...]
                  + jnp.dot(ydn_ref[...], wb_s[0:D_DN, :], preferred_element_type=F32)
                  + jnp.dot(yatt_ref[...], wb_s[D_DN:D_MIX, :], preferred_element_type=F32))


def _out_proj(x2d, y_dn, y_att, w_out):
    n = x2d.shape[0]
    tm = OUT_PROJ_ROWS
    row = lambda i: (i, 0)
    return pl.pallas_call(
        _out_proj_kernel,
        grid=(n // tm,),
        in_specs=[pl.BlockSpec((tm, D_MODEL), row),
                  pl.BlockSpec((tm, D_DN), row),
                  pl.BlockSpec((tm, D_ATT), row),
                  pl.BlockSpec((D_MIX, D_MODEL), lambda i: (0, 0))],
        out_specs=pl.BlockSpec((tm, D_MODEL), row),
        out_shape=jax.ShapeDtypeStruct((n, D_MODEL), F32),
        scratch_shapes=[pltpu.VMEM((D_MIX, D_MODEL), BF16)],
        compiler_params=pltpu.CompilerParams(
            dimension_semantics=("arbitrary",), vmem_limit_bytes=VMEM_LIMIT),
        name="out_proj",
    )(x2d, y_dn, y_att, w_out)


def _gate_rows(v):
    col = jnp.pad(v.reshape(-1, 1).astype(F32), ((DN_HEADS, SUBLANES - 2 * DN_HEADS), (0, 0)))
    return jnp.broadcast_to(col, (SUBLANES, LANES))


def _layer(x, norm_w, w_in, conv_w, a_log, dt_bias, dn_norm_w, q_norm_w, k_norm_w, rel_bias, w_out):
    bsz, seq, _ = x.shape
    x2d = x.reshape(bsz * seq, D_MODEL)
    c_dn = 4 * D_DN
    c_ba = c_dn + 2 * DN_HEADS
    w_ba = jnp.pad(w_in[:, c_dn:c_ba], ((0, 0), (0, LANES - 2 * DN_HEADS)))
    w_all = jnp.concatenate([w_in[:, :c_dn], w_in[:, c_ba:], w_ba], axis=1).astype(BF16)

    qw = jnp.tile(q_norm_w.reshape(1, ATT_HEAD_DIM).astype(F32), (1, 2)) * (ATT_HEAD_DIM ** -0.5 * LOG2E)
    kw = jnp.tile(k_norm_w.reshape(1, ATT_HEAD_DIM).astype(F32), (1, 2))
    dnw = jnp.tile(dn_norm_w.reshape(1, DN_HEAD_DIM).astype(F32), (1, DN_HEADS))
    dn_qkv, dn_z, ba, att_g, *att_perm = _in_proj(x, norm_w.reshape(1, D_MODEL), w_all, qw, kw, dnw)
    att_perm = [a.reshape(bsz, seq, 3 * D_ATT) for a in att_perm]

    y_dn = _deltanet(dn_qkv, dn_z, ba, conv_w, _gate_rows(a_log), _gate_rows(dt_bias))
    y_att = _dilated_attn(att_perm, att_g, rel_bias)
    out = _out_proj(x2d, y_dn.reshape(bsz * seq, D_DN), y_att.reshape(bsz * seq, D_ATT),
                    w_out.astype(F32))
    return out.reshape(bsz, seq, D_MODEL)


def kernel(x, norm_w, w_in, conv_w, a_log, dt_bias, dn_norm_w, q_norm_w, k_norm_w, rel_bias, w_out):
    rel_bias = rel_bias.astype(F32)
    for layer in range(norm_w.shape[0]):
        x = _layer(x, norm_w[layer], w_in[layer], conv_w[layer], a_log[layer], dt_bias[layer],
                   dn_norm_w[layer], q_norm_w[layer], k_norm_w[layer], rel_bias, w_out[layer])
    return x
```
